```python
import math
import jax, jax.numpy as jnp
from jax import lax
import numpy as np

D_MODEL = 1024
BATCH = 16
SEQ = 2048
DEPTH = 2

CTX_LEN = 256
GRID_W = 64
Q_BLOCK = 128
ROPE_THETA = 10000.0
EPS = 1e-6
HEAD_DIM = 64
N_BRANCH = 4
BRANCH_WIDTH = 4 * HEAD_DIM
MLA_HEADS = 4
MLA_Q_LORA = 192
MLA_KV_LORA = 128
MLA_NOPE = 64
MLA_ROPE = 32
MLA_V = 64
MLA_QK = MLA_NOPE + MLA_ROPE
NA_HEADS = 4
NA_ROWS = 8
NA_COLS = 16
DIFF_HEADS = 4
DIFF_DIM = 32
GQA_HEADS = 4
GQA_KV_HEADS = 2
D_FF = 4 * D_MODEL
SPLIT_SIZES = (MLA_Q_LORA, MLA_KV_LORA, MLA_ROPE,
               3 * NA_HEADS * HEAD_DIM,
               3 * DIFF_HEADS * 2 * DIFF_DIM,
               (GQA_HEADS + 2 * GQA_KV_HEADS) * HEAD_DIM,
               N_BRANCH * D_MODEL)
IN_COLS = sum(SPLIT_SIZES)

kernel_name = 'hybrid_gated_branch_dit_block'


def rmsnorm(x, g):
    xf = x.astype(jnp.float32)
    y = xf * lax.rsqrt(jnp.mean(xf * xf, axis=-1, keepdims=True) + EPS)
    return (y * g.astype(jnp.float32)).astype(x.dtype)


def modulate(h, shift, scale):
    return h * (1.0 + scale) + shift


def split_last(z, sizes):
    parts, start = [], 0
    for n in sizes:
        parts.append(z[..., start:start + n])
        start += n
    return parts


def flat_heads(o):
    return o.reshape(*o.shape[:2], -1)


def rope_tables(rows, cols, rot_dim):
    n = rot_dim // 4
    inv_freq = jnp.power(ROPE_THETA, -jnp.arange(n, dtype=jnp.float32) / n)
    ang_r = rows.astype(jnp.float32)[:, None] * inv_freq
    ang_c = cols.astype(jnp.float32)[:, None] * inv_freq
    ang = jnp.concatenate([ang_r, ang_r, ang_c, ang_c], axis=-1)
    return jnp.cos(ang), jnp.sin(ang)


def apply_rope(t, cos, sin):
    shape = (cos.shape[0],) + (1,) * (t.ndim - 3) + (cos.shape[-1],)
    cos, sin = cos.reshape(shape), sin.reshape(shape)
    t1, t2, t3, t4 = jnp.split(t, 4, axis=-1)
    rot = jnp.concatenate([-t2, t1, -t4, t3], axis=-1)
    return (t * cos + rot * sin).astype(t.dtype)


def sweep_query_blocks(fn, q):
    B, S = q.shape[:2]
    nb = S // Q_BLOCK
    qb = jnp.moveaxis(q.reshape(B, nb, Q_BLOCK, *q.shape[2:]), 1, 0)
    out = lax.map(fn, qb)
    return jnp.moveaxis(out, 0, 1).reshape(B, S, *out.shape[3:])


def gqa_attention(q, k, v):
    B, _, H, d = q.shape
    n_kv = k.shape[2]
    G = H // n_kv
    scale = d ** -0.5

    def block(qb):
        qb = qb.reshape(B, Q_BLOCK, n_kv, G, d)
        s = jnp.einsum('bqngd,bknd->bngqk', qb, k).astype(jnp.float32) * scale
        p = jax.nn.softmax(s, axis=-1).astype(v.dtype)
        o = jnp.einsum('bngqk,bknd->bqngd', p, v)
        return o.reshape(B, Q_BLOCK, H, v.shape[-1])

    return sweep_query_blocks(block, q)


def mla_mixer(zl, zc, rope, g_qa, w_uq, g_kva, w_ukv, g_q, g_k, need_ctx):
    cos, sin = rope

    def queries(cq):
        q = (rmsnorm(cq, g_qa) @ w_uq).reshape(*cq.shape[:2], MLA_HEADS, MLA_QK)
        return rmsnorm(q, g_q)

    def keys_values(ckv, k_rope):
        kv = (rmsnorm(ckv, g_kva) @ w_ukv).reshape(*ckv.shape[:2], MLA_HEADS, MLA_NOPE + MLA_V)
        k_rope = jnp.broadcast_to(k_rope[:, :, None, :], (*k_rope.shape[:2], MLA_HEADS, MLA_ROPE))
        k = jnp.concatenate([kv[..., :MLA_NOPE], k_rope], axis=-1)
        return rmsnorm(k, g_k), kv[..., MLA_NOPE:]

    def rotate(t):
        return jnp.concatenate([t[..., :MLA_NOPE], apply_rope(t[..., MLA_NOPE:], cos, sin)], axis=-1)

    cq, ckv, kr = zl
    cq_c, ckv_c, kr_c = zc
    ql = rotate(queries(cq))
    kl, vl = keys_values(ckv, kr)
    kl = rotate(kl)
    k_ctx, v_ctx = keys_values(ckv_c, kr_c)
    o = gqa_attention(ql, jnp.concatenate([kl, k_ctx], axis=1), jnp.concatenate([vl, v_ctx], axis=1))
    o_c = flat_heads(gqa_attention(queries(cq_c), k_ctx, v_ctx)) if need_ctx else None
    return flat_heads(o), o_c


def na_mixer(zl, zc, g_q, g_k, rpb, need_ctx):
    def qkv(z):
        z = z.reshape(*z.shape[:2], 3, NA_HEADS, HEAD_DIM)
        return rmsnorm(z[:, :, 0], g_q), rmsnorm(z[:, :, 1], g_k), z[:, :, 2]

    ql, kl, vl = qkv(zl)
    qc, k_ctx, v_ctx = qkv(zc)
    B, S = ql.shape[:2]
    n_rows = S // GRID_W
    win_r = min(NA_ROWS, n_rows)
    win_c = NA_COLS
    n_nb = win_r * win_c
    scale = HEAD_DIM ** -0.5

    def to_grid(t):
        return t.reshape(B, n_rows, GRID_W, NA_HEADS, HEAD_DIM)

    k_grid, v_grid = to_grid(kl), to_grid(vl)
    row_start = jnp.clip(jnp.arange(n_rows) - win_r // 2, 0, n_rows - win_r)
    col = jnp.arange(GRID_W)
    col_idx = jnp.clip(col - win_c // 2, 0, GRID_W - win_c)[:, None] + jnp.arange(win_c)
    dc = col_idx - col[:, None] + (NA_COLS - 1)

    def row_block(args):
        q_row, r = args
        r0 = row_start[r]
        k_nb = jnp.take(lax.dynamic_slice_in_dim(k_grid, r0, win_r, axis=1), col_idx, axis=2)
        v_nb = jnp.take(lax.dynamic_slice_in_dim(v_grid, r0, win_r, axis=1), col_idx, axis=2)
        dr = r0 + jnp.arange(win_r) - r + (NA_ROWS - 1)
        bias = rpb[:, dr[:, None, None], dc[None, :, :]].transpose(0, 2, 1, 3)
        s_nb = jnp.einsum('bqhd,brqchd->bhqrc', q_row, k_nb).astype(jnp.float32) * scale + bias.astype(jnp.float32)
        s_ctx = jnp.einsum('bqhd,bkhd->bhqk', q_row, k_ctx).astype(jnp.float32) * scale
        p = jax.nn.softmax(jnp.concatenate([s_nb.reshape(B, NA_HEADS, GRID_W, n_nb), s_ctx], axis=-1), axis=-1)
        p = p.astype(v_nb.dtype)
        p_nb = p[..., :n_nb].reshape(B, NA_HEADS, GRID_W, win_r, win_c)
        return (jnp.einsum('bhqrc,brqchd->bqhd', p_nb, v_nb)
                + jnp.einsum('bhqk,bkhd->bqhd', p[..., n_nb:], v_ctx))

    out = lax.map(row_block, (jnp.moveaxis(to_grid(ql), 1, 0), jnp.arange(n_rows)))
    o = jnp.moveaxis(out, 0, 1).reshape(B, S, NA_HEADS * HEAD_DIM)
    o_c = flat_heads(gqa_attention(qc, k_ctx, v_ctx)) if need_ctx else None
    return o, o_c


def diff_mixer(zl, zc, rope, g_q, g_k, lq1, lk1, lq2, lk2, g_sub, lam_init, need_ctx):
    cos, sin = rope
    width = DIFF_HEADS * 2 * DIFF_DIM

    def qk(z, which, g):
        t = z[..., which * width:(which + 1) * width].reshape(*z.shape[:2], DIFF_HEADS, 2, DIFF_DIM)
        return rmsnorm(t, g)

    def vals(z):
        return z[..., 2 * width:].reshape(*z.shape[:2], DIFF_HEADS, 2 * DIFF_DIM)

    lam = (jnp.exp(jnp.sum(lq1.astype(jnp.float32) * lk1.astype(jnp.float32)))
           - jnp.exp(jnp.sum(lq2.astype(jnp.float32) * lk2.astype(jnp.float32))) + lam_init)
    scale = DIFF_DIM ** -0.5

    def attend(q, k, v):
        def block(qb):
            s = jnp.einsum('bqhmd,bkhmd->bhmqk', qb, k).astype(jnp.float32) * scale
            p = jax.nn.softmax(s, axis=-1)
            p_diff = (p[:, :, 0] - lam * p[:, :, 1]).astype(v.dtype)
            return jnp.einsum('bhqk,bkhd->bqhd', p_diff, v)
        o = rmsnorm(sweep_query_blocks(block, q), g_sub) * (1.0 - lam_init)
        return flat_heads(o)

    ql = apply_rope(qk(zl, 0, g_q), cos, sin)
    kl = apply_rope(qk(zl, 1, g_k), cos, sin)
    vl = vals(zl)
    k_ctx, v_ctx = qk(zc, 1, g_k), vals(zc)
    o = attend(ql, jnp.concatenate([kl, k_ctx], axis=1), jnp.concatenate([vl, v_ctx], axis=1))
    o_c = attend(qk(zc, 0, g_q), k_ctx, v_ctx) if need_ctx else None
    return o, o_c


def gqa_mixer(zl, zc, rope, g_q, g_k, need_ctx):
    cos, sin = rope
    nq, nkv = GQA_HEADS * HEAD_DIM, GQA_KV_HEADS * HEAD_DIM

    def qkv(z):
        q = rmsnorm(z[..., :nq].reshape(*z.shape[:2], GQA_HEADS, HEAD_DIM), g_q)
        k = rmsnorm(z[..., nq:nq + nkv].reshape(*z.shape[:2], GQA_KV_HEADS, HEAD_DIM), g_k)
        v = z[..., nq + nkv:].reshape(*z.shape[:2], GQA_KV_HEADS, HEAD_DIM)
        return q, k, v

    ql, kl, vl = qkv(zl)
    qc, k_ctx, v_ctx = qkv(zc)
    ql, kl = apply_rope(ql, cos, sin), apply_rope(kl, cos, sin)
    o = gqa_attention(ql, jnp.concatenate([kl, k_ctx], axis=1), jnp.concatenate([vl, v_ctx], axis=1))
    o_c = flat_heads(gqa_attention(qc, k_ctx, v_ctx)) if need_ctx else None
    return flat_heads(o), o_c


def merge_branches(branches, z_gate, w_branch, w_out):
    o = jnp.stack(branches, axis=2)
    proj = jnp.einsum('bsnw,nwd->bsnd', o, w_branch)
    gate = jax.nn.sigmoid(z_gate.reshape(*z_gate.shape[:2], N_BRANCH, D_MODEL))
    return jnp.sum(gate * proj, axis=2) @ w_out


def sq_relu_mlp(h, w_up, w_down):
    return jnp.square(jax.nn.relu(h @ w_up)) @ w_down


def setup_inputs(seed: int = 0) -> dict:
    key = jax.random.key(seed)
    k = jax.random.split(key, 31)
    L, D = DEPTH, D_MODEL

    def nrm(i, shape, scale):
        return scale * jax.random.normal(k[i], shape, jnp.float32)

    def gain(i, shape):
        return 1.0 + nrm(i, shape, 0.1)

    return {
        'x': nrm(0, (BATCH, SEQ, D), 1.0),
        'c': nrm(1, (BATCH, D), 1.0),
        'ctx': nrm(2, (BATCH, CTX_LEN, D), 1.0),
        'c_ctx': nrm(3, (D,), 1.0),
        'w_ada': nrm(4, (L, D, 6 * D), 0.5 * D ** -0.5),
        'b_ada': nrm(5, (L, 6 * D), 0.02),
        'g_norm1': gain(6, (L, D)),
        'g_norm2': gain(7, (L, D)),
        'w_in': nrm(8, (L, D, IN_COLS), D ** -0.5),
        'g_mla_qa': gain(9, (L, MLA_Q_LORA)),
        'w_mla_uq': nrm(10, (L, MLA_Q_LORA, MLA_HEADS * MLA_QK), MLA_Q_LORA ** -0.5),
        'g_mla_kva': gain(11, (L, MLA_KV_LORA)),
        'w_mla_ukv': nrm(12, (L, MLA_KV_LORA, MLA_HEADS * (MLA_NOPE + MLA_V)), MLA_KV_LORA ** -0.5),
        'g_mla_q': gain(13, (L, MLA_QK)),
        'g_mla_k': gain(14, (L, MLA_QK)),
        'g_na_q': gain(15, (L, HEAD_DIM)),
        'g_na_k': gain(16, (L, HEAD_DIM)),
        'na_rpb': nrm(17, (L, NA_HEADS, 2 * NA_ROWS - 1, 2 * NA_COLS - 1), 0.02),
        'g_diff_q': gain(18, (L, DIFF_DIM)),
        'g_diff_k': gain(19, (L, DIFF_DIM)),
        'diff_lq1': nrm(20, (L, DIFF_DIM), 0.1),
        'diff_lk1': nrm(21, (L, DIFF_DIM), 0.1),
        'diff_lq2': nrm(22, (L, DIFF_DIM), 0.1),
        'diff_lk2': nrm(23, (L, DIFF_DIM), 0.1),
        'g_diff_sub': gain(24, (L, 2 * DIFF_DIM)),
        'g_gqa_q': gain(25, (L, HEAD_DIM)),
        'g_gqa_k': gain(26, (L, HEAD_DIM)),
        'w_branch': nrm(27, (L, N_BRANCH, BRANCH_WIDTH, D), BRANCH_WIDTH ** -0.5),
        'w_out': nrm(28, (L, D, D), D ** -0.5),
        'w_up': nrm(29, (L, D, D_FF), D ** -0.5),
        'w_down': nrm(30, (L, D_FF, D), D_FF ** -0.5),
    }


def reference(x, c, ctx, c_ctx, w_ada, b_ada, g_norm1, g_norm2, w_in,
              g_mla_qa, w_mla_uq, g_mla_kva, w_mla_ukv, g_mla_q, g_mla_k,
              g_na_q, g_na_k, na_rpb,
              g_diff_q, g_diff_k, diff_lq1, diff_lk1, diff_lq2, diff_lk2, g_diff_sub,
              g_gqa_q, g_gqa_k, w_branch, w_out, w_up, w_down):
    S = x.shape[1]
    t = jnp.arange(S)
    rows, cols = t // GRID_W, t % GRID_W
    rope32 = rope_tables(rows, cols, MLA_ROPE)
    rope64 = rope_tables(rows, cols, HEAD_DIM)
    xc = ctx
    for l in range(DEPTH):
        need_ctx = l < DEPTH - 1
        lam_init = 0.8 - 0.6 * math.exp(-0.3 * l)
        mod = jax.nn.silu(c) @ w_ada[l] + b_ada[l]
        mod_c = jax.nn.silu(c_ctx) @ w_ada[l] + b_ada[l]
        sh1, sc1, gt1, sh2, sc2, gt2 = jnp.split(mod[:, None, :], 6, axis=-1)
        csh1, csc1, cgt1, csh2, csc2, cgt2 = jnp.split(mod_c, 6, axis=-1)

        z = split_last(modulate(rmsnorm(x, g_norm1[l]), sh1, sc1) @ w_in[l], SPLIT_SIZES)
        zc = split_last(modulate(rmsnorm(xc, g_norm1[l]), csh1, csc1) @ w_in[l], SPLIT_SIZES)

        oa, oa_c = mla_mixer(z[0:3], zc[0:3], rope32, g_mla_qa[l], w_mla_uq[l], g_mla_kva[l],
                             w_mla_ukv[l], g_mla_q[l], g_mla_k[l], need_ctx)
        ob, ob_c = na_mixer(z[3], zc[3], g_na_q[l], g_na_k[l], na_rpb[l], need_ctx)
        oc, oc_c = diff_mixer(z[4], zc[4], rope32, g_diff_q[l], g_diff_k[l], diff_lq1[l], diff_lk1[l],
                              diff_lq2[l], diff_lk2[l], g_diff_sub[l], lam_init, need_ctx)
        od, od_c = gqa_mixer(z[5], zc[5], rope64, g_gqa_q[l], g_gqa_k[l], need_ctx)

        x = x + gt1 * merge_branches([oa, ob, oc, od], z[6], w_branch[l], w_out[l])
        x = x + gt2 * sq_relu_mlp(modulate(rmsnorm(x, g_norm2[l]), sh2, sc2), w_up[l], w_down[l])
        if need_ctx:
            xc = xc + cgt1 * merge_branches([oa_c, ob_c, oc_c, od_c], zc[6], w_branch[l], w_out[l])
            xc = xc + cgt2 * sq_relu_mlp(modulate(rmsnorm(xc, g_norm2[l]), csh2, csc2), w_up[l], w_down[l])
    return x
```

```python
import functools
import math

import jax
import jax.numpy as jnp
from jax import lax
from jax.experimental import pallas as pl
from jax.experimental.pallas import tpu as pltpu

F32 = jnp.float32
BF16 = jnp.bfloat16

D = 1024
SEQ = 2048
CTX = 256
TOK = SEQ + CTX
GRID_W = 64
N_ROWS = SEQ // GRID_W
ROPE_THETA = 10000.0
EPS = 1e-6
HEAD_DIM = 64
N_BRANCH = 4
MLA_Q_LORA = 192
MLA_KV_LORA = 128
MLA_NOPE = 64
MLA_ROPE = 32
MLA_QK = MLA_NOPE + MLA_ROPE
NA_ROWS = 8
NA_COLS = 16
DIFF_DIM = 32
D_FF = 4 * D

TM = 256
N_LAT = SEQ // TM
N_ALL = TOK // TM
LANES = 128
MASK_VALUE = -1e30

C_CQ, C_CKV, C_KR, C_NA, C_DF, C_GQ, C_GATE = 0, 192, 320, 352, 1120, 1888, 2400
Z_CQ, Z_CKV, Z_KR, Z_NA, Z_DF, Z_GQ, Z_W = 0, 256, 384, 896, 1664, 2432, 2944

VMEM_LIMIT = 56 * 1024 * 1024


def _cparams(n_axes):
    return pltpu.CompilerParams(dimension_semantics=("arbitrary",) * n_axes,
                                vmem_limit_bytes=VMEM_LIMIT)


def _dot(a, b):
    return jnp.dot(a, b, preferred_element_type=F32)


def _modnorm(x, g, shift, scale):
    y = x * lax.rsqrt(jnp.mean(x * x, axis=-1, keepdims=True) + EPS) * g
    return y * (1.0 + scale) + shift


def _ada_kernel(c_ref, w_ref, b_ref, o_ref):
    c = c_ref[...]
    s = c * jax.nn.sigmoid(c)
    o_ref[0] = _dot(s.astype(BF16), w_ref[0].astype(BF16)) + b_ref[0]


def _ada_call(c_all, w_ada, b_ada):
    n_layers = w_ada.shape[0]
    rows = c_all.shape[0]
    bn = 1536
    return pl.pallas_call(
        _ada_kernel,
        grid=(n_layers, 6 * D // bn),
        in_specs=[pl.BlockSpec((rows, D), lambda l, j: (0, 0)),
                  pl.BlockSpec((1, D, bn), lambda l, j: (l, 0, j)),
                  pl.BlockSpec((1, 1, bn), lambda l, j: (l, 0, j))],
        out_specs=pl.BlockSpec((1, rows, bn), lambda l, j: (l, 0, j)),
        out_shape=jax.ShapeDtypeStruct((n_layers, rows, 6 * D), F32),
        compiler_params=_cparams(2),
        name="ada",
    )(c_all, w_ada, b_ada.reshape(n_layers, 1, 6 * D))


def _inproj_kernel(x_ref, mod_ref, g_ref, w_ref, z_ref):
    mod = mod_ref[0]
    xn = _modnorm(x_ref[0], g_ref[...], mod[:, 0:D], mod[:, D:2 * D])
    z_ref[0] = _dot(xn.astype(BF16), w_ref[...])


def _mod_index(n_batch):
    return lambda b, t: (jnp.where(t < N_LAT, b, n_batch), 0, 0)


def _inproj_call(xx, mod, g1, w_small):
    n_batch = xx.shape[0]
    return pl.pallas_call(
        _inproj_kernel,
        grid=(n_batch, N_ALL),
        in_specs=[pl.BlockSpec((1, TM, D), lambda b, t: (b, t, 0)),
                  pl.BlockSpec((1, 1, 6 * D), _mod_index(n_batch)),
                  pl.BlockSpec((1, D), lambda b, t: (0, 0)),
                  pl.BlockSpec((D, Z_W), lambda b, t: (0, 0))],
        out_specs=pl.BlockSpec((1, TM, Z_W), lambda b, t: (b, t, 0)),
        out_shape=jax.ShapeDtypeStruct((n_batch, TOK, Z_W), F32),
        compiler_params=_cparams(2),
        name="inproj",
    )(xx, mod, g1, w_small)


def _seg_mean_sq(t, bd, inv_d):
    x2 = t * t
    hi = x2.astype(BF16)
    lo = (x2 - hi.astype(F32)).astype(BF16)
    return (_dot(hi, bd) + _dot(lo, bd)) * inv_d


def _rope_chunk(y, tab_ref, lo, shift):
    tc = tab_ref[0, :, lo:lo + LANES]
    ta = tab_ref[1, :, lo:lo + LANES]
    tb = tab_ref[2, :, lo:lo + LANES]
    return y * tc + pltpu.roll(y, LANES - shift, 1) * ta + pltpu.roll(y, shift, 1) * tb


def _prep_kernel(z_ref, bd128_ref, bd64_ref, bd32_ref, bd64h_ref,
                 gqa_ref, gkva_ref, wuq_ref, wuk_ref, wuv_ref,
                 mq_tab, mk_tab, nag_ref, dq_tab, dk_tab, gq_tab, gk_tab,
                 mq_ref, mkt_ref, mv_ref, nq_ref, nkt_ref, nv_ref,
                 dq_ref, dkt_ref, dv_ref, gq_ref, gkt_ref, gv_ref):
    def normed(t, bd, inv_d):
        return t * lax.rsqrt(_seg_mean_sq(t, bd, inv_d) + EPS)

    def put(o_ref, lo, val):
        o_ref[0, :, lo:lo + LANES] = val.astype(BF16)

    def put_t(o_ref, lo, val):
        o_ref[0, lo:lo + LANES, :] = val.T.astype(BF16)

    def rope_section(t, bd, inv_d, tab_ref, shift, o_ref, lo0, store):
        yn = normed(t, bd, inv_d)
        for j in range(t.shape[1] // LANES):
            lo = lo0 + j * LANES
            store(o_ref, lo, _rope_chunk(yn[:, j * LANES:(j + 1) * LANES], tab_ref, lo, shift))

    cq = z_ref[0, :, Z_CQ:Z_CQ + 256]
    cqn = cq * lax.rsqrt(jnp.sum(cq * cq, axis=-1, keepdims=True) * (1.0 / MLA_Q_LORA) + EPS) * gqa_ref[...]
    q = _dot(cqn.astype(BF16), wuq_ref[...])
    ckv = z_ref[0, :, Z_CKV:Z_CKV + 128]
    kvn = (ckv * lax.rsqrt(jnp.mean(ckv * ckv, axis=-1, keepdims=True) + EPS) * gkva_ref[...]).astype(BF16)
    k = _dot(kvn, wuk_ref[...]) + z_ref[0, :, Z_KR:Z_KR + 512]
    mv_ref[0] = _dot(kvn, wuv_ref[...]).astype(BF16)
    bd128 = bd128_ref[...]
    for c in range(2):
        rope_section(q[:, 256 * c:256 * c + 256], bd128, 1.0 / MLA_QK, mq_tab, MLA_ROPE // 4, mq_ref, 256 * c, put)
        rope_section(k[:, 256 * c:256 * c + 256], bd128, 1.0 / MLA_QK, mk_tab, MLA_ROPE // 4, mkt_ref, 256 * c, put_t)

    bd64 = bd64_ref[...]
    nq_ref[0] = (normed(z_ref[0, :, Z_NA:Z_NA + 256], bd64, 1.0 / HEAD_DIM) * nag_ref[0:1, :]).astype(BF16)
    nk = normed(z_ref[0, :, Z_NA + 256:Z_NA + 512], bd64, 1.0 / HEAD_DIM) * nag_ref[1:2, :]
    nkt_ref[0] = nk.T.astype(BF16)
    nv_ref[0] = z_ref[0, :, Z_NA + 512:Z_NA + 768].astype(BF16)

    bd32 = bd32_ref[...]
    rope_section(z_ref[0, :, Z_DF:Z_DF + 256], bd32, 1.0 / DIFF_DIM, dq_tab, DIFF_DIM // 4, dq_ref, 0, put)
    rope_section(z_ref[0, :, Z_DF + 256:Z_DF + 512], bd32, 1.0 / DIFF_DIM, dk_tab, DIFF_DIM // 4, dkt_ref, 0, put_t)
    dv_ref[0] = z_ref[0, :, Z_DF + 512:Z_DF + 768].astype(BF16)

    rope_section(z_ref[0, :, Z_GQ:Z_GQ + 256], bd64, 1.0 / HEAD_DIM, gq_tab, HEAD_DIM // 4, gq_ref, 0, put)
    rope_section(z_ref[0, :, Z_GQ + 256:Z_GQ + 384], bd64h_ref[...], 1.0 / HEAD_DIM, gk_tab, HEAD_DIM // 4, gkt_ref, 0, put_t)
    gv_ref[0] = z_ref[0, :, Z_GQ + 384:Z_GQ + 512].astype(BF16)


def _prep_call(z, consts, lw):
    n_batch = z.shape[0]
    tok_spec = lambda w: pl.BlockSpec((1, TM, w), lambda t, b: (b, t, 0))
    tr_spec = lambda w: pl.BlockSpec((1, w, TM), lambda t, b: (b, 0, t))
    tab_spec = lambda w: pl.BlockSpec((3, TM, w), lambda t, b: (0, t, 0))
    full = lambda a: pl.BlockSpec(a.shape, lambda t, b: (0,) * a.ndim)
    tok_shape = lambda w: jax.ShapeDtypeStruct((n_batch, TOK, w), BF16)
    tr_shape = lambda w: jax.ShapeDtypeStruct((n_batch, w, TOK), BF16)
    small = [consts["bd128"], consts["bd64"], consts["bd32"], consts["bd64h"],
             lw["g_qa"], lw["g_kva"], lw["w_uq"], lw["w_uk"], lw["w_uv"]]
    return pl.pallas_call(
        _prep_kernel,
        grid=(N_ALL, n_batch),
        in_specs=[tok_spec(Z_W)] + [full(a) for a in small]
                 + [tab_spec(512), tab_spec(512), full(lw["na_g"]),
                    tab_spec(256), tab_spec(256), tab_spec(256), tab_spec(128)],
        out_specs=[tok_spec(512), tr_spec(512), tok_spec(256),
                   tok_spec(256), tr_spec(256), tok_spec(256),
                   tok_spec(256), tr_spec(256), tok_spec(256),
                   tok_spec(256), tr_spec(128), tok_spec(128)],
        out_shape=[tok_shape(512), tr_shape(512), tok_shape(256),
                   tok_shape(256), tr_shape(256), tok_shape(256),
                   tok_shape(256), tr_shape(256), tok_shape(256),
                   tok_shape(256), tr_shape(128), tok_shape(128)],
        compiler_params=_cparams(2),
        name="prep",
    )(z, *small, lw["mq_tab"], lw["mk_tab"], lw["na_g"], lw["dq_tab"], lw["dk_tab"], lw["gq_tab"], lw["gk_tab"])


def _lane_mask(shape, lo, hi):
    lane = lax.broadcasted_iota(jnp.int32, shape, 1)
    return jnp.logical_and(lane >= lo, lane < hi)


def _softmax_pv(qm, kt, v):
    s = _dot(qm, kt)
    p = jnp.exp(s - jnp.max(s, axis=-1, keepdims=True))
    l = jnp.sum(p, axis=-1, keepdims=True)
    return _dot(p.astype(BF16), v) / l


def _key_ranges(n_tiles, body):
    t = pl.program_id(1)

    @pl.when(t < N_LAT)
    def _():
        body(0, TOK)

    if n_tiles > N_LAT:
        @pl.when(t >= N_LAT)
        def _():
            body(SEQ, CTX)


def _mla_attn_kernel(q_ref, kt_ref, v_ref, o_ref, *, n_tiles):
    def body(k0, nk):
        acc = jnp.zeros((TM, 256), F32)
        for h in range(4):
            c0 = 256 * (h // 2)
            lo = LANES * (h % 2)
            qc = q_ref[0, :, c0:c0 + 256]
            qm = jnp.where(_lane_mask(qc.shape, lo, lo + LANES), qc, jnp.zeros_like(qc))
            pv = _softmax_pv(qm, kt_ref[0, c0:c0 + 256, k0:k0 + nk], v_ref[0, k0:k0 + nk, :])
            acc = jnp.where(_lane_mask(acc.shape, 64 * h, 64 * h + 64), pv, acc)
        o_ref[0] = acc.astype(BF16)

    _key_ranges(n_tiles, body)


def _diff_attn_kernel(q_ref, kt_ref, v_ref, lam_ref, gsub_ref, o_ref, *, n_tiles, lam_init):
    lq1, lk1, lq2, lk2 = (lam_ref[i:i + 1, :] for i in range(4))
    lam = (jnp.exp(jnp.sum(lq1 * lk1, axis=-1, keepdims=True))
           - jnp.exp(jnp.sum(lq2 * lk2, axis=-1, keepdims=True)) + lam_init)

    def body(k0, nk):
        q = q_ref[0]
        kt = kt_ref[0, :, k0:k0 + nk]
        v = v_ref[0, k0:k0 + nk, :]
        acc = jnp.zeros((TM, 256), F32)
        for h in range(4):
            maps = []
            for m in range(2):
                lo = 64 * h + DIFF_DIM * m
                qm = jnp.where(_lane_mask(q.shape, lo, lo + DIFF_DIM), q, jnp.zeros_like(q))
                maps.append(_softmax_pv(qm, kt, v))
            oh = jnp.where(_lane_mask(acc.shape, 64 * h, 64 * h + 64), maps[0] - lam * maps[1], 0.0)
            ms = jnp.sum(oh * oh, axis=-1, keepdims=True) * (1.0 / (2 * DIFF_DIM))
            acc = acc + oh * lax.rsqrt(ms + EPS)
        o_ref[0] = (acc * gsub_ref[...] * (1.0 - lam_init)).astype(BF16)

    _key_ranges(n_tiles, body)


def _gqa_attn_kernel(q_ref, kt_ref, v_ref, o_ref, *, n_tiles):
    def body(k0, nk):
        kt = kt_ref[0, :, k0:k0 + nk]
        v = v_ref[0, k0:k0 + nk, :]
        for g in range(2):
            qc = q_ref[0, :, LANES * g:LANES * g + LANES]
            acc = jnp.zeros((TM, LANES), F32)
            for n in range(2):
                qm = jnp.where(_lane_mask(qc.shape, 64 * n, 64 * n + 64), qc, jnp.zeros_like(qc))
                pv = _softmax_pv(qm, kt, v)
                acc = jnp.where(_lane_mask(acc.shape, 64 * n, 64 * n + 64), pv, acc)
            o_ref[0, :, LANES * g:LANES * g + LANES] = acc.astype(BF16)

    _key_ranges(n_tiles, body)


def _global_attn_call(kernel, name, q, kt, v, extra, n_tiles):
    n_batch, _, wq = q.shape
    wk = kt.shape[1]
    wv = v.shape[2]
    n_out = n_tiles * TM
    full = lambda a: pl.BlockSpec(a.shape, lambda b, t: (0,) * a.ndim)
    return pl.pallas_call(
        functools.partial(kernel, n_tiles=n_tiles),
        grid=(n_batch, n_tiles),
        in_specs=[pl.BlockSpec((1, TM, wq), lambda b, t: (b, t, 0)),
                  pl.BlockSpec((1, wk, TOK), lambda b, t: (b, 0, 0)),
                  pl.BlockSpec((1, TOK, wv), lambda b, t: (b, 0, 0))] + [full(a) for a in extra],
        out_specs=pl.BlockSpec((1, TM, 256), lambda b, t: (b, t, 0)),
        out_shape=jax.ShapeDtypeStruct((n_batch, n_out, 256), BF16),
        compiler_params=_cparams(2),
        name=name,
    )(q, kt, v, *extra)


def _na_window(t):
    return jnp.clip(t - 1, 0, N_LAT - 3)


def _na_attn_kernel(q_ref, k0_ref, k1_ref, k2_ref, kc_ref, v0_ref, v1_ref, v2_ref, vc_ref, bias_ref, o_ref, *, n_tiles):
    t = pl.program_id(0)
    kts = (k0_ref, k1_ref, k2_ref)
    vs = (v0_ref, v1_ref, v2_ref)

    @pl.when(t < N_LAT)
    def _():
        q = q_ref[0]
        acc = jnp.zeros((TM, 256), F32)
        for h in range(4):
            qm = jnp.where(_lane_mask(q.shape, 64 * h, 64 * h + 64), q, jnp.zeros_like(q))
            s = [_dot(qm, kts[j][0]) + bias_ref[0, h, :, TM * j:TM * j + TM] for j in range(3)]
            s.append(_dot(qm, kc_ref[0]))
            m = functools.reduce(jnp.maximum, [jnp.max(x, axis=-1, keepdims=True) for x in s])
            p = [jnp.exp(x - m) for x in s]
            l = functools.reduce(lambda a, b: a + b, [jnp.sum(x, axis=-1, keepdims=True) for x in p])
            pv = _dot(p[3].astype(BF16), vc_ref[0])
            for j in range(3):
                pv = pv + _dot(p[j].astype(BF16), vs[j][0])
            acc = jnp.where(_lane_mask(acc.shape, 64 * h, 64 * h + 64), pv / l, acc)
        o_ref[0] = acc.astype(BF16)

    if n_tiles > N_LAT:
        @pl.when(t >= N_LAT)
        def _():
            q = q_ref[0]
            acc = jnp.zeros((TM, 256), F32)
            for h in range(4):
                qm = jnp.where(_lane_mask(q.shape, 64 * h, 64 * h + 64), q, jnp.zeros_like(q))
                pv = _softmax_pv(qm, kc_ref[0], vc_ref[0])
                acc = jnp.where(_lane_mask(acc.shape, 64 * h, 64 * h + 64), pv, acc)
            o_ref[0] = acc.astype(BF16)


def _na_attn_call(q, kt, v, bias, n_tiles):
    n_batch = q.shape[0]
    kt_spec = lambda j: pl.BlockSpec((1, 256, TM), lambda t, b: (b, 0, _na_window(t) + j))
    v_spec = lambda j: pl.BlockSpec((1, TM, 256), lambda t, b: (b, _na_window(t) + j, 0))
    return pl.pallas_call(
        functools.partial(_na_attn_kernel, n_tiles=n_tiles),
        grid=(n_tiles, n_batch),
        in_specs=[pl.BlockSpec((1, TM, 256), lambda t, b: (b, t, 0)),
                  kt_spec(0), kt_spec(1), kt_spec(2),
                  pl.BlockSpec((1, 256, TM), lambda t, b: (b, 0, N_LAT)),
                  v_spec(0), v_spec(1), v_spec(2),
                  pl.BlockSpec((1, TM, 256), lambda t, b: (b, N_LAT, 0)),
                  pl.BlockSpec((1, 4, TM, 3 * TM), lambda t, b: (jnp.minimum(t, N_LAT - 1), 0, 0, 0))],
        out_specs=pl.BlockSpec((1, TM, 256), lambda t, b: (b, t, 0)),
        out_shape=jax.ShapeDtypeStruct((n_batch, n_tiles * TM, 256), BF16),
        compiler_params=_cparams(2),
        name="na_attn",
    )(q, kt, kt, kt, kt, v, v, v, v, bias)


def _merge_kernel(x_ref, mod_ref, g_ref, oa_ref, ob_ref, oc_ref, od_ref, wg_ref, wb_ref, wo_ref, out_ref):
    x = x_ref[0]
    mod = mod_ref[0]
    xn = _modnorm(x, g_ref[...], mod[:, 0:D], mod[:, D:2 * D]).astype(BF16)
    acc = jnp.zeros((TM, D), F32)
    for n, o_ref in enumerate((oa_ref, ob_ref, oc_ref, od_ref)):
        gate = jax.nn.sigmoid(_dot(xn, wg_ref[:, n * D:(n + 1) * D]))
        acc = acc + gate * _dot(o_ref[0], wb_ref[n])
    y = _dot(acc.astype(BF16), wo_ref[...])
    out_ref[0] = x + mod[:, 2 * D:3 * D] * y


def _merge_call(xx, mod, g1, branches, w_gate, w_branch, w_out, n_tiles):
    n_batch = xx.shape[0]
    tok = lambda w: pl.BlockSpec((1, TM, w), lambda b, t: (b, t, 0))
    full = lambda a: pl.BlockSpec(a.shape, lambda b, t: (0,) * a.ndim)
    return pl.pallas_call(
        _merge_kernel,
        grid=(n_batch, n_tiles),
        in_specs=[tok(D), pl.BlockSpec((1, 1, 6 * D), _mod_index(n_batch)), full(g1)]
                 + [tok(256)] * 4 + [full(w_gate), full(w_branch), full(w_out)],
        out_specs=tok(D),
        out_shape=jax.ShapeDtypeStruct((n_batch, n_tiles * TM, D), F32),
        compiler_params=_cparams(2),
        name="merge",
    )(xx, mod, g1, *branches, w_gate, w_branch, w_out)


def _mlp_kernel(x_ref, mod_ref, g_ref, wu_ref, wd_ref, out_ref):
    x = x_ref[0]
    mod = mod_ref[0]
    xn = _modnorm(x, g_ref[...], mod[:, 3 * D:4 * D], mod[:, 4 * D:5 * D]).astype(BF16)
    h = jnp.square(jnp.maximum(_dot(xn, wu_ref[...]), 0.0))
    out_ref[0] = x + mod[:, 5 * D:6 * D] * _dot(h.astype(BF16), wd_ref[...])


def _mlp_call(xx, mod, g2, w_up, w_down, n_tiles):
    n_batch = xx.shape[0]
    tok = lambda w: pl.BlockSpec((1, TM, w), lambda b, t: (b, t, 0))
    full = lambda a: pl.BlockSpec(a.shape, lambda b, t: (0,) * a.ndim)
    return pl.pallas_call(
        _mlp_kernel,
        grid=(n_batch, n_tiles),
        in_specs=[tok(D), pl.BlockSpec((1, 1, 6 * D), _mod_index(n_batch)), full(g2), full(w_up), full(w_down)],
        out_specs=tok(D),
        out_shape=jax.ShapeDtypeStruct((n_batch, n_tiles * TM, D), F32),
        compiler_params=_cparams(2),
        name="mlp",
    )(xx, mod, g2, w_up, w_down)


def _block_diag_ones(n, seg):
    i = jnp.arange(n) // seg
    return (i[:, None] == i[None, :]).astype(BF16)


def _rope_parts(rot_dim):
    t = jnp.arange(SEQ)
    rows, cols = t // GRID_W, t % GRID_W
    n = rot_dim // 4
    inv_freq = jnp.power(ROPE_THETA, -jnp.arange(n, dtype=F32) / n)
    ang_r = rows.astype(F32)[:, None] * inv_freq
    ang_c = cols.astype(F32)[:, None] * inv_freq
    ang = jnp.concatenate([ang_r, ang_r, ang_c, ang_c], axis=-1)
    cos = jnp.concatenate([jnp.cos(ang), jnp.ones((CTX, rot_dim), F32)], axis=0)
    sin = jnp.concatenate([jnp.sin(ang), jnp.zeros((CTX, rot_dim), F32)], axis=0)
    even = (jnp.arange(rot_dim) // n) % 2 == 0
    return cos, jnp.where(even, -sin, 0.0), jnp.where(even, 0.0, sin)


def _rope_table(parts, gain, scale, shift):
    cos, sin_l, sin_r = parts
    return jnp.stack([cos * gain, sin_l * jnp.roll(gain, -shift), sin_r * jnp.roll(gain, shift)]) * scale


def _mla_slot(parts):
    ones = jnp.ones((TOK, MLA_NOPE), F32)
    zeros_n = jnp.zeros((TOK, MLA_NOPE), F32)
    pad = jnp.zeros((TOK, LANES - MLA_QK), F32)
    cos, sin_l, sin_r = parts
    tile = lambda a, b: jnp.tile(jnp.concatenate([a, b, pad], axis=1), (1, 4))
    return tile(ones, cos), tile(zeros_n, sin_l), tile(zeros_n, sin_r)


def _na_bias_table(rpb):
    n_heads = rpb.shape[0]
    rb = jnp.arange(N_LAT)[:, None, None]
    q = jnp.arange(TM)[None, :, None]
    key = jnp.arange(3 * TM)[None, None, :]
    rows_per_tile = TM // GRID_W
    qr, qc = rows_per_tile * rb + q // GRID_W, q % GRID_W
    kr = rows_per_tile * jnp.clip(rb - 1, 0, N_LAT - 3) + key // GRID_W
    kc = key % GRID_W
    r0 = jnp.clip(qr - NA_ROWS // 2, 0, N_ROWS - NA_ROWS)
    c0 = jnp.clip(qc - NA_COLS // 2, 0, GRID_W - NA_COLS)
    valid = (kr >= r0) & (kr < r0 + NA_ROWS) & (kc >= c0) & (kc < c0 + NA_COLS)
    dr = jnp.clip(kr - qr + NA_ROWS - 1, 0, 2 * NA_ROWS - 2)
    dc = jnp.clip(kc - qc + NA_COLS - 1, 0, 2 * NA_COLS - 2)
    vals = rpb[:, dr, dc]
    return jnp.where(valid[None], vals, MASK_VALUE).transpose(1, 0, 2, 3).astype(F32)


def _layer_params(l, p, rope32, rope64):
    w_in = p["w_in"][l]
    zcol = lambda n: jnp.zeros((D, n), F32)
    kr = w_in[:, C_KR:C_KR + MLA_ROPE]
    kr_slots = jnp.concatenate([jnp.concatenate([zcol(MLA_NOPE), kr, zcol(LANES - MLA_QK)], axis=1)] * 4, axis=1)
    gq_cols = w_in[:, C_GQ:C_GQ + 256].reshape(D, 4, HEAD_DIM)[:, jnp.array([0, 2, 1, 3])].reshape(D, 256)
    w_small = jnp.concatenate([
        w_in[:, C_CQ:C_CQ + MLA_Q_LORA], zcol(256 - MLA_Q_LORA),
        w_in[:, C_CKV:C_CKV + MLA_KV_LORA],
        kr_slots,
        w_in[:, C_NA:C_NA + 768],
        w_in[:, C_DF:C_DF + 768],
        gq_cols, w_in[:, C_GQ + 256:C_GQ + 512]], axis=1).astype(BF16)

    w_uq = p["w_mla_uq"][l].reshape(MLA_Q_LORA, 4, MLA_QK)
    w_uq = jnp.pad(w_uq, ((0, 256 - MLA_Q_LORA), (0, 0), (0, LANES - MLA_QK))).reshape(256, 512).astype(BF16)
    w_ukv = p["w_mla_ukv"][l].reshape(MLA_KV_LORA, 4, 2 * MLA_NOPE)
    w_uk = jnp.pad(w_ukv[:, :, :MLA_NOPE], ((0, 0), (0, 0), (0, LANES - MLA_NOPE))).reshape(MLA_KV_LORA, 512).astype(BF16)
    w_uv = w_ukv[:, :, MLA_NOPE:].reshape(MLA_KV_LORA, 256).astype(BF16)

    mla_gain = lambda g: jnp.tile(jnp.pad(g, (0, LANES - MLA_QK)), 4)
    mla_parts = _mla_slot(rope32)
    tile32 = tuple(jnp.tile(a, (1, 8)) for a in rope32)
    tile64_4 = tuple(jnp.tile(a, (1, 4)) for a in rope64)
    tile64_2 = tuple(jnp.tile(a, (1, 2)) for a in rope64)
    hd_scale = HEAD_DIM ** -0.5

    w_branch = p["w_branch"][l]
    wb_gqa = w_branch[3].reshape(4, HEAD_DIM, D)[jnp.array([0, 2, 1, 3])].reshape(256, D)
    w_branch = jnp.concatenate([w_branch[:3], wb_gqa[None]], axis=0).astype(BF16)

    return dict(
        g1=p["g_norm1"][l][None], g2=p["g_norm2"][l][None],
        w_small=w_small, w_gate=w_in[:, C_GATE:].astype(BF16),
        g_qa=jnp.pad(p["g_mla_qa"][l], (0, 256 - MLA_Q_LORA))[None], g_kva=p["g_mla_kva"][l][None],
        w_uq=w_uq, w_uk=w_uk, w_uv=w_uv,
        mq_tab=_rope_table(mla_parts, mla_gain(p["g_mla_q"][l]), MLA_QK ** -0.5, MLA_ROPE // 4),
        mk_tab=_rope_table(mla_parts, mla_gain(p["g_mla_k"][l]), 1.0, MLA_ROPE // 4),
        na_g=jnp.stack([jnp.tile(p["g_na_q"][l], 4) * hd_scale, jnp.tile(p["g_na_k"][l], 4)]),
        na_bias=_na_bias_table(p["na_rpb"][l]),
        dq_tab=_rope_table(tile32, jnp.tile(p["g_diff_q"][l], 8), DIFF_DIM ** -0.5, DIFF_DIM // 4),
        dk_tab=_rope_table(tile32, jnp.tile(p["g_diff_k"][l], 8), 1.0, DIFF_DIM // 4),
        lam=jnp.stack([p["diff_lq1"][l], p["diff_lk1"][l], p["diff_lq2"][l], p["diff_lk2"][l]]),
        g_sub=jnp.tile(p["g_diff_sub"][l], 4)[None],
        gq_tab=_rope_table(tile64_4, jnp.tile(p["g_gqa_q"][l], 4), hd_scale, HEAD_DIM // 4),
        gk_tab=_rope_table(tile64_2, jnp.tile(p["g_gqa_k"][l], 2), 1.0, HEAD_DIM // 4),
        w_branch=w_branch, w_out=p["w_out"][l].astype(BF16),
        w_up=p["w_up"][l].astype(BF16), w_down=p["w_down"][l].astype(BF16),
    )


def kernel(x, c, ctx, c_ctx, w_ada, b_ada, g_norm1, g_norm2, w_in, g_mla_qa, w_mla_uq, g_mla_kva, w_mla_ukv, g_mla_q, g_mla_k, g_na_q, g_na_k, na_rpb, g_diff_q, g_diff_k, diff_lq1, diff_lk1, diff_lq2, diff_lk2, g_diff_sub, g_gqa_q, g_gqa_k, w_branch, w_out, w_up, w_down):
    p = dict(w_in=w_in, g_norm1=g_norm1, g_norm2=g_norm2, g_mla_qa=g_mla_qa, w_mla_uq=w_mla_uq,
             g_mla_kva=g_mla_kva, w_mla_ukv=w_mla_ukv, g_mla_q=g_mla_q, g_mla_k=g_mla_k,
             g_na_q=g_na_q, g_na_k=g_na_k, na_rpb=na_rpb, g_diff_q=g_diff_q, g_diff_k=g_diff_k,
             diff_lq1=diff_lq1, diff_lk1=diff_lk1, diff_lq2=diff_lq2, diff_lk2=diff_lk2,
             g_diff_sub=g_diff_sub, g_gqa_q=g_gqa_q, g_gqa_k=g_gqa_k, w_branch=w_branch,
             w_out=w_out, w_up=w_up, w_down=w_down)
    n_batch = x.shape[0]
    depth = w_ada.shape[0]
    assert x.shape[1:] == (SEQ, D) and ctx.shape[1:] == (CTX, D)

    mod_rows = -(-(n_batch + 1) // 8) * 8
    c_all = jnp.concatenate([c, c_ctx[None], jnp.zeros((mod_rows - n_batch - 1, D), F32)], axis=0)
    mods = _ada_call(c_all, w_ada, b_ada).reshape(depth, mod_rows, 1, 6 * D)

    consts = dict(bd128=_block_diag_ones(256, 128), bd64=_block_diag_ones(256, 64),
                  bd32=_block_diag_ones(256, 32), bd64h=_block_diag_ones(128, 64))
    rope32, rope64 = _rope_parts(MLA_ROPE), _rope_parts(HEAD_DIM)

    xx = jnp.concatenate([x, ctx], axis=1)
    for l in range(depth):
        last = l == depth - 1
        n_tiles = N_LAT if last else N_ALL
        lam_init = 0.8 - 0.6 * math.exp(-0.3 * l)
        lw = _layer_params(l, p, rope32, rope64)
        mod = mods[l]

        z = _inproj_call(xx, mod, lw["g1"], lw["w_small"])
        (mq, mkt, mv, nq, nkt, nv, dq, dkt, dv, gq, gkt, gv) = _prep_call(z, consts, lw)
        oa = _global_attn_call(_mla_attn_kernel, "mla_attn", mq, mkt, mv, [], n_tiles)
        ob = _na_attn_call(nq, nkt, nv, lw["na_bias"], n_tiles)
        oc = _global_attn_call(functools.partial(_diff_attn_kernel, lam_init=lam_init), "diff_attn",
                               dq, dkt, dv, [lw["lam"], lw["g_sub"]], n_tiles)
        od = _global_attn_call(_gqa_attn_kernel, "gqa_attn", gq, gkt, gv, [], n_tiles)
        xx = _merge_call(xx, mod, lw["g1"], (oa, ob, oc, od), lw["w_gate"], lw["w_branch"], lw["w_out"], n_tiles)
        xx = _mlp_call(xx, mod, lw["g2"], lw["w_up"], lw["w_down"], n_tiles)
    return xx
```

```python
import functools
import math

import jax
import jax.numpy as jnp
from jax import lax
from jax.experimental import pallas as pl
from jax.experimental.pallas import tpu as pltpu

F32 = jnp.float32
BF16 = jnp.bfloat16

D = 1024
SEQ = 2048
CTX = 256
TOK = SEQ + CTX
GRID_W = 64
N_ROWS = SEQ // GRID_W
ROPE_THETA = 10000.0
EPS = 1e-6
HEAD_DIM = 64
N_BRANCH = 4
MLA_Q_LORA = 192
MLA_KV_LORA = 128
MLA_NOPE = 64
MLA_ROPE = 32
MLA_QK = MLA_NOPE + MLA_ROPE
NA_ROWS = 8
NA_COLS = 16
DIFF_DIM = 32
D_FF = 4 * D

TM = 256
N_LAT = SEQ // TM
N_ALL = TOK // TM
LANES = 128
MASK_VALUE = -1e30

C_CQ, C_CKV, C_KR, C_NA, C_DF, C_GQ, C_GATE = 0, 192, 320, 352, 1120, 1888, 2400
Z_CQ, Z_CKV, Z_KR, Z_NA, Z_DF, Z_GQ, Z_W = 0, 256, 384, 896, 1664, 2432, 2944

VMEM_LIMIT = 56 * 1024 * 1024


def _cparams(n_axes):
    return pltpu.CompilerParams(dimension_semantics=("arbitrary",) * n_axes,
                                vmem_limit_bytes=VMEM_LIMIT)


def _dot(a, b):
    return jnp.dot(a, b, preferred_element_type=F32)


def _modnorm(x, g, shift, scale):
    y = x * lax.rsqrt(jnp.mean(x * x, axis=-1, keepdims=True) + EPS) * g
    return y * (1.0 + scale) + shift


def _ada_kernel(c_ref, w_ref, b_ref, o_ref):
    c = c_ref[...]
    s = c * jax.nn.sigmoid(c)
    o_ref[0] = _dot(s.astype(BF16), w_ref[0].astype(BF16)) + b_ref[0]


def _ada_call(c_all, w_ada, b_ada):
    n_layers = w_ada.shape[0]
    rows = c_all.shape[0]
    bn = 1536
    return pl.pallas_call(
        _ada_kernel,
        grid=(n_layers, 6 * D // bn),
        in_specs=[pl.BlockSpec((rows, D), lambda l, j: (0, 0)),
                  pl.BlockSpec((1, D, bn), lambda l, j: (l, 0, j)),
                  pl.BlockSpec((1, 1, bn), lambda l, j: (l, 0, j))],
        out_specs=pl.BlockSpec((1, rows, bn), lambda l, j: (l, 0, j)),
        out_shape=jax.ShapeDtypeStruct((n_layers, rows, 6 * D), F32),
        compiler_params=_cparams(2),
        name="ada",
    )(c_all, w_ada, b_ada.reshape(n_layers, 1, 6 * D))


def _inproj_kernel(x_ref, mod_ref, g_ref, w_ref, z_ref):
    mod = mod_ref[0]
    xn = _modnorm(x_ref[0], g_ref[...], mod[:, 0:D], mod[:, D:2 * D])
    z_ref[0] = _dot(xn.astype(BF16), w_ref[...])


def _mod_index(n_batch):
    return lambda b, t: (jnp.where(t < N_LAT, b, n_batch), 0, 0)


def _inproj_call(xx, mod, g1, w_small):
    n_batch = xx.shape[0]
    return pl.pallas_call(
        _inproj_kernel,
        grid=(n_batch, N_ALL),
        in_specs=[pl.BlockSpec((1, TM, D), lambda b, t: (b, t, 0)),
                  pl.BlockSpec((1, 1, 6 * D), _mod_index(n_batch)),
                  pl.BlockSpec((1, D), lambda b, t: (0, 0)),
                  pl.BlockSpec((D, Z_W), lambda b, t: (0, 0))],
        out_specs=pl.BlockSpec((1, TM, Z_W), lambda b, t: (b, t, 0)),
        out_shape=jax.ShapeDtypeStruct((n_batch, TOK, Z_W), F32),
        compiler_params=_cparams(2),
        name="inproj",
    )(xx, mod, g1, w_small)


def _seg_mean_sq(t, bd, inv_d):
    x2 = t * t
    hi = x2.astype(BF16)
    lo = (x2 - hi.astype(F32)).astype(BF16)
    return (_dot(hi, bd) + _dot(lo, bd)) * inv_d


def _rope_chunk(y, tab_ref, lo, shift):
    tc = tab_ref[0, :, lo:lo + LANES]
    ta = tab_ref[1, :, lo:lo + LANES]
    tb = tab_ref[2, :, lo:lo + LANES]
    return y * tc + pltpu.roll(y, LANES - shift, 1) * ta + pltpu.roll(y, shift, 1) * tb


def _prep_kernel(z_ref, bd128_ref, bd64_ref, bd32_ref, bd64h_ref,
                 gqa_ref, gkva_ref, wuq_ref, wuk_ref, wuv_ref,
                 mq_tab, mk_tab, nag_ref, dq_tab, dk_tab, gq_tab, gk_tab,
                 mq_ref, mkt_ref, mv_ref, nq_ref, nkt_ref, nv_ref,
                 dq_ref, dkt_ref, dv_ref, gq_ref, gkt_ref, gv_ref):
    def normed(t, bd, inv_d):
        return t * lax.rsqrt(_seg_mean_sq(t, bd, inv_d) + EPS)

    def put(o_ref, lo, val):
        o_ref[0, :, lo:lo + LANES] = val.astype(BF16)

    def put_t(o_ref, lo, val):
        o_ref[0, lo:lo + LANES, :] = val.T.astype(BF16)

    def rope_section(t, bd, inv_d, tab_ref, shift, o_ref, lo0, store):
        yn = normed(t, bd, inv_d)
        for j in range(t.shape[1] // LANES):
            lo = lo0 + j * LANES
            store(o_ref, lo, _rope_chunk(yn[:, j * LANES:(j + 1) * LANES], tab_ref, lo, shift))

    cq = z_ref[0, :, Z_CQ:Z_CQ + 256]
    cqn = cq * lax.rsqrt(jnp.sum(cq * cq, axis=-1, keepdims=True) * (1.0 / MLA_Q_LORA) + EPS) * gqa_ref[...]
    q = _dot(cqn.astype(BF16), wuq_ref[...])
    ckv = z_ref[0, :, Z_CKV:Z_CKV + 128]
    kvn = (ckv * lax.rsqrt(jnp.mean(ckv * ckv, axis=-1, keepdims=True) + EPS) * gkva_ref[...]).astype(BF16)
    k = _dot(kvn, wuk_ref[...]) + z_ref[0, :, Z_KR:Z_KR + 512]
    mv_ref[0] = _dot(kvn, wuv_ref[...]).astype(BF16)
    bd128 = bd128_ref[...]
    for c in range(2):
        rope_section(q[:, 256 * c:256 * c + 256], bd128, 1.0 / MLA_QK, mq_tab, MLA_ROPE // 4, mq_ref, 256 * c, put)
        rope_section(k[:, 256 * c:256 * c + 256], bd128, 1.0 / MLA_QK, mk_tab, MLA_ROPE // 4, mkt_ref, 256 * c, put_t)

    bd64 = bd64_ref[...]
    nq_ref[0] = (normed(z_ref[0, :, Z_NA:Z_NA + 256], bd64, 1.0 / HEAD_DIM) * nag_ref[0:1, :]).astype(BF16)
    nk = normed(z_ref[0, :, Z_NA + 256:Z_NA + 512], bd64, 1.0 / HEAD_DIM) * nag_ref[1:2, :]
    nkt_ref[0] = nk.T.astype(BF16)
    nv_ref[0] = z_ref[0, :, Z_NA + 512:Z_NA + 768].astype(BF16)

    bd32 = bd32_ref[...]
    rope_section(z_ref[0, :, Z_DF:Z_DF + 256], bd32, 1.0 / DIFF_DIM, dq_tab, DIFF_DIM // 4, dq_ref, 0, put)
    rope_section(z_ref[0, :, Z_DF + 256:Z_DF + 512], bd32, 1.0 / DIFF_DIM, dk_tab, DIFF_DIM // 4, dkt_ref, 0, put_t)
    dv_ref[0] = z_ref[0, :, Z_DF + 512:Z_DF + 768].astype(BF16)

    rope_section(z_ref[0, :, Z_GQ:Z_GQ + 256], bd64, 1.0 / HEAD_DIM, gq_tab, HEAD_DIM // 4, gq_ref, 0, put)
    rope_section(z_ref[0, :, Z_GQ + 256:Z_GQ + 384], bd64h_ref[...], 1.0 / HEAD_DIM, gk_tab, HEAD_DIM // 4, gkt_ref, 0, put_t)
    gv_ref[0] = z_ref[0, :, Z_GQ + 384:Z_GQ + 512].astype(BF16)


def _prep_call(z, consts, lw):
    n_batch = z.shape[0]
    tok_spec = lambda w: pl.BlockSpec((1, TM, w), lambda t, b: (b, t, 0))
    tr_spec = lambda w: pl.BlockSpec((1, w, TM), lambda t, b: (b, 0, t))
    tab_spec = lambda w: pl.BlockSpec((3, TM, w), lambda t, b: (0, t, 0))
    full = lambda a: pl.BlockSpec(a.shape, lambda t, b: (0,) * a.ndim)
    tok_shape = lambda w: jax.ShapeDtypeStruct((n_batch, TOK, w), BF16)
    tr_shape = lambda w: jax.ShapeDtypeStruct((n_batch, w, TOK), BF16)
    small = [consts["bd128"], consts["bd64"], consts["bd32"], consts["bd64h"],
             lw["g_qa"], lw["g_kva"], lw["w_uq"], lw["w_uk"], lw["w_uv"]]
    return pl.pallas_call(
        _prep_kernel,
        grid=(N_ALL, n_batch),
        in_specs=[tok_spec(Z_W)] + [full(a) for a in small]
                 + [tab_spec(512), tab_spec(512), full(lw["na_g"]),
                    tab_spec(256), tab_spec(256), tab_spec(256), tab_spec(128)],
        out_specs=[tok_spec(512), tr_spec(512), tok_spec(256),
                   tok_spec(256), tr_spec(256), tok_spec(256),
                   tok_spec(256), tr_spec(256), tok_spec(256),
                   tok_spec(256), tr_spec(128), tok_spec(128)],
        out_shape=[tok_shape(512), tr_shape(512), tok_shape(256),
                   tok_shape(256), tr_shape(256), tok_shape(256),
                   tok_shape(256), tr_shape(256), tok_shape(256),
                   tok_shape(256), tr_shape(128), tok_shape(128)],
        compiler_params=_cparams(2),
        name="prep",
    )(z, *small, lw["mq_tab"], lw["mk_tab"], lw["na_g"], lw["dq_tab"], lw["dk_tab"], lw["gq_tab"], lw["gk_tab"])


def _lane_mask(shape, lo, hi):
    lane = lax.broadcasted_iota(jnp.int32, shape, 1)
    return jnp.logical_and(lane >= lo, lane < hi)


def _softmax_pv(qm, kt, v):
    s = _dot(qm, kt)
    p = jnp.exp(s - jnp.max(s, axis=-1, keepdims=True))
    l = jnp.sum(p, axis=-1, keepdims=True)
    return _dot(p.astype(BF16), v) / l


def _key_ranges(n_tiles, body):
    t = pl.program_id(1)

    @pl.when(t < N_LAT)
    def _():
        body(0, TOK)

    if n_tiles > N_LAT:
        @pl.when(t >= N_LAT)
        def _():
            body(SEQ, CTX)


def _mla_attn_kernel(q_ref, kt_ref, v_ref, o_ref, *, n_tiles):
    def body(k0, nk):
        acc = jnp.zeros((TM, 256), F32)
        for h in range(4):
            c0 = 256 * (h // 2)
            lo = LANES * (h % 2)
            qc = q_ref[0, :, c0:c0 + 256]
            qm = jnp.where(_lane_mask(qc.shape, lo, lo + LANES), qc, jnp.zeros_like(qc))
            pv = _softmax_pv(qm, kt_ref[0, c0:c0 + 256, k0:k0 + nk], v_ref[0, k0:k0 + nk, :])
            acc = jnp.where(_lane_mask(acc.shape, 64 * h, 64 * h + 64), pv, acc)
        o_ref[0] = acc.astype(BF16)

    _key_ranges(n_tiles, body)


def _diff_attn_kernel(q_ref, kt_ref, v_ref, lam_ref, gsub_ref, o_ref, *, n_tiles, lam_init):
    lq1, lk1, lq2, lk2 = (lam_ref[i:i + 1, :] for i in range(4))
    lam = (jnp.exp(jnp.sum(lq1 * lk1, axis=-1, keepdims=True))
           - jnp.exp(jnp.sum(lq2 * lk2, axis=-1, keepdims=True)) + lam_init)

    def body(k0, nk):
        q = q_ref[0]
        kt = kt_ref[0, :, k0:k0 + nk]
        v = v_ref[0, k0:k0 + nk, :]
        acc = jnp.zeros((TM, 256), F32)
        for h in range(4):
            maps = []
            for m in range(2):
                lo = 64 * h + DIFF_DIM * m
                qm = jnp.where(_lane_mask(q.shape, lo, lo + DIFF_DIM), q, jnp.zeros_like(q))
                maps.append(_softmax_pv(qm, kt, v))
            oh = jnp.where(_lane_mask(acc.shape, 64 * h, 64 * h + 64), maps[0] - lam * maps[1], 0.0)
            ms = jnp.sum(oh * oh, axis=-1, keepdims=True) * (1.0 / (2 * DIFF_DIM))
            acc = acc + oh * lax.rsqrt(ms + EPS)
        o_ref[0] = (acc * gsub_ref[...] * (1.0 - lam_init)).astype(BF16)

    _key_ranges(n_tiles, body)


def _gqa_attn_kernel(q_ref, kt_ref, v_ref, o_ref, *, n_tiles):
    def body(k0, nk):
        kt = kt_ref[0, :, k0:k0 + nk]
        v = v_ref[0, k0:k0 + nk, :]
        for g in range(2):
            qc = q_ref[0, :, LANES * g:LANES * g + LANES]
            acc = jnp.zeros((TM, LANES), F32)
            for n in range(2):
                qm = jnp.where(_lane_mask(qc.shape, 64 * n, 64 * n + 64), qc, jnp.zeros_like(qc))
                pv = _softmax_pv(qm, kt, v)
                acc = jnp.where(_lane_mask(acc.shape, 64 * n, 64 * n + 64), pv, acc)
            o_ref[0, :, LANES * g:LANES * g + LANES] = acc.astype(BF16)

    _key_ranges(n_tiles, body)


def _global_attn_call(kernel, name, q, kt, v, extra, n_tiles):
    n_batch, _, wq = q.shape
    wk = kt.shape[1]
    wv = v.shape[2]
    n_out = n_tiles * TM
    full = lambda a: pl.BlockSpec(a.shape, lambda b, t: (0,) * a.ndim)
    return pl.pallas_call(
        functools.partial(kernel, n_tiles=n_tiles),
        grid=(n_batch, n_tiles),
        in_specs=[pl.BlockSpec((1, TM, wq), lambda b, t: (b, t, 0)),
                  pl.BlockSpec((1, wk, TOK), lambda b, t: (b, 0, 0)),
                  pl.BlockSpec((1, TOK, wv), lambda b, t: (b, 0, 0))] + [full(a) for a in extra],
        out_specs=pl.BlockSpec((1, TM, 256), lambda b, t: (b, t, 0)),
        out_shape=jax.ShapeDtypeStruct((n_batch, n_out, 256), BF16),
        compiler_params=_cparams(2),
        name=name,
    )(q, kt, v, *extra)


def _na_window(t):
    return jnp.clip(t - 1, 0, N_LAT - 3)


def _na_build_bias(t, pair_ref, bias_ref):
    rows_per_tile = TM // GRID_W
    k_row0 = rows_per_tile * _na_window(t)
    lane = lax.broadcasted_iota(jnp.int32, (GRID_W, LANES), 1)
    for ql in range(rows_per_tile):
        qr = rows_per_tile * t + ql
        r0 = jnp.clip(qr - NA_ROWS // 2, 0, N_ROWS - NA_ROWS)
        for j in range(3 * TM // LANES):
            kr = k_row0 + 2 * j
            ok0 = jnp.logical_and(kr >= r0, kr < r0 + NA_ROWS).astype(jnp.int32)
            ok1 = jnp.logical_and(kr + 1 >= r0, kr + 1 < r0 + NA_ROWS).astype(jnp.int32)
            ok = jnp.where(lane < GRID_W, ok0, ok1) > 0
            d = jnp.clip(kr - qr + NA_ROWS - 1, -1, 2 * NA_ROWS - 2) + 1
            for h in range(4):
                blk = jnp.where(ok, pair_ref[h, d], MASK_VALUE)
                bias_ref[h, GRID_W * ql:GRID_W * (ql + 1), LANES * j:LANES * (j + 1)] = blk


def _na_attn_kernel(q_ref, k0_ref, k1_ref, k2_ref, kc_ref, v0_ref, v1_ref, v2_ref, vc_ref, pair_ref, o_ref, bias_ref,
                    *, n_tiles):
    t = pl.program_id(0)
    kts = (k0_ref, k1_ref, k2_ref)
    vs = (v0_ref, v1_ref, v2_ref)

    @pl.when(jnp.logical_and(t < N_LAT, pl.program_id(1) == 0))
    def _():
        _na_build_bias(t, pair_ref, bias_ref)

    @pl.when(t < N_LAT)
    def _():
        q = q_ref[0]
        acc = jnp.zeros((TM, 256), F32)
        for h in range(4):
            qm = jnp.where(_lane_mask(q.shape, 64 * h, 64 * h + 64), q, jnp.zeros_like(q))
            s = [_dot(qm, kts[j][0]) + bias_ref[h, :, TM * j:TM * j + TM] for j in range(3)]
            s.append(_dot(qm, kc_ref[0]))
            m = functools.reduce(jnp.maximum, [jnp.max(x, axis=-1, keepdims=True) for x in s])
            p = [jnp.exp(x - m) for x in s]
            l = functools.reduce(lambda a, b: a + b, [jnp.sum(x, axis=-1, keepdims=True) for x in p])
            pv = _dot(p[3].astype(BF16), vc_ref[0])
            for j in range(3):
                pv = pv + _dot(p[j].astype(BF16), vs[j][0])
            acc = jnp.where(_lane_mask(acc.shape, 64 * h, 64 * h + 64), pv / l, acc)
        o_ref[0] = acc.astype(BF16)

    if n_tiles > N_LAT:
        @pl.when(t >= N_LAT)
        def _():
            q = q_ref[0]
            acc = jnp.zeros((TM, 256), F32)
            for h in range(4):
                qm = jnp.where(_lane_mask(q.shape, 64 * h, 64 * h + 64), q, jnp.zeros_like(q))
                pv = _softmax_pv(qm, kc_ref[0], vc_ref[0])
                acc = jnp.where(_lane_mask(acc.shape, 64 * h, 64 * h + 64), pv, acc)
            o_ref[0] = acc.astype(BF16)


def _na_attn_call(q, kt, v, pair, n_tiles):
    n_batch = q.shape[0]
    kt_spec = lambda j: pl.BlockSpec((1, 256, TM), lambda t, b: (b, 0, _na_window(t) + j))
    v_spec = lambda j: pl.BlockSpec((1, TM, 256), lambda t, b: (b, _na_window(t) + j, 0))
    return pl.pallas_call(
        functools.partial(_na_attn_kernel, n_tiles=n_tiles),
        grid=(n_tiles, n_batch),
        in_specs=[pl.BlockSpec((1, TM, 256), lambda t, b: (b, t, 0)),
                  kt_spec(0), kt_spec(1), kt_spec(2),
                  pl.BlockSpec((1, 256, TM), lambda t, b: (b, 0, N_LAT)),
                  v_spec(0), v_spec(1), v_spec(2),
                  pl.BlockSpec((1, TM, 256), lambda t, b: (b, N_LAT, 0)),
                  pl.BlockSpec(pair.shape, lambda t, b: (0, 0, 0, 0))],
        out_specs=pl.BlockSpec((1, TM, 256), lambda t, b: (b, t, 0)),
        out_shape=jax.ShapeDtypeStruct((n_batch, n_tiles * TM, 256), BF16),
        scratch_shapes=[pltpu.VMEM((4, TM, 3 * TM), F32)],
        compiler_params=_cparams(2),
        name="na_attn",
    )(q, kt, kt, kt, kt, v, v, v, v, pair)


def _merge_kernel(x_ref, mod_ref, g_ref, oa_ref, ob_ref, oc_ref, od_ref, wg_ref, wb_ref, wo_ref, out_ref):
    x = x_ref[0]
    mod = mod_ref[0]
    xn = _modnorm(x, g_ref[...], mod[:, 0:D], mod[:, D:2 * D]).astype(BF16)
    acc = jnp.zeros((TM, D), F32)
    for n, o_ref in enumerate((oa_ref, ob_ref, oc_ref, od_ref)):
        gate = jax.nn.sigmoid(_dot(xn, wg_ref[:, n * D:(n + 1) * D]))
        acc = acc + gate * _dot(o_ref[0], wb_ref[n])
    y = _dot(acc.astype(BF16), wo_ref[...])
    out_ref[0] = x + mod[:, 2 * D:3 * D] * y


def _merge_call(xx, mod, g1, branches, w_gate, w_branch, w_out, n_tiles):
    n_batch = xx.shape[0]
    tok = lambda w: pl.BlockSpec((1, TM, w), lambda b, t: (b, t, 0))
    full = lambda a: pl.BlockSpec(a.shape, lambda b, t: (0,) * a.ndim)
    return pl.pallas_call(
        _merge_kernel,
        grid=(n_batch, n_tiles),
        in_specs=[tok(D), pl.BlockSpec((1, 1, 6 * D), _mod_index(n_batch)), full(g1)]
                 + [tok(256)] * 4 + [full(w_gate), full(w_branch), full(w_out)],
        out_specs=tok(D),
        out_shape=jax.ShapeDtypeStruct((n_batch, n_tiles * TM, D), F32),
        compiler_params=_cparams(2),
        name="merge",
    )(xx, mod, g1, *branches, w_gate, w_branch, w_out)


def _mlp_kernel(x_ref, mod_ref, g_ref, wu_ref, wd_ref, out_ref):
    x = x_ref[0]
    mod = mod_ref[0]
    xn = _modnorm(x, g_ref[...], mod[:, 3 * D:4 * D], mod[:, 4 * D:5 * D]).astype(BF16)
    h = jnp.square(jnp.maximum(_dot(xn, wu_ref[...]), 0.0))
    out_ref[0] = x + mod[:, 5 * D:6 * D] * _dot(h.astype(BF16), wd_ref[...])


def _mlp_call(xx, mod, g2, w_up, w_down, n_tiles):
    n_batch = xx.shape[0]
    tok = lambda w: pl.BlockSpec((1, TM, w), lambda b, t: (b, t, 0))
    full = lambda a: pl.BlockSpec(a.shape, lambda b, t: (0,) * a.ndim)
    return pl.pallas_call(
        _mlp_kernel,
        grid=(n_batch, n_tiles),
        in_specs=[tok(D), pl.BlockSpec((1, 1, 6 * D), _mod_index(n_batch)), full(g2), full(w_up), full(w_down)],
        out_specs=tok(D),
        out_shape=jax.ShapeDtypeStruct((n_batch, n_tiles * TM, D), F32),
        compiler_params=_cparams(2),
        name="mlp",
    )(xx, mod, g2, w_up, w_down)


def _block_diag_ones(n, seg):
    i = jnp.arange(n) // seg
    return (i[:, None] == i[None, :]).astype(BF16)


def _rope_parts(rot_dim):
    t = jnp.arange(SEQ)
    rows, cols = t // GRID_W, t % GRID_W
    n = rot_dim // 4
    inv_freq = jnp.power(ROPE_THETA, -jnp.arange(n, dtype=F32) / n)
    ang_r = rows.astype(F32)[:, None] * inv_freq
    ang_c = cols.astype(F32)[:, None] * inv_freq
    ang = jnp.concatenate([ang_r, ang_r, ang_c, ang_c], axis=-1)
    cos = jnp.concatenate([jnp.cos(ang), jnp.ones((CTX, rot_dim), F32)], axis=0)
    sin = jnp.concatenate([jnp.sin(ang), jnp.zeros((CTX, rot_dim), F32)], axis=0)
    even = (jnp.arange(rot_dim) // n) % 2 == 0
    return cos, jnp.where(even, -sin, 0.0), jnp.where(even, 0.0, sin)


def _rope_table(parts, gain, scale, shift):
    cos, sin_l, sin_r = parts
    return jnp.stack([cos * gain, sin_l * jnp.roll(gain, -shift), sin_r * jnp.roll(gain, shift)]) * scale


def _mla_slot(parts):
    ones = jnp.ones((TOK, MLA_NOPE), F32)
    zeros_n = jnp.zeros((TOK, MLA_NOPE), F32)
    pad = jnp.zeros((TOK, LANES - MLA_QK), F32)
    cos, sin_l, sin_r = parts
    tile = lambda a, b: jnp.tile(jnp.concatenate([a, b, pad], axis=1), (1, 4))
    return tile(ones, cos), tile(zeros_n, sin_l), tile(zeros_n, sin_r)


def _na_pair_table(rpb):
    qc = jnp.arange(GRID_W)[:, None]
    kc = jnp.arange(GRID_W)[None, :]
    onehot = (kc - qc + NA_COLS - 1)[None] == jnp.arange(2 * NA_COLS - 1)[:, None, None]
    toeplitz = jnp.einsum("hrd,dqk->hrqk", rpb, onehot.astype(F32), precision=lax.Precision.HIGHEST)
    c0 = jnp.clip(qc - NA_COLS // 2, 0, GRID_W - NA_COLS)
    in_window = (kc >= c0) & (kc < c0 + NA_COLS)
    masked = jnp.where(in_window, toeplitz, MASK_VALUE)
    ext = jnp.pad(masked, ((0, 0), (1, 1), (0, 0), (0, 0)), constant_values=MASK_VALUE)
    return jnp.concatenate([ext[:, :-1], ext[:, 1:]], axis=-1)


def _layer_params(l, p, rope32, rope64):
    w_in = p["w_in"][l]
    zcol = lambda n: jnp.zeros((D, n), F32)
    kr = w_in[:, C_KR:C_KR + MLA_ROPE]
    kr_slots = jnp.concatenate([jnp.concatenate([zcol(MLA_NOPE), kr, zcol(LANES - MLA_QK)], axis=1)] * 4, axis=1)
    gq_cols = w_in[:, C_GQ:C_GQ + 256].reshape(D, 4, HEAD_DIM)[:, jnp.array([0, 2, 1, 3])].reshape(D, 256)
    w_small = jnp.concatenate([
        w_in[:, C_CQ:C_CQ + MLA_Q_LORA], zcol(256 - MLA_Q_LORA),
        w_in[:, C_CKV:C_CKV + MLA_KV_LORA],
        kr_slots,
        w_in[:, C_NA:C_NA + 768],
        w_in[:, C_DF:C_DF + 768],
        gq_cols, w_in[:, C_GQ + 256:C_GQ + 512]], axis=1).astype(BF16)

    w_uq = p["w_mla_uq"][l].reshape(MLA_Q_LORA, 4, MLA_QK)
    w_uq = jnp.pad(w_uq, ((0, 256 - MLA_Q_LORA), (0, 0), (0, LANES - MLA_QK))).reshape(256, 512).astype(BF16)
    w_ukv = p["w_mla_ukv"][l].reshape(MLA_KV_LORA, 4, 2 * MLA_NOPE)
    w_uk = jnp.pad(w_ukv[:, :, :MLA_NOPE], ((0, 0), (0, 0), (0, LANES - MLA_NOPE))).reshape(MLA_KV_LORA, 512).astype(BF16)
    w_uv = w_ukv[:, :, MLA_NOPE:].reshape(MLA_KV_LORA, 256).astype(BF16)

    mla_gain = lambda g: jnp.tile(jnp.pad(g, (0, LANES - MLA_QK)), 4)
    mla_parts = _mla_slot(rope32)
    tile32 = tuple(jnp.tile(a, (1, 8)) for a in rope32)
    tile64_4 = tuple(jnp.tile(a, (1, 4)) for a in rope64)
    tile64_2 = tuple(jnp.tile(a, (1, 2)) for a in rope64)
    hd_scale = HEAD_DIM ** -0.5

    w_branch = p["w_branch"][l]
    wb_gqa = w_branch[3].reshape(4, HEAD_DIM, D)[jnp.array([0, 2, 1, 3])].reshape(256, D)
    w_branch = jnp.concatenate([w_branch[:3], wb_gqa[None]], axis=0).astype(BF16)

    return dict(
        g1=p["g_norm1"][l][None], g2=p["g_norm2"][l][None],
        w_small=w_small, w_gate=w_in[:, C_GATE:].astype(BF16),
        g_qa=jnp.pad(p["g_mla_qa"][l], (0, 256 - MLA_Q_LORA))[None], g_kva=p["g_mla_kva"][l][None],
        w_uq=w_uq, w_uk=w_uk, w_uv=w_uv,
        mq_tab=_rope_table(mla_parts, mla_gain(p["g_mla_q"][l]), MLA_QK ** -0.5, MLA_ROPE // 4),
        mk_tab=_rope_table(mla_parts, mla_gain(p["g_mla_k"][l]), 1.0, MLA_ROPE // 4),
        na_g=jnp.stack([jnp.tile(p["g_na_q"][l], 4) * hd_scale, jnp.tile(p["g_na_k"][l], 4)]),
        na_pair=_na_pair_table(p["na_rpb"][l]),
        dq_tab=_rope_table(tile32, jnp.tile(p["g_diff_q"][l], 8), DIFF_DIM ** -0.5, DIFF_DIM // 4),
        dk_tab=_rope_table(tile32, jnp.tile(p["g_diff_k"][l], 8), 1.0, DIFF_DIM // 4),
        lam=jnp.stack([p["diff_lq1"][l], p["diff_lk1"][l], p["diff_lq2"][l], p["diff_lk2"][l]]),
        g_sub=jnp.tile(p["g_diff_sub"][l], 4)[None],
        gq_tab=_rope_table(tile64_4, jnp.tile(p["g_gqa_q"][l], 4), hd_scale, HEAD_DIM // 4),
        gk_tab=_rope_table(tile64_2, jnp.tile(p["g_gqa_k"][l], 2), 1.0, HEAD_DIM // 4),
        w_branch=w_branch, w_out=p["w_out"][l].astype(BF16),
        w_up=p["w_up"][l].astype(BF16), w_down=p["w_down"][l].astype(BF16),
    )


def kernel(x, c, ctx, c_ctx, w_ada, b_ada, g_norm1, g_norm2, w_in, g_mla_qa, w_mla_uq, g_mla_kva, w_mla_ukv, g_mla_q, g_mla_k, g_na_q, g_na_k, na_rpb, g_diff_q, g_diff_k, diff_lq1, diff_lk1, diff_lq2, diff_lk2, g_diff_sub, g_gqa_q, g_gqa_k, w_branch, w_out, w_up, w_down):
    p = dict(w_in=w_in, g_norm1=g_norm1, g_norm2=g_norm2, g_mla_qa=g_mla_qa, w_mla_uq=w_mla_uq,
             g_mla_kva=g_mla_kva, w_mla_ukv=w_mla_ukv, g_mla_q=g_mla_q, g_mla_k=g_mla_k,
             g_na_q=g_na_q, g_na_k=g_na_k, na_rpb=na_rpb, g_diff_q=g_diff_q, g_diff_k=g_diff_k,
             diff_lq1=diff_lq1, diff_lk1=diff_lk1, diff_lq2=diff_lq2, diff_lk2=diff_lk2,
             g_diff_sub=g_diff_sub, g_gqa_q=g_gqa_q, g_gqa_k=g_gqa_k, w_branch=w_branch,
             w_out=w_out, w_up=w_up, w_down=w_down)
    n_batch = x.shape[0]
    depth = w_ada.shape[0]
    assert x.shape[1:] == (SEQ, D) and ctx.shape[1:] == (CTX, D)

    mod_rows = -(-(n_batch + 1) // 8) * 8
    c_all = jnp.concatenate([c, c_ctx[None], jnp.zeros((mod_rows - n_batch - 1, D), F32)], axis=0)
    mods = _ada_call(c_all, w_ada, b_ada).reshape(depth, mod_rows, 1, 6 * D)

    consts = dict(bd128=_block_diag_ones(256, 128), bd64=_block_diag_ones(256, 64),
                  bd32=_block_diag_ones(256, 32), bd64h=_block_diag_ones(128, 64))
    rope32, rope64 = _rope_parts(MLA_ROPE), _rope_parts(HEAD_DIM)

    xx = jnp.concatenate([x, ctx], axis=1)
    for l in range(depth):
        last = l == depth - 1
        n_tiles = N_LAT if last else N_ALL
        lam_init = 0.8 - 0.6 * math.exp(-0.3 * l)
        lw = _layer_params(l, p, rope32, rope64)
        mod = mods[l]

        z = _inproj_call(xx, mod, lw["g1"], lw["w_small"])
        (mq, mkt, mv, nq, nkt, nv, dq, dkt, dv, gq, gkt, gv) = _prep_call(z, consts, lw)
        oa = _global_attn_call(_mla_attn_kernel, "mla_attn", mq, mkt, mv, [], n_tiles)
        ob = _na_attn_call(nq, nkt, nv, lw["na_pair"], n_tiles)
        oc = _global_attn_call(functools.partial(_diff_attn_kernel, lam_init=lam_init), "diff_attn",
                               dq, dkt, dv, [lw["lam"], lw["g_sub"]], n_tiles)
        od = _global_attn_call(_gqa_attn_kernel, "gqa_attn", gq, gkt, gv, [], n_tiles)
        xx = _merge_call(xx, mod, lw["g1"], (oa, ob, oc, od), lw["w_gate"], lw["w_branch"], lw["w_out"], n_tiles)
        xx = _mlp_call(xx, mod, lw["g2"], lw["w_up"], lw["w_down"], n_tiles)
    return xx
```

```python
import functools
import math

import jax
import jax.numpy as jnp
from jax import lax
from jax.experimental import pallas as pl
from jax.experimental.pallas import tpu as pltpu

F32 = jnp.float32
BF16 = jnp.bfloat16

D = 1024
SEQ = 2048
CTX = 256
GRID_W = 64
N_ROWS = SEQ // GRID_W
ROPE_THETA = 10000.0
EPS = 1e-6
HEAD_DIM = 64
MLA_Q_LORA = 192
MLA_KV_LORA = 128
MLA_NOPE = 64
MLA_ROPE = 32
MLA_QK = MLA_NOPE + MLA_ROPE
NA_ROWS = 8
NA_COLS = 16
DIFF_DIM = 32
D_FF = 4 * D

LANES = 128
TM = 512
TQ = 2048
QC = 512
NA_TQ = 256
NA_WIN = 3
MASK_VALUE = -1e30

C_CQ, C_CKV, C_KR, C_NA, C_DF, C_GQ, C_GATE = 0, 192, 320, 352, 1120, 1888, 2400
Z_CQ, Z_CKV, Z_KR, Z_NA, Z_DF, Z_GQ, Z_W = 0, 256, 384, 896, 1664, 2432, 2944
G_MQ, G_MK, G_NQ, G_NK, G_DQ, G_DK, G_GQ, G_GK = range(8)
T_MLA, T_R32, T_R64 = 0, 3, 6

VMEM_LIMIT = 56 * 1024 * 1024


def _cparams(n_axes):
    return pltpu.CompilerParams(dimension_semantics=("arbitrary",) * n_axes,
                                vmem_limit_bytes=VMEM_LIMIT)


def _dot(a, b):
    return jnp.dot(a, b, preferred_element_type=F32)


def _const_spec(a):
    return pl.BlockSpec(a.shape, lambda *_: (0,) * a.ndim, pipeline_mode=pl.Buffered(1))


def _modnorm(x, g, shift, scale):
    y = x * lax.rsqrt(jnp.mean(x * x, axis=-1, keepdims=True) + EPS) * g
    return y * (1.0 + scale) + shift


def _ada_kernel(c_ref, w_ref, b_ref, o_ref):
    c = c_ref[...]
    s = c * jax.nn.sigmoid(c)
    o_ref[0] = _dot(s.astype(BF16), w_ref[0].astype(BF16)) + b_ref[0]


def _ada_call(c_all, w_ada, b_ada):
    n_layers = w_ada.shape[0]
    rows = c_all.shape[0]
    bn = 1536
    return pl.pallas_call(
        _ada_kernel,
        grid=(n_layers, 6 * D // bn),
        in_specs=[pl.BlockSpec((rows, D), lambda l, j: (0, 0)),
                  pl.BlockSpec((1, D, bn), lambda l, j: (l, 0, j)),
                  pl.BlockSpec((1, 1, bn), lambda l, j: (l, 0, j))],
        out_specs=pl.BlockSpec((1, rows, bn), lambda l, j: (l, 0, j)),
        out_shape=jax.ShapeDtypeStruct((n_layers, rows, 6 * D), F32),
        compiler_params=_cparams(2),
        name="ada",
    )(c_all, w_ada, b_ada.reshape(n_layers, 1, 6 * D))


def _seg_mean_sq(t, bd, inv_d):
    x2 = t * t
    hi = x2.astype(BF16)
    lo = (x2 - hi.astype(F32)).astype(BF16)
    return (_dot(hi, bd) + _dot(lo, bd)) * inv_d


def _qkv_kernel(*refs, rope):
    (x_ref, mod_ref, g1_ref, w_ref, bd128_ref, bd64_ref, bd32_ref, bd64h_ref,
     gqa_ref, gkva_ref, wuq_ref, wuk_ref, wuv_ref, gains_ref) = refs[:14]
    tab_ref = refs[14] if rope else None
    (mq_ref, mkt_ref, mv_ref, nq_ref, nkt_ref, nv_ref,
     dq_ref, dkt_ref, dv_ref, gq_ref, gkt_ref, gv_ref) = refs[-12:]

    mod = mod_ref[0]
    xn = _modnorm(x_ref[...], g1_ref[...], mod[:, 0:D], mod[:, D:2 * D]).astype(BF16)

    def cols(lo, hi):
        return _dot(xn, w_ref[:, lo:hi])

    def put(o_ref, lo, val):
        o_ref[:, lo:lo + LANES] = val.astype(BF16)

    def put_t(o_ref, lo, val):
        o_ref[0, lo:lo + LANES, :] = val.T.astype(BF16)

    def section(t, bd, inv_d, gain_row, tab, shift, o_ref, store):
        width = t.shape[1]
        y = t * lax.rsqrt(_seg_mean_sq(t, bd, inv_d) + EPS) * gains_ref[gain_row:gain_row + 1, 0:width]
        for j in range(width // LANES):
            yc = y[:, j * LANES:(j + 1) * LANES]
            if rope and tab is not None:
                yc = (yc * tab_ref[tab] + pltpu.roll(yc, LANES - shift, 1) * tab_ref[tab + 1]
                      + pltpu.roll(yc, shift, 1) * tab_ref[tab + 2])
            store(o_ref, j * LANES, yc)

    za = cols(Z_CQ, Z_NA)
    cq = za[:, Z_CQ:Z_CQ + 256]
    cqn = cq * lax.rsqrt(jnp.sum(cq * cq, axis=-1, keepdims=True) * (1.0 / MLA_Q_LORA) + EPS) * gqa_ref[...]
    q = _dot(cqn.astype(BF16), wuq_ref[...])
    ckv = za[:, Z_CKV:Z_CKV + 128]
    kvn = (ckv * lax.rsqrt(jnp.mean(ckv * ckv, axis=-1, keepdims=True) + EPS) * gkva_ref[...]).astype(BF16)
    k = _dot(kvn, wuk_ref[...]) + za[:, Z_KR:Z_KR + 512]
    mv_ref[...] = _dot(kvn, wuv_ref[...]).astype(BF16)
    bd128 = bd128_ref[...]
    for c in range(2):
        sl = slice(256 * c, 256 * c + 256)
        section(q[:, sl], bd128, 1.0 / MLA_QK, G_MQ, T_MLA, MLA_ROPE // 4, mq_ref.at[:, sl], put)
        section(k[:, sl], bd128, 1.0 / MLA_QK, G_MK, T_MLA, MLA_ROPE // 4, mkt_ref.at[:, sl, :], put_t)

    zn = cols(Z_NA, Z_DF)
    bd64 = bd64_ref[...]
    section(zn[:, 0:256], bd64, 1.0 / HEAD_DIM, G_NQ, None, 0, nq_ref, put)
    section(zn[:, 256:512], bd64, 1.0 / HEAD_DIM, G_NK, None, 0, nkt_ref, put_t)
    nv_ref[...] = zn[:, 512:768].astype(BF16)

    zd = cols(Z_DF, Z_GQ)
    bd32 = bd32_ref[...]
    section(zd[:, 0:256], bd32, 1.0 / DIFF_DIM, G_DQ, T_R32, DIFF_DIM // 4, dq_ref, put)
    section(zd[:, 256:512], bd32, 1.0 / DIFF_DIM, G_DK, T_R32, DIFF_DIM // 4, dkt_ref, put_t)
    dv_ref[...] = zd[:, 512:768].astype(BF16)

    zg = cols(Z_GQ, Z_W)
    section(zg[:, 0:256], bd64, 1.0 / HEAD_DIM, G_GQ, T_R64, HEAD_DIM // 4, gq_ref, put)
    section(zg[:, 256:384], bd64h_ref[...], 1.0 / HEAD_DIM, G_GK, T_R64, HEAD_DIM // 4, gkt_ref, put_t)
    gv_ref[...] = zg[:, 384:512].astype(BF16)


def _qkv_call(rows, mod, mod_row, lw, consts, tabs, n_batch, seq, tm):
    tiles = seq // tm
    row_map = lambda t, b: (b * tiles + t, 0)
    tok_spec = lambda w: pl.BlockSpec((tm, w), row_map)
    tr_spec = lambda w: pl.BlockSpec((1, w, tm), lambda t, b: (b, 0, t))
    tok_shape = lambda w: jax.ShapeDtypeStruct((n_batch * seq, w), BF16)
    tr_shape = lambda w: jax.ShapeDtypeStruct((n_batch, w, seq), BF16)
    small = [lw["g1"], lw["w_small"], consts["bd128"], consts["bd64"], consts["bd32"], consts["bd64h"],
             lw["g_qa"], lw["g_kva"], lw["w_uq"], lw["w_uk"], lw["w_uv"], lw["gains"]]
    in_specs = [tok_spec(D), pl.BlockSpec((1, 1, 6 * D), lambda t, b: (mod_row(b), 0, 0))]
    in_specs += [_const_spec(a) for a in small]
    args = [rows, mod] + small
    if tabs is not None:
        in_specs.append(pl.BlockSpec((tabs.shape[0], tm, LANES), lambda t, b: (0, t, 0)))
        args.append(tabs)
    return pl.pallas_call(
        functools.partial(_qkv_kernel, rope=tabs is not None),
        grid=(tiles, n_batch),
        in_specs=in_specs,
        out_specs=[tok_spec(512), tr_spec(512), tok_spec(256),
                   tok_spec(256), tr_spec(256), tok_spec(256),
                   tok_spec(256), tr_spec(256), tok_spec(256),
                   tok_spec(256), tr_spec(128), tok_spec(128)],
        out_shape=[tok_shape(512), tr_shape(512), tok_shape(256),
                   tok_shape(256), tr_shape(256), tok_shape(256),
                   tok_shape(256), tr_shape(256), tok_shape(256),
                   tok_shape(256), tr_shape(128), tok_shape(128)],
        compiler_params=_cparams(2),
        name="qkv",
    )(*args)


def _lane_mask(shape, lo, hi):
    lane = lax.broadcasted_iota(jnp.int32, shape, 1)
    return jnp.logical_and(lane >= lo, lane < hi)


def _keep_lanes(x, lo, hi):
    return jnp.where(_lane_mask(x.shape, lo, hi), x, jnp.zeros_like(x))


def _softmax_pv(qm, groups):
    s = []
    for kt, _, bias in groups:
        x = _dot(qm, kt)
        s.append(x if bias is None else x + bias)
    m = functools.reduce(jnp.maximum, [jnp.max(x, axis=-1, keepdims=True) for x in s])
    p = [jnp.exp(x - m) for x in s]
    l = functools.reduce(lambda a, b: a + b, [jnp.sum(x, axis=-1, keepdims=True) for x in p])
    pv = functools.reduce(lambda a, b: a + b, [_dot(x.astype(BF16), g[1]) for x, g in zip(p, groups)])
    return pv / l


def _mla_heads(q, groups_fn):
    acc = jnp.zeros((q.shape[0], 256), F32)
    for h in range(4):
        c0 = 256 * (h // 2)
        lo = LANES * (h % 2)
        pv = _softmax_pv(_keep_lanes(q[:, c0:c0 + 256], lo, lo + LANES), groups_fn(c0))
        acc = jnp.where(_lane_mask(acc.shape, 64 * h, 64 * h + 64), pv, acc)
    return acc


def _plain_heads(q, groups_fn):
    acc = jnp.zeros((q.shape[0], 256), F32)
    for h in range(4):
        pv = _softmax_pv(_keep_lanes(q, 64 * h, 64 * h + 64), groups_fn(h))
        acc = jnp.where(_lane_mask(acc.shape, 64 * h, 64 * h + 64), pv, acc)
    return acc


def _diff_lambda(lam_ref, lam_init):
    lq1, lk1, lq2, lk2 = (lam_ref[i:i + 1, :] for i in range(4))
    return (jnp.exp(jnp.sum(lq1 * lk1, axis=-1, keepdims=True))
            - jnp.exp(jnp.sum(lq2 * lk2, axis=-1, keepdims=True)) + lam_init)


def _diff_heads(q, groups, lam, gsub, lam_init):
    acc = jnp.zeros((q.shape[0], 256), F32)
    for h in range(4):
        maps = [_softmax_pv(_keep_lanes(q, 64 * h + DIFF_DIM * m, 64 * h + DIFF_DIM * (m + 1)), groups)
                for m in range(2)]
        oh = jnp.where(_lane_mask(acc.shape, 64 * h, 64 * h + 64), maps[0] - lam * maps[1], 0.0)
        ms = jnp.sum(oh * oh, axis=-1, keepdims=True) * (1.0 / (2 * DIFF_DIM))
        acc = acc + oh * lax.rsqrt(ms + EPS)
    return acc * gsub * (1.0 - lam_init)


def _gqa_heads(q, groups):
    out = []
    for g in range(2):
        qc = q[:, LANES * g:LANES * (g + 1)]
        acc = jnp.zeros((q.shape[0], LANES), F32)
        for n in range(2):
            pv = _softmax_pv(_keep_lanes(qc, 64 * n, 64 * n + 64), groups)
            acc = jnp.where(_lane_mask(acc.shape, 64 * n, 64 * n + 64), pv, acc)
        out.append(acc)
    return out


def _for_query_chunks(body):
    def step(c, carry):
        body(pl.ds(pl.multiple_of(c * QC, QC), QC))
        return carry

    lax.fori_loop(0, TQ // QC, step, 0)


def _mla_attn_kernel(q_ref, ktl_ref, vl_ref, ktc_ref, vc_ref, o_ref):
    groups_fn = lambda c0: [(ktl_ref[0, c0:c0 + 256, :], vl_ref[...], None),
                            (ktc_ref[0, c0:c0 + 256, :], vc_ref[...], None)]

    def body(rows):
        o_ref[rows, :] = _mla_heads(q_ref[rows, :], groups_fn).astype(BF16)

    _for_query_chunks(body)


def _diff_attn_kernel(q_ref, ktl_ref, vl_ref, ktc_ref, vc_ref, lam_ref, gsub_ref, o_ref, *, lam_init):
    def body(rows):
        lam = _diff_lambda(lam_ref, lam_init)
        groups = [(ktl_ref[0], vl_ref[...], None), (ktc_ref[0], vc_ref[...], None)]
        o_ref[rows, :] = _diff_heads(q_ref[rows, :], groups, lam, gsub_ref[...], lam_init).astype(BF16)

    _for_query_chunks(body)


def _gqa_attn_kernel(q_ref, ktl_ref, vl_ref, ktc_ref, vc_ref, o_ref):
    def body(rows):
        groups = [(ktl_ref[0], vl_ref[...], None), (ktc_ref[0], vc_ref[...], None)]
        halves = _gqa_heads(q_ref[rows, :], groups)
        for g in range(2):
            o_ref[rows, LANES * g:LANES * (g + 1)] = halves[g].astype(BF16)

    _for_query_chunks(body)


def _latent_attn_call(kernel, name, q, ktl, vl, ktc, vc, extra, n_batch):
    wq, wk, wv = q.shape[1], ktl.shape[1], vl.shape[1]
    tiles = SEQ // TQ
    return pl.pallas_call(
        kernel,
        grid=(n_batch, tiles),
        in_specs=[pl.BlockSpec((TQ, wq), lambda b, t: (b * tiles + t, 0)),
                  pl.BlockSpec((1, wk, SEQ), lambda b, t: (b, 0, 0)),
                  pl.BlockSpec((SEQ, wv), lambda b, t: (b, 0)),
                  pl.BlockSpec((1, wk, CTX), lambda b, t: (b, 0, 0)),
                  pl.BlockSpec((CTX, wv), lambda b, t: (b, 0))] + [_const_spec(a) for a in extra],
        out_specs=pl.BlockSpec((TQ, 256), lambda b, t: (b * tiles + t, 0)),
        out_shape=jax.ShapeDtypeStruct((n_batch * SEQ, 256), BF16),
        compiler_params=_cparams(2),
        name=name,
    )(q, ktl, vl, ktc, vc, *extra)


def _na_window(t):
    return jnp.clip(t - 1, 0, SEQ // NA_TQ - NA_WIN)


def _na_build_bias(t, pair_ref, bias_ref):
    rows_per_tile = NA_TQ // GRID_W
    k_row0 = rows_per_tile * _na_window(t)
    lane = lax.broadcasted_iota(jnp.int32, (GRID_W, LANES), 1)
    for ql in range(rows_per_tile):
        qr = rows_per_tile * t + ql
        r0 = jnp.clip(qr - NA_ROWS // 2, 0, N_ROWS - NA_ROWS)
        for j in range(NA_WIN * NA_TQ // LANES):
            kr = k_row0 + 2 * j
            ok0 = jnp.logical_and(kr >= r0, kr < r0 + NA_ROWS).astype(jnp.int32)
            ok1 = jnp.logical_and(kr + 1 >= r0, kr + 1 < r0 + NA_ROWS).astype(jnp.int32)
            ok = jnp.where(lane < GRID_W, ok0, ok1) > 0
            d = jnp.clip(kr - qr + NA_ROWS - 1, -1, 2 * NA_ROWS - 2) + 1
            for h in range(4):
                blk = jnp.where(ok, pair_ref[h, d], MASK_VALUE)
                bias_ref[h, GRID_W * ql:GRID_W * (ql + 1), LANES * j:LANES * (j + 1)] = blk


def _na_attn_kernel(q_ref, k0_ref, k1_ref, k2_ref, kc_ref, v0_ref, v1_ref, v2_ref, vc_ref, pair_ref, o_ref, bias_ref):
    t = pl.program_id(0)

    @pl.when(pl.program_id(1) == 0)
    def _():
        _na_build_bias(t, pair_ref, bias_ref)

    win = ((k0_ref, v0_ref), (k1_ref, v1_ref), (k2_ref, v2_ref))
    groups_fn = lambda h: ([(kt[0], v[...], bias_ref[h, :, NA_TQ * j:NA_TQ * (j + 1)]) for j, (kt, v) in enumerate(win)]
                           + [(kc_ref[0], vc_ref[...], None)])
    o_ref[...] = _plain_heads(q_ref[...], groups_fn).astype(BF16)


def _na_attn_call(q, ktl, vl, ktc, vc, pair, n_batch):
    tiles = SEQ // NA_TQ
    kt_spec = lambda j: pl.BlockSpec((1, 256, NA_TQ), lambda t, b: (b, 0, _na_window(t) + j))
    v_spec = lambda j: pl.BlockSpec((NA_TQ, 256), lambda t, b: (b * tiles + _na_window(t) + j, 0))
    return pl.pallas_call(
        _na_attn_kernel,
        grid=(tiles, n_batch),
        in_specs=[pl.BlockSpec((NA_TQ, 256), lambda t, b: (b * tiles + t, 0)),
                  kt_spec(0), kt_spec(1), kt_spec(2),
                  pl.BlockSpec((1, 256, CTX), lambda t, b: (b, 0, 0)),
                  v_spec(0), v_spec(1), v_spec(2),
                  pl.BlockSpec((CTX, 256), lambda t, b: (b, 0)),
                  _const_spec(pair)],
        out_specs=pl.BlockSpec((NA_TQ, 256), lambda t, b: (b * tiles + t, 0)),
        out_shape=jax.ShapeDtypeStruct((n_batch * SEQ, 256), BF16),
        scratch_shapes=[pltpu.VMEM((4, NA_TQ, NA_WIN * NA_TQ), F32)],
        compiler_params=_cparams(2),
        name="na_attn",
    )(q, ktl, ktl, ktl, ktc, vl, vl, vl, vc, pair)


def _ctx_attn_kernel(mq_ref, mkt_ref, mv_ref, nq_ref, nkt_ref, nv_ref, dq_ref, dkt_ref, dv_ref,
                     gq_ref, gkt_ref, gv_ref, lam_ref, gsub_ref, oa_ref, ob_ref, oc_ref, od_ref, *, lam_init):
    oa_ref[...] = _mla_heads(mq_ref[...], lambda c0: [(mkt_ref[0, c0:c0 + 256, :], mv_ref[...], None)]).astype(BF16)
    ob_ref[...] = _plain_heads(nq_ref[...], lambda h: [(nkt_ref[0], nv_ref[...], None)]).astype(BF16)
    lam = _diff_lambda(lam_ref, lam_init)
    oc_ref[...] = _diff_heads(dq_ref[...], [(dkt_ref[0], dv_ref[...], None)], lam, gsub_ref[...], lam_init).astype(BF16)
    halves = _gqa_heads(gq_ref[...], [(gkt_ref[0], gv_ref[...], None)])
    for g in range(2):
        od_ref[:, LANES * g:LANES * (g + 1)] = halves[g].astype(BF16)


def _ctx_attn_call(qkv_c, lam, gsub, lam_init, n_batch):
    specs = []
    for i, a in enumerate(qkv_c):
        if i % 3 == 1:
            specs.append(pl.BlockSpec((1, a.shape[1], CTX), lambda b: (b, 0, 0)))
        else:
            specs.append(pl.BlockSpec((CTX, a.shape[1]), lambda b: (b, 0)))
    out = jax.ShapeDtypeStruct((n_batch * CTX, 256), BF16)
    return pl.pallas_call(
        functools.partial(_ctx_attn_kernel, lam_init=lam_init),
        grid=(n_batch,),
        in_specs=specs + [_const_spec(lam), _const_spec(gsub)],
        out_specs=[pl.BlockSpec((CTX, 256), lambda b: (b, 0))] * 4,
        out_shape=[out] * 4,
        compiler_params=_cparams(1),
        name="ctx_attn",
    )(*qkv_c, lam, gsub)


def _merge_kernel(x_ref, mod_ref, g_ref, oa_ref, ob_ref, oc_ref, od_ref, wg_ref, wb_ref, wo_ref, out_ref):
    x = x_ref[...]
    mod = mod_ref[0]
    xn = _modnorm(x, g_ref[...], mod[:, 0:D], mod[:, D:2 * D]).astype(BF16)
    acc = jnp.zeros(x.shape, F32)
    for n, o_ref in enumerate((oa_ref, ob_ref, oc_ref, od_ref)):
        gate = jax.nn.sigmoid(_dot(xn, wg_ref[:, n * D:(n + 1) * D]))
        acc = acc + gate * _dot(o_ref[...], wb_ref[n])
    y = _dot(acc.astype(BF16), wo_ref[...])
    out_ref[...] = x + mod[:, 2 * D:3 * D] * y


def _merge_call(rows, mod, mod_row, g1, branches, w_gate, w_branch, w_out, tm):
    n_rows = rows.shape[0]
    tm = min(tm, n_rows)
    tok = lambda w: pl.BlockSpec((tm, w), lambda i: (i, 0))
    return pl.pallas_call(
        _merge_kernel,
        grid=(n_rows // tm,),
        in_specs=[tok(D), pl.BlockSpec((1, 1, 6 * D), lambda i: (mod_row(i), 0, 0)), _const_spec(g1)]
                 + [tok(256)] * 4 + [_const_spec(w_gate), _const_spec(w_branch), _const_spec(w_out)],
        out_specs=tok(D),
        out_shape=jax.ShapeDtypeStruct((n_rows, D), F32),
        compiler_params=_cparams(1),
        name="merge",
    )(rows, mod, g1, *branches, w_gate, w_branch, w_out)


def _mlp_kernel(x_ref, mod_ref, g_ref, wu_ref, wd_ref, out_ref):
    x = x_ref[...]
    mod = mod_ref[0]
    xn = _modnorm(x, g_ref[...], mod[:, 3 * D:4 * D], mod[:, 4 * D:5 * D]).astype(BF16)
    h = jnp.square(jnp.maximum(_dot(xn, wu_ref[...]), 0.0))
    out_ref[...] = x + mod[:, 5 * D:6 * D] * _dot(h.astype(BF16), wd_ref[...])


def _mlp_call(rows, mod, mod_row, g2, w_up, w_down, tm):
    n_rows = rows.shape[0]
    tm = min(tm, n_rows)
    tok = lambda w: pl.BlockSpec((tm, w), lambda i: (i, 0))
    return pl.pallas_call(
        _mlp_kernel,
        grid=(n_rows // tm,),
        in_specs=[tok(D), pl.BlockSpec((1, 1, 6 * D), lambda i: (mod_row(i), 0, 0)), _const_spec(g2),
                  _const_spec(w_up), _const_spec(w_down)],
        out_specs=tok(D),
        out_shape=jax.ShapeDtypeStruct((n_rows, D), F32),
        compiler_params=_cparams(1),
        name="mlp",
    )(rows, mod, g2, w_up, w_down)


def _block_diag_ones(n, seg):
    i = jnp.arange(n) // seg
    return (i[:, None] == i[None, :]).astype(BF16)


def _rope_parts(rot_dim):
    t = jnp.arange(SEQ)
    rows, cols = t // GRID_W, t % GRID_W
    n = rot_dim // 4
    inv_freq = jnp.power(ROPE_THETA, -jnp.arange(n, dtype=F32) / n)
    ang_r = rows.astype(F32)[:, None] * inv_freq
    ang_c = cols.astype(F32)[:, None] * inv_freq
    ang = jnp.concatenate([ang_r, ang_r, ang_c, ang_c], axis=-1)
    cos, sin = jnp.cos(ang), jnp.sin(ang)
    even = (jnp.arange(rot_dim) // n) % 2 == 0
    return cos, jnp.where(even, -sin, 0.0), jnp.where(even, 0.0, sin)


def _rope_tables():
    r32, r64 = _rope_parts(MLA_ROPE), _rope_parts(HEAD_DIM)
    ones = jnp.ones((SEQ, MLA_NOPE), F32)
    zeros_n = jnp.zeros((SEQ, MLA_NOPE), F32)
    pad = jnp.zeros((SEQ, LANES - MLA_QK), F32)
    mla = [jnp.concatenate([lead, part, pad], axis=1) for lead, part in zip((ones, zeros_n, zeros_n), r32)]
    return jnp.stack(mla + [jnp.tile(a, (1, LANES // MLA_ROPE)) for a in r32]
                     + [jnp.tile(a, (1, LANES // HEAD_DIM)) for a in r64])


def _na_pair_table(rpb):
    qc = jnp.arange(GRID_W)[:, None]
    kc = jnp.arange(GRID_W)[None, :]
    onehot = (kc - qc + NA_COLS - 1)[None] == jnp.arange(2 * NA_COLS - 1)[:, None, None]
    toeplitz = jnp.einsum("hrd,dqk->hrqk", rpb, onehot.astype(F32), precision=lax.Precision.HIGHEST)
    c0 = jnp.clip(qc - NA_COLS // 2, 0, GRID_W - NA_COLS)
    in_window = (kc >= c0) & (kc < c0 + NA_COLS)
    masked = jnp.where(in_window, toeplitz, MASK_VALUE)
    ext = jnp.pad(masked, ((0, 0), (1, 1), (0, 0), (0, 0)), constant_values=MASK_VALUE)
    return jnp.concatenate([ext[:, :-1], ext[:, 1:]], axis=-1)


def _layer_params(l, p):
    w_in = p["w_in"][l]
    zcol = lambda n: jnp.zeros((D, n), F32)
    kr = w_in[:, C_KR:C_KR + MLA_ROPE]
    kr_slots = jnp.concatenate([jnp.concatenate([zcol(MLA_NOPE), kr, zcol(LANES - MLA_QK)], axis=1)] * 4, axis=1)
    gq_cols = w_in[:, C_GQ:C_GQ + 256].reshape(D, 4, HEAD_DIM)[:, jnp.array([0, 2, 1, 3])].reshape(D, 256)
    w_small = jnp.concatenate([
        w_in[:, C_CQ:C_CQ + MLA_Q_LORA], zcol(256 - MLA_Q_LORA),
        w_in[:, C_CKV:C_CKV + MLA_KV_LORA],
        kr_slots,
        w_in[:, C_NA:C_NA + 768],
        w_in[:, C_DF:C_DF + 768],
        gq_cols, w_in[:, C_GQ + 256:C_GQ + 512]], axis=1).astype(BF16)

    w_uq = p["w_mla_uq"][l].reshape(MLA_Q_LORA, 4, MLA_QK)
    w_uq = jnp.pad(w_uq, ((0, 256 - MLA_Q_LORA), (0, 0), (0, LANES - MLA_QK))).reshape(256, 512).astype(BF16)
    w_ukv = p["w_mla_ukv"][l].reshape(MLA_KV_LORA, 4, 2 * MLA_NOPE)
    w_uk = jnp.pad(w_ukv[:, :, :MLA_NOPE], ((0, 0), (0, 0), (0, LANES - MLA_NOPE))).reshape(MLA_KV_LORA, 512).astype(BF16)
    w_uv = w_ukv[:, :, MLA_NOPE:].reshape(MLA_KV_LORA, 256).astype(BF16)

    mla_gain = lambda g: jnp.tile(jnp.pad(g, (0, LANES - MLA_QK)), 4)
    row = lambda g: jnp.pad(g, (0, 512 - g.shape[0]))
    hd_scale = HEAD_DIM ** -0.5
    gains = jnp.stack([
        row(mla_gain(p["g_mla_q"][l]) * MLA_QK ** -0.5), row(mla_gain(p["g_mla_k"][l])),
        row(jnp.tile(p["g_na_q"][l], 4) * hd_scale), row(jnp.tile(p["g_na_k"][l], 4)),
        row(jnp.tile(p["g_diff_q"][l], 8) * DIFF_DIM ** -0.5), row(jnp.tile(p["g_diff_k"][l], 8)),
        row(jnp.tile(p["g_gqa_q"][l], 4) * hd_scale), row(jnp.tile(p["g_gqa_k"][l], 2))])

    w_branch = p["w_branch"][l]
    wb_gqa = w_branch[3].reshape(4, HEAD_DIM, D)[jnp.array([0, 2, 1, 3])].reshape(256, D)
    w_branch = jnp.concatenate([w_branch[:3], wb_gqa[None]], axis=0).astype(BF16)

    return dict(
        g1=p["g_norm1"][l][None], g2=p["g_norm2"][l][None],
        w_small=w_small, w_gate=w_in[:, C_GATE:].astype(BF16),
        g_qa=jnp.pad(p["g_mla_qa"][l], (0, 256 - MLA_Q_LORA))[None], g_kva=p["g_mla_kva"][l][None],
        w_uq=w_uq, w_uk=w_uk, w_uv=w_uv, gains=gains,
        na_pair=_na_pair_table(p["na_rpb"][l]),
        lam=jnp.stack([p["diff_lq1"][l], p["diff_lk1"][l], p["diff_lq2"][l], p["diff_lk2"][l]]),
        g_sub=jnp.tile(p["g_diff_sub"][l], 4)[None],
        w_branch=w_branch, w_out=p["w_out"][l].astype(BF16),
        w_up=p["w_up"][l].astype(BF16), w_down=p["w_down"][l].astype(BF16),
    )


def kernel(x, c, ctx, c_ctx, w_ada, b_ada, g_norm1, g_norm2, w_in, g_mla_qa, w_mla_uq, g_mla_kva, w_mla_ukv, g_mla_q, g_mla_k, g_na_q, g_na_k, na_rpb, g_diff_q, g_diff_k, diff_lq1, diff_lk1, diff_lq2, diff_lk2, g_diff_sub, g_gqa_q, g_gqa_k, w_branch, w_out, w_up, w_down):
    p = dict(w_in=w_in, g_norm1=g_norm1, g_norm2=g_norm2, g_mla_qa=g_mla_qa, w_mla_uq=w_mla_uq,
             g_mla_kva=g_mla_kva, w_mla_ukv=w_mla_ukv, g_mla_q=g_mla_q, g_mla_k=g_mla_k,
             g_na_q=g_na_q, g_na_k=g_na_k, na_rpb=na_rpb, g_diff_q=g_diff_q, g_diff_k=g_diff_k,
             diff_lq1=diff_lq1, diff_lk1=diff_lk1, diff_lq2=diff_lq2, diff_lk2=diff_lk2,
             g_diff_sub=g_diff_sub, g_gqa_q=g_gqa_q, g_gqa_k=g_gqa_k, w_branch=w_branch,
             w_out=w_out, w_up=w_up, w_down=w_down)
    n_batch = x.shape[0]
    depth = w_ada.shape[0]
    assert x.shape[1:] == (SEQ, D) and ctx.shape[1:] == (CTX, D)

    mod_rows = -(-(n_batch + 1) // 8) * 8
    c_all = jnp.concatenate([c, c_ctx[None], jnp.zeros((mod_rows - n_batch - 1, D), F32)], axis=0)
    mods = _ada_call(c_all, w_ada, b_ada).reshape(depth, mod_rows, 1, 6 * D)
    lat_tiles = SEQ // TM
    lat_row = lambda i: i // lat_tiles
    ctx_row = lambda i: n_batch

    consts = dict(bd128=_block_diag_ones(256, 128), bd64=_block_diag_ones(256, 64),
                  bd32=_block_diag_ones(256, 32), bd64h=_block_diag_ones(128, 64))
    tabs = _rope_tables()

    xl = x.reshape(n_batch * SEQ, D)
    xc = ctx.reshape(n_batch * CTX, D)
    for l in range(depth):
        need_ctx = l < depth - 1
        lam_init = 0.8 - 0.6 * math.exp(-0.3 * l)
        lw = _layer_params(l, p)
        mod = mods[l]

        lat = _qkv_call(xl, mod, lambda b: b, lw, consts, tabs, n_batch, SEQ, TM)
        cx = _qkv_call(xc, mod, ctx_row, lw, consts, None, n_batch, CTX, CTX)
        mq, mkt, mv, nq, nkt, nv, dq, dkt, dv, gq, gkt, gv = lat
        oa = _latent_attn_call(_mla_attn_kernel, "mla_attn", mq, mkt, mv, cx[1], cx[2], [], n_batch)
        ob = _na_attn_call(nq, nkt, nv, cx[4], cx[5], lw["na_pair"], n_batch)
        oc = _latent_attn_call(functools.partial(_diff_attn_kernel, lam_init=lam_init), "diff_attn",
                               dq, dkt, dv, cx[7], cx[8], [lw["lam"], lw["g_sub"]], n_batch)
        od = _latent_attn_call(_gqa_attn_kernel, "gqa_attn", gq, gkt, gv, cx[10], cx[11], [], n_batch)
        if need_ctx:
            oc_all = _ctx_attn_call(cx, lw["lam"], lw["g_sub"], lam_init, n_batch)
            xc = _merge_call(xc, mod, ctx_row, lw["g1"], oc_all, lw["w_gate"], lw["w_branch"], lw["w_out"], TM)
            xc = _mlp_call(xc, mod, ctx_row, lw["g2"], lw["w_up"], lw["w_down"], TM)
        xl = _merge_call(xl, mod, lat_row, lw["g1"], (oa, ob, oc, od), lw["w_gate"], lw["w_branch"], lw["w_out"], TM)
        xl = _mlp_call(xl, mod, lat_row, lw["g2"], lw["w_up"], lw["w_down"], TM)
    return xl.reshape(n_batch, SEQ, D)
```

```python
import functools
import math

import jax
import jax.numpy as jnp
from jax import lax
from jax.experimental import pallas as pl
from jax.experimental.pallas import tpu as pltpu

F32 = jnp.float32
BF16 = jnp.bfloat16

D = 1024
SEQ = 2048
CTX = 256
GRID_W = 64
N_ROWS = SEQ // GRID_W
ROPE_THETA = 10000.0
EPS = 1e-6
HEAD_DIM = 64
MLA_Q_LORA = 192
MLA_KV_LORA = 128
MLA_NOPE = 64
MLA_ROPE = 32
MLA_QK = MLA_NOPE + MLA_ROPE
NA_ROWS = 8
NA_COLS = 16
DIFF_DIM = 32
D_FF = 4 * D

LANES = 128
TM = 512
TQ = 2048
QC = 512
NA_TQ = 256
NA_WIN = 3
MASK_VALUE = -1e30
LOG2E = math.log2(math.e)
MAX_FIXED_SHIFT = 50.0

C_CQ, C_CKV, C_KR, C_NA, C_DF, C_GQ, C_GATE = 0, 192, 320, 352, 1120, 1888, 2400
Z_CQ, Z_CKV, Z_KR, Z_NA, Z_DF, Z_GQ, Z_W = 0, 256, 384, 896, 1664, 2432, 2944
G_MQ, G_MK, G_NQ, G_NK, G_DQ, G_DK, G_GQ, G_GK = range(8)
T_MLA, T_R32, T_R64 = 0, 3, 6

VMEM_LIMIT = 56 * 1024 * 1024


def _cparams(n_axes):
    return pltpu.CompilerParams(dimension_semantics=("arbitrary",) * n_axes,
                                vmem_limit_bytes=VMEM_LIMIT)


def _dot(a, b):
    return jnp.dot(a, b, preferred_element_type=F32)


def _const_spec(a):
    return pl.BlockSpec(a.shape, lambda *_: (0,) * a.ndim, pipeline_mode=pl.Buffered(1))


def _modnorm(x, g, shift, scale):
    y = x * lax.rsqrt(jnp.mean(x * x, axis=-1, keepdims=True) + EPS) * g
    return y * (1.0 + scale) + shift


def _ada_kernel(c_ref, w_ref, b_ref, o_ref):
    c = c_ref[...]
    s = c * jax.nn.sigmoid(c)
    o_ref[0] = _dot(s.astype(BF16), w_ref[0].astype(BF16)) + b_ref[0]


def _ada_call(c_all, w_ada, b_ada):
    n_layers = w_ada.shape[0]
    rows = c_all.shape[0]
    bn = 1536
    return pl.pallas_call(
        _ada_kernel,
        grid=(n_layers, 6 * D // bn),
        in_specs=[pl.BlockSpec((rows, D), lambda l, j: (0, 0)),
                  pl.BlockSpec((1, D, bn), lambda l, j: (l, 0, j)),
                  pl.BlockSpec((1, 1, bn), lambda l, j: (l, 0, j))],
        out_specs=pl.BlockSpec((1, rows, bn), lambda l, j: (l, 0, j)),
        out_shape=jax.ShapeDtypeStruct((n_layers, rows, 6 * D), F32),
        compiler_params=_cparams(2),
        name="ada",
    )(c_all, w_ada, b_ada.reshape(n_layers, 1, 6 * D))


def _seg_mean_sq(t, bd, inv_d):
    x2 = t * t
    hi = x2.astype(BF16)
    lo = (x2 - hi.astype(F32)).astype(BF16)
    return (_dot(hi, bd) + _dot(lo, bd)) * inv_d


def _qkv_kernel(*refs, rope):
    (x_ref, mod_ref, g1_ref, w_ref, bd128_ref, bd64_ref, bd32_ref, bd64h_ref,
     gqa_ref, gkva_ref, wuq_ref, wuk_ref, wuv_ref, gains_ref) = refs[:14]
    tab_ref = refs[14] if rope else None
    (mqt_ref, mk_ref, mvt_ref, nq_ref, nkt_ref, nv_ref,
     dqt_ref, dk_ref, dvt_ref, gqt_ref, gk_ref, gvt_ref) = refs[-12:]

    mod = mod_ref[0]
    xn = _modnorm(x_ref[...], g1_ref[...], mod[:, 0:D], mod[:, D:2 * D]).astype(BF16)

    def cols(lo, hi):
        return _dot(xn, w_ref[:, lo:hi])

    def put(o_ref, lo, val):
        o_ref[:, lo:lo + LANES] = val.astype(BF16)

    def put_t(o_ref, lo, val):
        o_ref[0, lo:lo + LANES, :] = val.T.astype(BF16)

    def section(t, bd, inv_d, gain_row, tab, shift, o_ref, store):
        width = t.shape[1]
        y = t * lax.rsqrt(_seg_mean_sq(t, bd, inv_d) + EPS) * gains_ref[gain_row:gain_row + 1, 0:width]
        for j in range(width // LANES):
            yc = y[:, j * LANES:(j + 1) * LANES]
            if rope and tab is not None:
                yc = (yc * tab_ref[tab] + pltpu.roll(yc, LANES - shift, 1) * tab_ref[tab + 1]
                      + pltpu.roll(yc, shift, 1) * tab_ref[tab + 2])
            store(o_ref, j * LANES, yc)

    za = cols(Z_CQ, Z_NA)
    cq = za[:, Z_CQ:Z_CQ + 256]
    cqn = cq * lax.rsqrt(jnp.sum(cq * cq, axis=-1, keepdims=True) * (1.0 / MLA_Q_LORA) + EPS) * gqa_ref[...]
    q = _dot(cqn.astype(BF16), wuq_ref[...])
    ckv = za[:, Z_CKV:Z_CKV + 128]
    kvn = (ckv * lax.rsqrt(jnp.mean(ckv * ckv, axis=-1, keepdims=True) + EPS) * gkva_ref[...]).astype(BF16)
    k = _dot(kvn, wuk_ref[...]) + za[:, Z_KR:Z_KR + 512]
    mvt_ref[0] = _dot(kvn, wuv_ref[...]).T.astype(BF16)
    bd128 = bd128_ref[...]
    for c in range(2):
        sl = slice(256 * c, 256 * c + 256)
        section(q[:, sl], bd128, 1.0 / MLA_QK, G_MQ, T_MLA, MLA_ROPE // 4, mqt_ref.at[:, sl, :], put_t)
        section(k[:, sl], bd128, 1.0 / MLA_QK, G_MK, T_MLA, MLA_ROPE // 4, mk_ref.at[:, sl], put)

    zn = cols(Z_NA, Z_DF)
    bd64 = bd64_ref[...]
    section(zn[:, 0:256], bd64, 1.0 / HEAD_DIM, G_NQ, None, 0, nq_ref, put)
    section(zn[:, 256:512], bd64, 1.0 / HEAD_DIM, G_NK, None, 0, nkt_ref, put_t)
    nv_ref[...] = zn[:, 512:768].astype(BF16)

    zd = cols(Z_DF, Z_GQ)
    bd32 = bd32_ref[...]
    section(zd[:, 0:256], bd32, 1.0 / DIFF_DIM, G_DQ, T_R32, DIFF_DIM // 4, dqt_ref, put_t)
    section(zd[:, 256:512], bd32, 1.0 / DIFF_DIM, G_DK, T_R32, DIFF_DIM // 4, dk_ref, put)
    dvt_ref[0] = zd[:, 512:768].T.astype(BF16)

    zg = cols(Z_GQ, Z_W)
    section(zg[:, 0:256], bd64, 1.0 / HEAD_DIM, G_GQ, T_R64, HEAD_DIM // 4, gqt_ref, put_t)
    section(zg[:, 256:384], bd64h_ref[...], 1.0 / HEAD_DIM, G_GK, T_R64, HEAD_DIM // 4, gk_ref, put)
    gvt_ref[0] = zg[:, 384:512].T.astype(BF16)


def _qkv_call(rows, mod, mod_row, lw, consts, tabs, n_batch, seq, tm):
    tiles = seq // tm
    row_map = lambda t, b: (b * tiles + t, 0)
    tok_spec = lambda w: pl.BlockSpec((tm, w), row_map)
    tr_spec = lambda w: pl.BlockSpec((1, w, tm), lambda t, b: (b, 0, t))
    tok_shape = lambda w: jax.ShapeDtypeStruct((n_batch * seq, w), BF16)
    tr_shape = lambda w: jax.ShapeDtypeStruct((n_batch, w, seq), BF16)
    small = [lw["g1"], lw["w_small"], consts["bd128"], consts["bd64"], consts["bd32"], consts["bd64h"],
             lw["g_qa"], lw["g_kva"], lw["w_uq"], lw["w_uk"], lw["w_uv"], lw["gains"]]
    in_specs = [tok_spec(D), pl.BlockSpec((1, 1, 6 * D), lambda t, b: (mod_row(b), 0, 0))]
    in_specs += [_const_spec(a) for a in small]
    args = [rows, mod] + small
    if tabs is not None:
        in_specs.append(pl.BlockSpec((tabs.shape[0], tm, LANES), lambda t, b: (0, t, 0)))
        args.append(tabs)
    return pl.pallas_call(
        functools.partial(_qkv_kernel, rope=tabs is not None),
        grid=(tiles, n_batch),
        in_specs=in_specs,
        out_specs=[tr_spec(512), tok_spec(512), tr_spec(256),
                   tok_spec(256), tr_spec(256), tok_spec(256),
                   tr_spec(256), tok_spec(256), tr_spec(256),
                   tr_spec(256), tok_spec(128), tr_spec(128)],
        out_shape=[tr_shape(512), tok_shape(512), tr_shape(256),
                   tok_shape(256), tr_shape(256), tok_shape(256),
                   tr_shape(256), tok_shape(256), tr_shape(256),
                   tr_shape(256), tok_shape(128), tr_shape(128)],
        compiler_params=_cparams(2),
        name="qkv",
    )(*args)


def _lane_mask(shape, lo, hi):
    lane = lax.broadcasted_iota(jnp.int32, shape, 1)
    return jnp.logical_and(lane >= lo, lane < hi)


def _keep_lanes(x, lo, hi):
    return jnp.where(_lane_mask(x.shape, lo, hi), x, jnp.zeros_like(x))


def _keep_rows(x, lo, hi):
    row = lax.broadcasted_iota(jnp.int32, x.shape, 0)
    return jnp.where(jnp.logical_and(row >= lo, row < hi), x, jnp.zeros_like(x))


def _sum_all(xs):
    return functools.reduce(lambda a, b: a + b, xs)


def _softmax_pv(qm, groups):
    s = []
    for kt, _, bias in groups:
        x = _dot(qm, kt)
        s.append(x if bias is None else x + bias)
    m = functools.reduce(jnp.maximum, [jnp.max(x, axis=-1, keepdims=True) for x in s])
    p = [jnp.exp2(x - m) for x in s]
    l = _sum_all([jnp.sum(x, axis=-1, keepdims=True) for x in p])
    return _sum_all([_dot(x.astype(BF16), g[1]) for x, g in zip(p, groups)]) / l


def _softmax_vt(qt, groups, shift):
    s = [_dot(k, qt) for k, _ in groups]
    if shift is None:
        shift = functools.reduce(jnp.maximum, [jnp.max(x, axis=0, keepdims=True) for x in s])
    p = [jnp.exp2(x - shift) for x in s]
    l = _sum_all([jnp.sum(x, axis=0, keepdims=True) for x in p])
    return _sum_all([_dot(g[1], x.astype(BF16)) for x, g in zip(p, groups)]) / l


def _mla_heads_t(qt_fn, groups_fn, shift):
    outs = []
    for h in range(4):
        c0 = 256 * (h // 2)
        lo = LANES * (h % 2)
        outs.append(_softmax_vt(_keep_rows(qt_fn(c0), lo, lo + LANES), groups_fn(h, c0), shift))
    return jnp.concatenate(outs, axis=0)


def _diff_heads_t(qt, groups_fn, lam, gsub_t, lam_init, shift):
    outs = []
    for h in range(4):
        maps = [_softmax_vt(_keep_rows(qt, 64 * h + DIFF_DIM * m, 64 * h + DIFF_DIM * (m + 1)), groups_fn(h), shift)
                for m in range(2)]
        oh = maps[0] - lam * maps[1]
        ms = jnp.mean(oh * oh, axis=0, keepdims=True)
        outs.append(oh * lax.rsqrt(ms + EPS) * gsub_t[64 * h:64 * h + 64, :])
    return jnp.concatenate(outs, axis=0) * (1.0 - lam_init)


def _gqa_heads_t(qt, groups_fn, shift):
    outs = []
    for g in range(2):
        qc = qt[LANES * g:LANES * (g + 1), :]
        for n in range(2):
            outs.append(_softmax_vt(_keep_rows(qc, 64 * n, 64 * n + 64), groups_fn(n), shift))
    return jnp.concatenate(outs, axis=0)


def _plain_heads(q, groups_fn):
    acc = jnp.zeros((q.shape[0], 256), F32)
    for h in range(4):
        pv = _softmax_pv(_keep_lanes(q, 64 * h, 64 * h + 64), groups_fn(h))
        acc = jnp.where(_lane_mask(acc.shape, 64 * h, 64 * h + 64), pv, acc)
    return acc


def _diff_lambda(lam_ref, lam_init):
    lq1, lk1, lq2, lk2 = (lam_ref[i:i + 1, :] for i in range(4))
    return (jnp.exp(jnp.sum(lq1 * lk1, axis=-1, keepdims=True))
            - jnp.exp(jnp.sum(lq2 * lk2, axis=-1, keepdims=True)) + lam_init)


def _latent_attention(shift_ref, chunk_fn):
    bound = shift_ref[0]

    def run(shift):
        def step(c, carry):
            chunk_fn(pl.ds(pl.multiple_of(c * QC, QC), QC), shift)
            return carry

        lax.fori_loop(0, TQ // QC, step, 0)

    @pl.when(bound <= MAX_FIXED_SHIFT)
    def _():
        run(bound)

    @pl.when(jnp.logical_not(bound <= MAX_FIXED_SHIFT))
    def _():
        run(None)


def _mla_attn_kernel(shift_ref, qt_ref, kl_ref, vtl_ref, kc_ref, vtc_ref, o_ref):
    def chunk(cols, shift):
        def groups_fn(h, c0):
            dims = slice(64 * h, 64 * h + 64)
            return [(kl_ref[:, c0:c0 + 256], vtl_ref[0, dims, :]), (kc_ref[:, c0:c0 + 256], vtc_ref[0, dims, :])]

        ot = _mla_heads_t(lambda c0: qt_ref[0, c0:c0 + 256, cols], groups_fn, shift)
        o_ref[cols, :] = ot.T.astype(BF16)

    _latent_attention(shift_ref, chunk)


def _diff_attn_kernel(shift_ref, qt_ref, kl_ref, vtl_ref, kc_ref, vtc_ref, lam_ref, gsubt_ref, o_ref, *, lam_init):
    def chunk(cols, shift):
        def groups_fn(h):
            dims = slice(64 * h, 64 * h + 64)
            return [(kl_ref[...], vtl_ref[0, dims, :]), (kc_ref[...], vtc_ref[0, dims, :])]

        lam = _diff_lambda(lam_ref, lam_init)
        ot = _diff_heads_t(qt_ref[0, :, cols], groups_fn, lam, gsubt_ref[...], lam_init, shift)
        o_ref[cols, :] = ot.T.astype(BF16)

    _latent_attention(shift_ref, chunk)


def _gqa_attn_kernel(shift_ref, qt_ref, kl_ref, vtl_ref, kc_ref, vtc_ref, o_ref):
    def chunk(cols, shift):
        def groups_fn(n):
            dims = slice(64 * n, 64 * n + 64)
            return [(kl_ref[...], vtl_ref[0, dims, :]), (kc_ref[...], vtc_ref[0, dims, :])]

        ot = _gqa_heads_t(qt_ref[0, :, cols], groups_fn, shift)
        o_ref[cols, :] = ot.T.astype(BF16)

    _latent_attention(shift_ref, chunk)


def _latent_attn_call(kernel, name, shift, qt, kl, vtl, kc, vtc, extra, n_batch):
    wq, wk, wv = qt.shape[1], kl.shape[1], vtl.shape[1]
    tiles = SEQ // TQ
    return pl.pallas_call(
        kernel,
        grid=(n_batch, tiles),
        in_specs=[pl.BlockSpec(memory_space=pltpu.SMEM),
                  pl.BlockSpec((1, wq, TQ), lambda b, t: (b, 0, t)),
                  pl.BlockSpec((SEQ, wk), lambda b, t: (b, 0)),
                  pl.BlockSpec((1, wv, SEQ), lambda b, t: (b, 0, 0)),
                  pl.BlockSpec((CTX, wk), lambda b, t: (b, 0)),
                  pl.BlockSpec((1, wv, CTX), lambda b, t: (b, 0, 0))] + [_const_spec(a) for a in extra],
        out_specs=pl.BlockSpec((TQ, 256), lambda b, t: (b * tiles + t, 0)),
        out_shape=jax.ShapeDtypeStruct((n_batch * SEQ, 256), BF16),
        compiler_params=_cparams(2),
        name=name,
    )(shift, qt, kl, vtl, kc, vtc, *extra)


def _na_window(t):
    return jnp.clip(t - 1, 0, SEQ // NA_TQ - NA_WIN)


def _na_build_bias(t, pair_ref, bias_ref):
    rows_per_tile = NA_TQ // GRID_W
    k_row0 = rows_per_tile * _na_window(t)
    lane = lax.broadcasted_iota(jnp.int32, (GRID_W, LANES), 1)
    for ql in range(rows_per_tile):
        qr = rows_per_tile * t + ql
        r0 = jnp.clip(qr - NA_ROWS // 2, 0, N_ROWS - NA_ROWS)
        for j in range(NA_WIN * NA_TQ // LANES):
            kr = k_row0 + 2 * j
            ok0 = jnp.logical_and(kr >= r0, kr < r0 + NA_ROWS).astype(jnp.int32)
            ok1 = jnp.logical_and(kr + 1 >= r0, kr + 1 < r0 + NA_ROWS).astype(jnp.int32)
            ok = jnp.where(lane < GRID_W, ok0, ok1) > 0
            d = jnp.clip(kr - qr + NA_ROWS - 1, -1, 2 * NA_ROWS - 2) + 1
            for h in range(4):
                blk = jnp.where(ok, pair_ref[h, d], MASK_VALUE)
                bias_ref[h, GRID_W * ql:GRID_W * (ql + 1), LANES * j:LANES * (j + 1)] = blk


def _na_attn_kernel(q_ref, k0_ref, k1_ref, k2_ref, kc_ref, v0_ref, v1_ref, v2_ref, vc_ref, pair_ref, o_ref, bias_ref):
    t = pl.program_id(0)

    @pl.when(pl.program_id(1) == 0)
    def _():
        _na_build_bias(t, pair_ref, bias_ref)

    win = ((k0_ref, v0_ref), (k1_ref, v1_ref), (k2_ref, v2_ref))
    groups_fn = lambda h: ([(kt[0], v[...], bias_ref[h, :, NA_TQ * j:NA_TQ * (j + 1)]) for j, (kt, v) in enumerate(win)]
                           + [(kc_ref[0], vc_ref[...], None)])
    o_ref[...] = _plain_heads(q_ref[...], groups_fn).astype(BF16)


def _na_attn_call(q, ktl, vl, ktc, vc, pair, n_batch):
    tiles = SEQ // NA_TQ
    kt_spec = lambda j: pl.BlockSpec((1, 256, NA_TQ), lambda t, b: (b, 0, _na_window(t) + j))
    v_spec = lambda j: pl.BlockSpec((NA_TQ, 256), lambda t, b: (b * tiles + _na_window(t) + j, 0))
    return pl.pallas_call(
        _na_attn_kernel,
        grid=(tiles, n_batch),
        in_specs=[pl.BlockSpec((NA_TQ, 256), lambda t, b: (b * tiles + t, 0)),
                  kt_spec(0), kt_spec(1), kt_spec(2),
                  pl.BlockSpec((1, 256, CTX), lambda t, b: (b, 0, 0)),
                  v_spec(0), v_spec(1), v_spec(2),
                  pl.BlockSpec((CTX, 256), lambda t, b: (b, 0)),
                  _const_spec(pair)],
        out_specs=pl.BlockSpec((NA_TQ, 256), lambda t, b: (b * tiles + t, 0)),
        out_shape=jax.ShapeDtypeStruct((n_batch * SEQ, 256), BF16),
        scratch_shapes=[pltpu.VMEM((4, NA_TQ, NA_WIN * NA_TQ), F32)],
        compiler_params=_cparams(2),
        name="na_attn",
    )(q, ktl, ktl, ktl, ktc, vl, vl, vl, vc, pair)


def _ctx_attn_kernel(mqt_ref, mk_ref, mvt_ref, nq_ref, nkt_ref, nv_ref, dqt_ref, dk_ref, dvt_ref,
                     gqt_ref, gk_ref, gvt_ref, lam_ref, gsubt_ref, oa_ref, ob_ref, oc_ref, od_ref, *, lam_init):
    head_dims = lambda h: slice(64 * h, 64 * h + 64)
    oa_ref[...] = _mla_heads_t(lambda c0: mqt_ref[0, c0:c0 + 256, :],
                               lambda h, c0: [(mk_ref[:, c0:c0 + 256], mvt_ref[0, head_dims(h), :])],
                               None).T.astype(BF16)
    ob_ref[...] = _plain_heads(nq_ref[...], lambda h: [(nkt_ref[0], nv_ref[...], None)]).astype(BF16)
    lam = _diff_lambda(lam_ref, lam_init)
    oc_ref[...] = _diff_heads_t(dqt_ref[0], lambda h: [(dk_ref[...], dvt_ref[0, head_dims(h), :])], lam,
                                gsubt_ref[...], lam_init, None).T.astype(BF16)
    od_ref[...] = _gqa_heads_t(gqt_ref[0], lambda n: [(gk_ref[...], gvt_ref[0, head_dims(n), :])], None).T.astype(BF16)


def _ctx_attn_call(qkv_c, lam, gsub, lam_init, n_batch):
    specs = []
    for a in qkv_c:
        if a.ndim == 3:
            specs.append(pl.BlockSpec((1, a.shape[1], CTX), lambda b: (b, 0, 0)))
        else:
            specs.append(pl.BlockSpec((CTX, a.shape[1]), lambda b: (b, 0)))
    out = jax.ShapeDtypeStruct((n_batch * CTX, 256), BF16)
    return pl.pallas_call(
        functools.partial(_ctx_attn_kernel, lam_init=lam_init),
        grid=(n_batch,),
        in_specs=specs + [_const_spec(lam), _const_spec(gsub)],
        out_specs=[pl.BlockSpec((CTX, 256), lambda b: (b, 0))] * 4,
        out_shape=[out] * 4,
        compiler_params=_cparams(1),
        name="ctx_attn",
    )(*qkv_c, lam, gsub)


def _merge_kernel(x_ref, mod_ref, g_ref, oa_ref, ob_ref, oc_ref, od_ref, wg_ref, wb_ref, wo_ref, out_ref):
    x = x_ref[...]
    mod = mod_ref[0]
    xn = _modnorm(x, g_ref[...], mod[:, 0:D], mod[:, D:2 * D]).astype(BF16)
    acc = jnp.zeros(x.shape, F32)
    for n, o_ref in enumerate((oa_ref, ob_ref, oc_ref, od_ref)):
        gate = jax.nn.sigmoid(_dot(xn, wg_ref[:, n * D:(n + 1) * D]))
        acc = acc + gate * _dot(o_ref[...], wb_ref[n])
    y = _dot(acc.astype(BF16), wo_ref[...])
    out_ref[...] = x + mod[:, 2 * D:3 * D] * y


def _merge_call(rows, mod, mod_row, g1, branches, w_gate, w_branch, w_out, tm):
    n_rows = rows.shape[0]
    tm = min(tm, n_rows)
    tok = lambda w: pl.BlockSpec((tm, w), lambda i: (i, 0))
    return pl.pallas_call(
        _merge_kernel,
        grid=(n_rows // tm,),
        in_specs=[tok(D), pl.BlockSpec((1, 1, 6 * D), lambda i: (mod_row(i), 0, 0)), _const_spec(g1)]
                 + [tok(256)] * 4 + [_const_spec(w_gate), _const_spec(w_branch), _const_spec(w_out)],
        out_specs=tok(D),
        out_shape=jax.ShapeDtypeStruct((n_rows, D), F32),
        compiler_params=_cparams(1),
        name="merge",
    )(rows, mod, g1, *branches, w_gate, w_branch, w_out)


def _mlp_kernel(x_ref, mod_ref, g_ref, wu_ref, wd_ref, out_ref):
    x = x_ref[...]
    mod = mod_ref[0]
    xn = _modnorm(x, g_ref[...], mod[:, 3 * D:4 * D], mod[:, 4 * D:5 * D]).astype(BF16)
    h = jnp.square(jnp.maximum(_dot(xn, wu_ref[...]), 0.0))
    out_ref[...] = x + mod[:, 5 * D:6 * D] * _dot(h.astype(BF16), wd_ref[...])


def _mlp_call(rows, mod, mod_row, g2, w_up, w_down, tm):
    n_rows = rows.shape[0]
    tm = min(tm, n_rows)
    tok = lambda w: pl.BlockSpec((tm, w), lambda i: (i, 0))
    return pl.pallas_call(
        _mlp_kernel,
        grid=(n_rows // tm,),
        in_specs=[tok(D), pl.BlockSpec((1, 1, 6 * D), lambda i: (mod_row(i), 0, 0)), _const_spec(g2),
                  _const_spec(w_up), _const_spec(w_down)],
        out_specs=tok(D),
        out_shape=jax.ShapeDtypeStruct((n_rows, D), F32),
        compiler_params=_cparams(1),
        name="mlp",
    )(rows, mod, g2, w_up, w_down)


def _block_diag_ones(n, seg):
    i = jnp.arange(n) // seg
    return (i[:, None] == i[None, :]).astype(BF16)


def _rope_parts(rot_dim):
    t = jnp.arange(SEQ)
    rows, cols = t // GRID_W, t % GRID_W
    n = rot_dim // 4
    inv_freq = jnp.power(ROPE_THETA, -jnp.arange(n, dtype=F32) / n)
    ang_r = rows.astype(F32)[:, None] * inv_freq
    ang_c = cols.astype(F32)[:, None] * inv_freq
    ang = jnp.concatenate([ang_r, ang_r, ang_c, ang_c], axis=-1)
    cos, sin = jnp.cos(ang), jnp.sin(ang)
    even = (jnp.arange(rot_dim) // n) % 2 == 0
    return cos, jnp.where(even, -sin, 0.0), jnp.where(even, 0.0, sin)


def _rope_tables():
    r32, r64 = _rope_parts(MLA_ROPE), _rope_parts(HEAD_DIM)
    ones = jnp.ones((SEQ, MLA_NOPE), F32)
    zeros_n = jnp.zeros((SEQ, MLA_NOPE), F32)
    pad = jnp.zeros((SEQ, LANES - MLA_QK), F32)
    mla = [jnp.concatenate([lead, part, pad], axis=1) for lead, part in zip((ones, zeros_n, zeros_n), r32)]
    return jnp.stack(mla + [jnp.tile(a, (1, LANES // MLA_ROPE)) for a in r32]
                     + [jnp.tile(a, (1, LANES // HEAD_DIM)) for a in r64])


def _na_pair_table(rpb):
    qc = jnp.arange(GRID_W)[:, None]
    kc = jnp.arange(GRID_W)[None, :]
    onehot = (kc - qc + NA_COLS - 1)[None] == jnp.arange(2 * NA_COLS - 1)[:, None, None]
    toeplitz = jnp.einsum("hrd,dqk->hrqk", rpb, onehot.astype(F32), precision=lax.Precision.HIGHEST)
    c0 = jnp.clip(qc - NA_COLS // 2, 0, GRID_W - NA_COLS)
    in_window = (kc >= c0) & (kc < c0 + NA_COLS)
    masked = jnp.where(in_window, toeplitz * LOG2E, MASK_VALUE)
    ext = jnp.pad(masked, ((0, 0), (1, 1), (0, 0), (0, 0)), constant_values=MASK_VALUE)
    return jnp.concatenate([ext[:, :-1], ext[:, 1:]], axis=-1)


def _layer_params(l, p):
    w_in = p["w_in"][l]
    zcol = lambda n: jnp.zeros((D, n), F32)
    kr = w_in[:, C_KR:C_KR + MLA_ROPE]
    kr_slots = jnp.concatenate([jnp.concatenate([zcol(MLA_NOPE), kr, zcol(LANES - MLA_QK)], axis=1)] * 4, axis=1)
    gq_cols = w_in[:, C_GQ:C_GQ + 256].reshape(D, 4, HEAD_DIM)[:, jnp.array([0, 2, 1, 3])].reshape(D, 256)
    w_small = jnp.concatenate([
        w_in[:, C_CQ:C_CQ + MLA_Q_LORA], zcol(256 - MLA_Q_LORA),
        w_in[:, C_CKV:C_CKV + MLA_KV_LORA],
        kr_slots,
        w_in[:, C_NA:C_NA + 768],
        w_in[:, C_DF:C_DF + 768],
        gq_cols, w_in[:, C_GQ + 256:C_GQ + 512]], axis=1).astype(BF16)

    w_uq = p["w_mla_uq"][l].reshape(MLA_Q_LORA, 4, MLA_QK)
    w_uq = jnp.pad(w_uq, ((0, 256 - MLA_Q_LORA), (0, 0), (0, LANES - MLA_QK))).reshape(256, 512).astype(BF16)
    w_ukv = p["w_mla_ukv"][l].reshape(MLA_KV_LORA, 4, 2 * MLA_NOPE)
    w_uk = jnp.pad(w_ukv[:, :, :MLA_NOPE], ((0, 0), (0, 0), (0, LANES - MLA_NOPE))).reshape(MLA_KV_LORA, 512).astype(BF16)
    w_uv = w_ukv[:, :, MLA_NOPE:].reshape(MLA_KV_LORA, 256).astype(BF16)

    mla_gain = lambda g: jnp.tile(jnp.pad(g, (0, LANES - MLA_QK)), 4)
    row = lambda g: jnp.pad(g, (0, 512 - g.shape[0]))
    hd_scale = HEAD_DIM ** -0.5 * LOG2E
    gains = jnp.stack([
        row(mla_gain(p["g_mla_q"][l]) * (MLA_QK ** -0.5 * LOG2E)), row(mla_gain(p["g_mla_k"][l])),
        row(jnp.tile(p["g_na_q"][l], 4) * hd_scale), row(jnp.tile(p["g_na_k"][l], 4)),
        row(jnp.tile(p["g_diff_q"][l], 8) * (DIFF_DIM ** -0.5 * LOG2E)), row(jnp.tile(p["g_diff_k"][l], 8)),
        row(jnp.tile(p["g_gqa_q"][l], 4) * hd_scale), row(jnp.tile(p["g_gqa_k"][l], 2))])

    def score_bound(dim, g_q, g_k):
        return (LOG2E * dim ** 0.5 * jnp.max(jnp.abs(g_q)) * jnp.max(jnp.abs(g_k))).reshape(1).astype(F32)

    w_branch = p["w_branch"][l]
    wb_gqa = w_branch[3].reshape(4, HEAD_DIM, D)[jnp.array([0, 2, 1, 3])].reshape(256, D)
    w_branch = jnp.concatenate([w_branch[:3], wb_gqa[None]], axis=0).astype(BF16)

    return dict(
        g1=p["g_norm1"][l][None], g2=p["g_norm2"][l][None],
        w_small=w_small, w_gate=w_in[:, C_GATE:].astype(BF16),
        g_qa=jnp.pad(p["g_mla_qa"][l], (0, 256 - MLA_Q_LORA))[None], g_kva=p["g_mla_kva"][l][None],
        w_uq=w_uq, w_uk=w_uk, w_uv=w_uv, gains=gains,
        mla_bound=score_bound(MLA_QK, p["g_mla_q"][l], p["g_mla_k"][l]),
        diff_bound=score_bound(DIFF_DIM, p["g_diff_q"][l], p["g_diff_k"][l]),
        gqa_bound=score_bound(HEAD_DIM, p["g_gqa_q"][l], p["g_gqa_k"][l]),
        na_pair=_na_pair_table(p["na_rpb"][l]),
        lam=jnp.stack([p["diff_lq1"][l], p["diff_lk1"][l], p["diff_lq2"][l], p["diff_lk2"][l]]),
        g_sub=jnp.tile(p["g_diff_sub"][l], 4)[:, None],
        w_branch=w_branch, w_out=p["w_out"][l].astype(BF16),
        w_up=p["w_up"][l].astype(BF16), w_down=p["w_down"][l].astype(BF16),
    )


def kernel(x, c, ctx, c_ctx, w_ada, b_ada, g_norm1, g_norm2, w_in, g_mla_qa, w_mla_uq, g_mla_kva, w_mla_ukv, g_mla_q, g_mla_k, g_na_q, g_na_k, na_rpb, g_diff_q, g_diff_k, diff_lq1, diff_lk1, diff_lq2, diff_lk2, g_diff_sub, g_gqa_q, g_gqa_k, w_branch, w_out, w_up, w_down):
    p = dict(w_in=w_in, g_norm1=g_norm1, g_norm2=g_norm2, g_mla_qa=g_mla_qa, w_mla_uq=w_mla_uq,
             g_mla_kva=g_mla_kva, w_mla_ukv=w_mla_ukv, g_mla_q=g_mla_q, g_mla_k=g_mla_k,
             g_na_q=g_na_q, g_na_k=g_na_k, na_rpb=na_rpb, g_diff_q=g_diff_q, g_diff_k=g_diff_k,
             diff_lq1=diff_lq1, diff_lk1=diff_lk1, diff_lq2=diff_lq2, diff_lk2=diff_lk2,
             g_diff_sub=g_diff_sub, g_gqa_q=g_gqa_q, g_gqa_k=g_gqa_k, w_branch=w_branch,
             w_out=w_out, w_up=w_up, w_down=w_down)
    n_batch = x.shape[0]
    depth = w_ada.shape[0]
    assert x.shape[1:] == (SEQ, D) and ctx.shape[1:] == (CTX, D)

    mod_rows = -(-(n_batch + 1) // 8) * 8
    c_all = jnp.concatenate([c, c_ctx[None], jnp.zeros((mod_rows - n_batch - 1, D), F32)], axis=0)
    mods = _ada_call(c_all, w_ada, b_ada).reshape(depth, mod_rows, 1, 6 * D)
    lat_tiles = SEQ // TM
    lat_row = lambda i: i // lat_tiles
    ctx_row = lambda i: n_batch

    consts = dict(bd128=_block_diag_ones(256, 128), bd64=_block_diag_ones(256, 64),
                  bd32=_block_diag_ones(256, 32), bd64h=_block_diag_ones(128, 64))
    tabs = _rope_tables()

    xl = x.reshape(n_batch * SEQ, D)
    xc = ctx.reshape(n_batch * CTX, D)
    for l in range(depth):
        need_ctx = l < depth - 1
        lam_init = 0.8 - 0.6 * math.exp(-0.3 * l)
        lw = _layer_params(l, p)
        mod = mods[l]

        lat = _qkv_call(xl, mod, lambda b: b, lw, consts, tabs, n_batch, SEQ, TM)
        cx = _qkv_call(xc, mod, ctx_row, lw, consts, None, n_batch, CTX, CTX)
        mqt, mk, mvt, nq, nkt, nv, dqt, dk, dvt, gqt, gk, gvt = lat
        oa = _latent_attn_call(_mla_attn_kernel, "mla_attn", lw["mla_bound"], mqt, mk, mvt, cx[1], cx[2], [], n_batch)
        ob = _na_attn_call(nq, nkt, nv, cx[4], cx[5], lw["na_pair"], n_batch)
        oc = _latent_attn_call(functools.partial(_diff_attn_kernel, lam_init=lam_init), "diff_attn",
                               lw["diff_bound"], dqt, dk, dvt, cx[7], cx[8], [lw["lam"], lw["g_sub"]], n_batch)
        od = _latent_attn_call(_gqa_attn_kernel, "gqa_attn", lw["gqa_bound"], gqt, gk, gvt, cx[10], cx[11], [],
                               n_batch)
        if need_ctx:
            oc_all = _ctx_attn_call(cx, lw["lam"], lw["g_sub"], lam_init, n_batch)
            xc = _merge_call(xc, mod, ctx_row, lw["g1"], oc_all, lw["w_gate"], lw["w_branch"], lw["w_out"], TM)
            xc = _mlp_call(xc, mod, ctx_row, lw["g2"], lw["w_up"], lw["w_down"], TM)
        xl = _merge_call(xl, mod, lat_row, lw["g1"], (oa, ob, oc, od), lw["w_gate"], lw["w_branch"], lw["w_out"], TM)
        xl = _mlp_call(xl, mod, lat_row, lw["g2"], lw["w_up"], lw["w_down"], TM)
    return xl.reshape(n_batch, SEQ, D)
```

```python
import functools
import math

import jax
import jax.numpy as jnp
from jax import lax
from jax.experimental import pallas as pl
from jax.experimental.pallas import tpu as pltpu

F32 = jnp.float32
BF16 = jnp.bfloat16

D = 1024
SEQ = 2048
CTX = 256
GRID_W = 64
N_ROWS = SEQ // GRID_W
ROPE_THETA = 10000.0
EPS = 1e-6
HEAD_DIM = 64
MLA_Q_LORA = 192
MLA_KV_LORA = 128
MLA_NOPE = 64
MLA_ROPE = 32
MLA_QK = MLA_NOPE + MLA_ROPE
NA_ROWS = 8
NA_COLS = 16
DIFF_DIM = 32
D_FF = 4 * D

LANES = 128
TM = 512
TQ = 2048
QC = 1024
NA_TQ = 256
NA_WIN = 3
MASK_VALUE = -1e30
LOG2E = math.log2(math.e)
MAX_FIXED_SHIFT = 50.0

C_CQ, C_CKV, C_KR, C_NA, C_DF, C_GQ, C_GATE = 0, 192, 320, 352, 1120, 1888, 2400
Z_CQ, Z_CKV, Z_KR, Z_NA, Z_DF, Z_GQ, Z_W = 0, 256, 384, 896, 1664, 2432, 2944
G_MQ, G_MK, G_NQ, G_NK, G_DQ, G_DK, G_GQ, G_GK = range(8)
T_MLA, T_R32, T_R64 = 0, 3, 6

VMEM_LIMIT = 56 * 1024 * 1024


def _cparams(n_axes):
    return pltpu.CompilerParams(dimension_semantics=("arbitrary",) * n_axes,
                                vmem_limit_bytes=VMEM_LIMIT)


def _dot(a, b):
    return jnp.dot(a, b, preferred_element_type=F32)


def _const_spec(a):
    return pl.BlockSpec(a.shape, lambda *_: (0,) * a.ndim, pipeline_mode=pl.Buffered(1))


def _modnorm(x, g, shift, scale):
    y = x * lax.rsqrt(jnp.mean(x * x, axis=-1, keepdims=True) + EPS) * g
    return y * (1.0 + scale) + shift


def _ada_kernel(c_ref, w_ref, b_ref, o_ref):
    c = c_ref[...]
    s = c * jax.nn.sigmoid(c)
    o_ref[0] = _dot(s.astype(BF16), w_ref[0].astype(BF16)) + b_ref[0]


def _ada_call(c_all, w_ada, b_ada):
    n_layers = w_ada.shape[0]
    rows = c_all.shape[0]
    bn = 1536
    return pl.pallas_call(
        _ada_kernel,
        grid=(n_layers, 6 * D // bn),
        in_specs=[pl.BlockSpec((rows, D), lambda l, j: (0, 0)),
                  pl.BlockSpec((1, D, bn), lambda l, j: (l, 0, j)),
                  pl.BlockSpec((1, 1, bn), lambda l, j: (l, 0, j))],
        out_specs=pl.BlockSpec((1, rows, bn), lambda l, j: (l, 0, j)),
        out_shape=jax.ShapeDtypeStruct((n_layers, rows, 6 * D), F32),
        compiler_params=_cparams(2),
        name="ada",
    )(c_all, w_ada, b_ada.reshape(n_layers, 1, 6 * D))


def _seg_mean_sq(t, bd, inv_d):
    x2 = t * t
    hi = x2.astype(BF16)
    lo = (x2 - hi.astype(F32)).astype(BF16)
    return (_dot(hi, bd) + _dot(lo, bd)) * inv_d


def _qkv_kernel(*refs, rope):
    (x_ref, mod_ref, g1_ref, w_ref, bd128_ref, bd64_ref, bd32_ref, bd64h_ref,
     gqa_ref, gkva_ref, wuq_ref, wuk_ref, wuv_ref, gains_ref) = refs[:14]
    tab_ref = refs[14] if rope else None
    (mqt_ref, mk_ref, mvt_ref, nq_ref, nkt_ref, nv_ref,
     dqt_ref, dk_ref, dvt_ref, gqt_ref, gk_ref, gvt_ref) = refs[-12:]

    mod = mod_ref[0]
    xn = _modnorm(x_ref[...], g1_ref[...], mod[:, 0:D], mod[:, D:2 * D]).astype(BF16)

    def cols(lo, hi):
        return _dot(xn, w_ref[:, lo:hi])

    def put(o_ref, lo, val):
        o_ref[:, lo:lo + LANES] = val.astype(BF16)

    def put_t(o_ref, lo, val):
        o_ref[0, lo:lo + LANES, :] = val.T.astype(BF16)

    def section(t, bd, inv_d, gain_row, tab, shift, o_ref, store):
        width = t.shape[1]
        y = t * lax.rsqrt(_seg_mean_sq(t, bd, inv_d) + EPS) * gains_ref[gain_row:gain_row + 1, 0:width]
        for j in range(width // LANES):
            yc = y[:, j * LANES:(j + 1) * LANES]
            if rope and tab is not None:
                yc = (yc * tab_ref[tab] + pltpu.roll(yc, LANES - shift, 1) * tab_ref[tab + 1]
                      + pltpu.roll(yc, shift, 1) * tab_ref[tab + 2])
            store(o_ref, j * LANES, yc)

    za = cols(Z_CQ, Z_NA)
    cq = za[:, Z_CQ:Z_CQ + 256]
    cqn = cq * lax.rsqrt(jnp.sum(cq * cq, axis=-1, keepdims=True) * (1.0 / MLA_Q_LORA) + EPS) * gqa_ref[...]
    q = _dot(cqn.astype(BF16), wuq_ref[...])
    ckv = za[:, Z_CKV:Z_CKV + 128]
    kvn = (ckv * lax.rsqrt(jnp.mean(ckv * ckv, axis=-1, keepdims=True) + EPS) * gkva_ref[...]).astype(BF16)
    k = _dot(kvn, wuk_ref[...]) + za[:, Z_KR:Z_KR + 512]
    mvt_ref[0] = _dot(kvn, wuv_ref[...]).T.astype(BF16)
    bd128 = bd128_ref[...]
    for c in range(2):
        sl = slice(256 * c, 256 * c + 256)
        section(q[:, sl], bd128, 1.0 / MLA_QK, G_MQ, T_MLA, MLA_ROPE // 4, mqt_ref.at[:, sl, :], put_t)
        section(k[:, sl], bd128, 1.0 / MLA_QK, G_MK, T_MLA, MLA_ROPE // 4, mk_ref.at[:, sl], put)

    zn = cols(Z_NA, Z_DF)
    bd64 = bd64_ref[...]
    section(zn[:, 0:256], bd64, 1.0 / HEAD_DIM, G_NQ, None, 0, nq_ref, put)
    section(zn[:, 256:512], bd64, 1.0 / HEAD_DIM, G_NK, None, 0, nkt_ref, put_t)
    nv_ref[...] = zn[:, 512:768].astype(BF16)

    zd = cols(Z_DF, Z_GQ)
    bd32 = bd32_ref[...]
    section(zd[:, 0:256], bd32, 1.0 / DIFF_DIM, G_DQ, T_R32, DIFF_DIM // 4, dqt_ref, put_t)
    section(zd[:, 256:512], bd32, 1.0 / DIFF_DIM, G_DK, T_R32, DIFF_DIM // 4, dk_ref, put)
    dvt_ref[0] = zd[:, 512:768].T.astype(BF16)

    zg = cols(Z_GQ, Z_W)
    section(zg[:, 0:256], bd64, 1.0 / HEAD_DIM, G_GQ, T_R64, HEAD_DIM // 4, gqt_ref, put_t)
    section(zg[:, 256:384], bd64h_ref[...], 1.0 / HEAD_DIM, G_GK, T_R64, HEAD_DIM // 4, gk_ref, put)
    gvt_ref[0] = zg[:, 384:512].T.astype(BF16)


def _qkv_call(rows, mod, mod_row, lw, consts, tabs, n_batch, seq, tm):
    tiles = seq // tm
    row_map = lambda t, b: (b * tiles + t, 0)
    tok_spec = lambda w: pl.BlockSpec((tm, w), row_map)
    tr_spec = lambda w: pl.BlockSpec((1, w, tm), lambda t, b: (b, 0, t))
    tok_shape = lambda w: jax.ShapeDtypeStruct((n_batch * seq, w), BF16)
    tr_shape = lambda w: jax.ShapeDtypeStruct((n_batch, w, seq), BF16)
    small = [lw["g1"], lw["w_small"], consts["bd128"], consts["bd64"], consts["bd32"], consts["bd64h"],
             lw["g_qa"], lw["g_kva"], lw["w_uq"], lw["w_uk"], lw["w_uv"], lw["gains"]]
    in_specs = [tok_spec(D), pl.BlockSpec((1, 1, 6 * D), lambda t, b: (mod_row(b), 0, 0))]
    in_specs += [_const_spec(a) for a in small]
    args = [rows, mod] + small
    if tabs is not None:
        in_specs.append(pl.BlockSpec((tabs.shape[0], tm, LANES), lambda t, b: (0, t, 0)))
        args.append(tabs)
    return pl.pallas_call(
        functools.partial(_qkv_kernel, rope=tabs is not None),
        grid=(tiles, n_batch),
        in_specs=in_specs,
        out_specs=[tr_spec(512), tok_spec(512), tr_spec(256),
                   tok_spec(256), tr_spec(256), tok_spec(256),
                   tr_spec(256), tok_spec(256), tr_spec(256),
                   tr_spec(256), tok_spec(128), tr_spec(128)],
        out_shape=[tr_shape(512), tok_shape(512), tr_shape(256),
                   tok_shape(256), tr_shape(256), tok_shape(256),
                   tr_shape(256), tok_shape(256), tr_shape(256),
                   tr_shape(256), tok_shape(128), tr_shape(128)],
        compiler_params=_cparams(2),
        name="qkv",
    )(*args)


def _lane_mask(shape, lo, hi):
    lane = lax.broadcasted_iota(jnp.int32, shape, 1)
    return jnp.logical_and(lane >= lo, lane < hi)


def _keep_lanes(x, lo, hi):
    return jnp.where(_lane_mask(x.shape, lo, hi), x, jnp.zeros_like(x))


def _keep_rows(x, lo, hi):
    row = lax.broadcasted_iota(jnp.int32, x.shape, 0)
    return jnp.where(jnp.logical_and(row >= lo, row < hi), x, jnp.zeros_like(x))


def _sum_all(xs):
    return functools.reduce(lambda a, b: a + b, xs)


def _softmax_pv(qm, groups):
    s = []
    for kt, _, bias in groups:
        x = _dot(qm, kt)
        s.append(x if bias is None else x + bias)
    m = functools.reduce(jnp.maximum, [jnp.max(x, axis=-1, keepdims=True) for x in s])
    p = [jnp.exp2(x - m) for x in s]
    l = _sum_all([jnp.sum(x, axis=-1, keepdims=True) for x in p])
    return _sum_all([_dot(x.astype(BF16), g[1]) for x, g in zip(p, groups)]) / l


def _softmax_vt(qt, groups, shift):
    s = [_dot(k, qt) for k, _ in groups]
    if shift is None:
        shift = functools.reduce(jnp.maximum, [jnp.max(x, axis=0, keepdims=True) for x in s])
    p = [jnp.exp2(x - shift) for x in s]
    l = _sum_all([jnp.sum(x, axis=0, keepdims=True) for x in p])
    return _sum_all([_dot(g[1], x.astype(BF16)) for x, g in zip(p, groups)]) / l


def _mla_heads_t(qt_fn, groups_fn, shift):
    outs = []
    for h in range(4):
        c0 = 256 * (h // 2)
        lo = LANES * (h % 2)
        outs.append(_softmax_vt(_keep_rows(qt_fn(c0), lo, lo + LANES), groups_fn(h, c0), shift))
    return jnp.concatenate(outs, axis=0)


def _diff_heads_t(qt, groups_fn, lam, gsub_t, lam_init, shift):
    outs = []
    for h in range(4):
        maps = [_softmax_vt(_keep_rows(qt, 64 * h + DIFF_DIM * m, 64 * h + DIFF_DIM * (m + 1)), groups_fn(h), shift)
                for m in range(2)]
        oh = maps[0] - lam * maps[1]
        ms = jnp.mean(oh * oh, axis=0, keepdims=True)
        outs.append(oh * lax.rsqrt(ms + EPS) * gsub_t[64 * h:64 * h + 64, :])
    return jnp.concatenate(outs, axis=0) * (1.0 - lam_init)


def _gqa_heads_t(qt, groups_fn, shift):
    outs = []
    for g in range(2):
        qc = qt[LANES * g:LANES * (g + 1), :]
        for n in range(2):
            outs.append(_softmax_vt(_keep_rows(qc, 64 * n, 64 * n + 64), groups_fn(n), shift))
    return jnp.concatenate(outs, axis=0)


def _plain_heads(q, groups_fn):
    acc = jnp.zeros((q.shape[0], 256), F32)
    for h in range(4):
        pv = _softmax_pv(_keep_lanes(q, 64 * h, 64 * h + 64), groups_fn(h))
        acc = jnp.where(_lane_mask(acc.shape, 64 * h, 64 * h + 64), pv, acc)
    return acc


def _diff_lambda(lam_ref, lam_init):
    lq1, lk1, lq2, lk2 = (lam_ref[i:i + 1, :] for i in range(4))
    return (jnp.exp(jnp.sum(lq1 * lk1, axis=-1, keepdims=True))
            - jnp.exp(jnp.sum(lq2 * lk2, axis=-1, keepdims=True)) + lam_init)


def _latent_attention(shift_ref, chunk_fn):
    bound = shift_ref[0]

    def run(shift):
        def step(c, carry):
            chunk_fn(pl.ds(pl.multiple_of(c * QC, QC), QC), shift)
            return carry

        lax.fori_loop(0, TQ // QC, step, 0)

    @pl.when(bound <= MAX_FIXED_SHIFT)
    def _():
        run(bound)

    @pl.when(jnp.logical_not(bound <= MAX_FIXED_SHIFT))
    def _():
        run(None)


def _mla_attn_kernel(shift_ref, qt_ref, kl_ref, vtl_ref, kc_ref, vtc_ref, o_ref):
    def chunk(cols, shift):
        def groups_fn(h, c0):
            dims = slice(64 * h, 64 * h + 64)
            return [(kl_ref[:, c0:c0 + 256], vtl_ref[0, dims, :]), (kc_ref[:, c0:c0 + 256], vtc_ref[0, dims, :])]

        ot = _mla_heads_t(lambda c0: qt_ref[0, c0:c0 + 256, cols], groups_fn, shift)
        o_ref[cols, :] = ot.T.astype(BF16)

    _latent_attention(shift_ref, chunk)


def _diff_attn_kernel(shift_ref, qt_ref, kl_ref, vtl_ref, kc_ref, vtc_ref, lam_ref, gsubt_ref, o_ref, *, lam_init):
    def chunk(cols, shift):
        def groups_fn(h):
            dims = slice(64 * h, 64 * h + 64)
            return [(kl_ref[...], vtl_ref[0, dims, :]), (kc_ref[...], vtc_ref[0, dims, :])]

        lam = _diff_lambda(lam_ref, lam_init)
        ot = _diff_heads_t(qt_ref[0, :, cols], groups_fn, lam, gsubt_ref[...], lam_init, shift)
        o_ref[cols, :] = ot.T.astype(BF16)

    _latent_attention(shift_ref, chunk)


def _gqa_attn_kernel(shift_ref, qt_ref, kl_ref, vtl_ref, kc_ref, vtc_ref, o_ref):
    def chunk(cols, shift):
        def groups_fn(n):
            dims = slice(64 * n, 64 * n + 64)
            return [(kl_ref[...], vtl_ref[0, dims, :]), (kc_ref[...], vtc_ref[0, dims, :])]

        ot = _gqa_heads_t(qt_ref[0, :, cols], groups_fn, shift)
        o_ref[cols, :] = ot.T.astype(BF16)

    _latent_attention(shift_ref, chunk)


def _latent_attn_call(kernel, name, shift, qt, kl, vtl, kc, vtc, extra, n_batch):
    wq, wk, wv = qt.shape[1], kl.shape[1], vtl.shape[1]
    tiles = SEQ // TQ
    return pl.pallas_call(
        kernel,
        grid=(n_batch, tiles),
        in_specs=[pl.BlockSpec(memory_space=pltpu.SMEM),
                  pl.BlockSpec((1, wq, TQ), lambda b, t: (b, 0, t)),
                  pl.BlockSpec((SEQ, wk), lambda b, t: (b, 0)),
                  pl.BlockSpec((1, wv, SEQ), lambda b, t: (b, 0, 0)),
                  pl.BlockSpec((CTX, wk), lambda b, t: (b, 0)),
                  pl.BlockSpec((1, wv, CTX), lambda b, t: (b, 0, 0))] + [_const_spec(a) for a in extra],
        out_specs=pl.BlockSpec((TQ, 256), lambda b, t: (b * tiles + t, 0)),
        out_shape=jax.ShapeDtypeStruct((n_batch * SEQ, 256), BF16),
        compiler_params=_cparams(2),
        name=name,
    )(shift, qt, kl, vtl, kc, vtc, *extra)


def _na_window(t):
    return jnp.clip(t - 1, 0, SEQ // NA_TQ - NA_WIN)


def _na_build_bias(t, pair_ref, bias_ref):
    rows_per_tile = NA_TQ // GRID_W
    k_row0 = rows_per_tile * _na_window(t)
    lane = lax.broadcasted_iota(jnp.int32, (GRID_W, LANES), 1)
    for ql in range(rows_per_tile):
        qr = rows_per_tile * t + ql
        r0 = jnp.clip(qr - NA_ROWS // 2, 0, N_ROWS - NA_ROWS)
        for j in range(NA_WIN * NA_TQ // LANES):
            kr = k_row0 + 2 * j
            ok0 = jnp.logical_and(kr >= r0, kr < r0 + NA_ROWS).astype(jnp.int32)
            ok1 = jnp.logical_and(kr + 1 >= r0, kr + 1 < r0 + NA_ROWS).astype(jnp.int32)
            ok = jnp.where(lane < GRID_W, ok0, ok1) > 0
            d = jnp.clip(kr - qr + NA_ROWS - 1, -1, 2 * NA_ROWS - 2) + 1
            for h in range(4):
                blk = jnp.where(ok, pair_ref[h, d], MASK_VALUE)
                bias_ref[h, GRID_W * ql:GRID_W * (ql + 1), LANES * j:LANES * (j + 1)] = blk


def _na_attn_kernel(q_ref, k0_ref, k1_ref, k2_ref, kc_ref, v0_ref, v1_ref, v2_ref, vc_ref, pair_ref, o_ref, bias_ref):
    t = pl.program_id(0)

    @pl.when(pl.program_id(1) == 0)
    def _():
        _na_build_bias(t, pair_ref, bias_ref)

    win = ((k0_ref, v0_ref), (k1_ref, v1_ref), (k2_ref, v2_ref))
    groups_fn = lambda h: ([(kt[0], v[...], bias_ref[h, :, NA_TQ * j:NA_TQ * (j + 1)]) for j, (kt, v) in enumerate(win)]
                           + [(kc_ref[0], vc_ref[...], None)])
    o_ref[...] = _plain_heads(q_ref[...], groups_fn).astype(BF16)


def _na_attn_call(q, ktl, vl, ktc, vc, pair, n_batch):
    tiles = SEQ // NA_TQ
    kt_spec = lambda j: pl.BlockSpec((1, 256, NA_TQ), lambda t, b: (b, 0, _na_window(t) + j))
    v_spec = lambda j: pl.BlockSpec((NA_TQ, 256), lambda t, b: (b * tiles + _na_window(t) + j, 0))
    return pl.pallas_call(
        _na_attn_kernel,
        grid=(tiles, n_batch),
        in_specs=[pl.BlockSpec((NA_TQ, 256), lambda t, b: (b * tiles + t, 0)),
                  kt_spec(0), kt_spec(1), kt_spec(2),
                  pl.BlockSpec((1, 256, CTX), lambda t, b: (b, 0, 0)),
                  v_spec(0), v_spec(1), v_spec(2),
                  pl.BlockSpec((CTX, 256), lambda t, b: (b, 0)),
                  _const_spec(pair)],
        out_specs=pl.BlockSpec((NA_TQ, 256), lambda t, b: (b * tiles + t, 0)),
        out_shape=jax.ShapeDtypeStruct((n_batch * SEQ, 256), BF16),
        scratch_shapes=[pltpu.VMEM((4, NA_TQ, NA_WIN * NA_TQ), F32)],
        compiler_params=_cparams(2),
        name="na_attn",
    )(q, ktl, ktl, ktl, ktc, vl, vl, vl, vc, pair)


def _ctx_attn_kernel(mqt_ref, mk_ref, mvt_ref, nq_ref, nkt_ref, nv_ref, dqt_ref, dk_ref, dvt_ref,
                     gqt_ref, gk_ref, gvt_ref, lam_ref, gsubt_ref, oa_ref, ob_ref, oc_ref, od_ref, *, lam_init):
    head_dims = lambda h: slice(64 * h, 64 * h + 64)
    oa_ref[...] = _mla_heads_t(lambda c0: mqt_ref[0, c0:c0 + 256, :],
                               lambda h, c0: [(mk_ref[:, c0:c0 + 256], mvt_ref[0, head_dims(h), :])],
                               None).T.astype(BF16)
    ob_ref[...] = _plain_heads(nq_ref[...], lambda h: [(nkt_ref[0], nv_ref[...], None)]).astype(BF16)
    lam = _diff_lambda(lam_ref, lam_init)
    oc_ref[...] = _diff_heads_t(dqt_ref[0], lambda h: [(dk_ref[...], dvt_ref[0, head_dims(h), :])], lam,
                                gsubt_ref[...], lam_init, None).T.astype(BF16)
    od_ref[...] = _gqa_heads_t(gqt_ref[0], lambda n: [(gk_ref[...], gvt_ref[0, head_dims(n), :])], None).T.astype(BF16)


def _ctx_attn_call(qkv_c, lam, gsub, lam_init, n_batch):
    specs = []
    for a in qkv_c:
        if a.ndim == 3:
            specs.append(pl.BlockSpec((1, a.shape[1], CTX), lambda b: (b, 0, 0)))
        else:
            specs.append(pl.BlockSpec((CTX, a.shape[1]), lambda b: (b, 0)))
    out = jax.ShapeDtypeStruct((n_batch * CTX, 256), BF16)
    return pl.pallas_call(
        functools.partial(_ctx_attn_kernel, lam_init=lam_init),
        grid=(n_batch,),
        in_specs=specs + [_const_spec(lam), _const_spec(gsub)],
        out_specs=[pl.BlockSpec((CTX, 256), lambda b: (b, 0))] * 4,
        out_shape=[out] * 4,
        compiler_params=_cparams(1),
        name="ctx_attn",
    )(*qkv_c, lam, gsub)


def _merge_kernel(x_ref, mod_ref, g_ref, oa_ref, ob_ref, oc_ref, od_ref, wg_ref, wb_ref, wo_ref, out_ref):
    x = x_ref[...]
    mod = mod_ref[0]
    xn = _modnorm(x, g_ref[...], mod[:, 0:D], mod[:, D:2 * D]).astype(BF16)
    acc = jnp.zeros(x.shape, F32)
    for n, o_ref in enumerate((oa_ref, ob_ref, oc_ref, od_ref)):
        gate = jax.nn.sigmoid(_dot(xn, wg_ref[:, n * D:(n + 1) * D]))
        acc = acc + gate * _dot(o_ref[...], wb_ref[n])
    y = _dot(acc.astype(BF16), wo_ref[...])
    out_ref[...] = x + mod[:, 2 * D:3 * D] * y


def _merge_call(rows, mod, mod_row, g1, branches, w_gate, w_branch, w_out, tm):
    n_rows = rows.shape[0]
    tm = min(tm, n_rows)
    tok = lambda w: pl.BlockSpec((tm, w), lambda i: (i, 0))
    return pl.pallas_call(
        _merge_kernel,
        grid=(n_rows // tm,),
        in_specs=[tok(D), pl.BlockSpec((1, 1, 6 * D), lambda i: (mod_row(i), 0, 0)), _const_spec(g1)]
                 + [tok(256)] * 4 + [_const_spec(w_gate), _const_spec(w_branch), _const_spec(w_out)],
        out_specs=tok(D),
        out_shape=jax.ShapeDtypeStruct((n_rows, D), F32),
        compiler_params=_cparams(1),
        name="merge",
    )(rows, mod, g1, *branches, w_gate, w_branch, w_out)


def _mlp_kernel(x_ref, mod_ref, g_ref, wu_ref, wd_ref, out_ref):
    x = x_ref[...]
    mod = mod_ref[0]
    xn = _modnorm(x, g_ref[...], mod[:, 3 * D:4 * D], mod[:, 4 * D:5 * D]).astype(BF16)
    h = jnp.square(jnp.maximum(_dot(xn, wu_ref[...]), 0.0))
    out_ref[...] = x + mod[:, 5 * D:6 * D] * _dot(h.astype(BF16), wd_ref[...])


def _mlp_call(rows, mod, mod_row, g2, w_up, w_down, tm):
    n_rows = rows.shape[0]
    tm = min(tm, n_rows)
    tok = lambda w: pl.BlockSpec((tm, w), lambda i: (i, 0))
    return pl.pallas_call(
        _mlp_kernel,
        grid=(n_rows // tm,),
        in_specs=[tok(D), pl.BlockSpec((1, 1, 6 * D), lambda i: (mod_row(i), 0, 0)), _const_spec(g2),
                  _const_spec(w_up), _const_spec(w_down)],
        out_specs=tok(D),
        out_shape=jax.ShapeDtypeStruct((n_rows, D), F32),
        compiler_params=_cparams(1),
        name="mlp",
    )(rows, mod, g2, w_up, w_down)


def _block_diag_ones(n, seg):
    i = jnp.arange(n) // seg
    return (i[:, None] == i[None, :]).astype(BF16)


def _rope_parts(rot_dim):
    t = jnp.arange(SEQ)
    rows, cols = t // GRID_W, t % GRID_W
    n = rot_dim // 4
    inv_freq = jnp.power(ROPE_THETA, -jnp.arange(n, dtype=F32) / n)
    ang_r = rows.astype(F32)[:, None] * inv_freq
    ang_c = cols.astype(F32)[:, None] * inv_freq
    ang = jnp.concatenate([ang_r, ang_r, ang_c, ang_c], axis=-1)
    cos, sin = jnp.cos(ang), jnp.sin(ang)
    even = (jnp.arange(rot_dim) // n) % 2 == 0
    return cos, jnp.where(even, -sin, 0.0), jnp.where(even, 0.0, sin)


def _rope_tables():
    r32, r64 = _rope_parts(MLA_ROPE), _rope_parts(HEAD_DIM)
    ones = jnp.ones((SEQ, MLA_NOPE), F32)
    zeros_n = jnp.zeros((SEQ, MLA_NOPE), F32)
    pad = jnp.zeros((SEQ, LANES - MLA_QK), F32)
    mla = [jnp.concatenate([lead, part, pad], axis=1) for lead, part in zip((ones, zeros_n, zeros_n), r32)]
    return jnp.stack(mla + [jnp.tile(a, (1, LANES // MLA_ROPE)) for a in r32]
                     + [jnp.tile(a, (1, LANES // HEAD_DIM)) for a in r64])


def _na_pair_table(rpb):
    qc = jnp.arange(GRID_W)[:, None]
    kc = jnp.arange(GRID_W)[None, :]
    onehot = (kc - qc + NA_COLS - 1)[None] == jnp.arange(2 * NA_COLS - 1)[:, None, None]
    toeplitz = jnp.einsum("hrd,dqk->hrqk", rpb, onehot.astype(F32), precision=lax.Precision.HIGHEST)
    c0 = jnp.clip(qc - NA_COLS // 2, 0, GRID_W - NA_COLS)
    in_window = (kc >= c0) & (kc < c0 + NA_COLS)
    masked = jnp.where(in_window, toeplitz * LOG2E, MASK_VALUE)
    ext = jnp.pad(masked, ((0, 0), (1, 1), (0, 0), (0, 0)), constant_values=MASK_VALUE)
    return jnp.concatenate([ext[:, :-1], ext[:, 1:]], axis=-1)


def _layer_params(l, p):
    w_in = p["w_in"][l]
    zcol = lambda n: jnp.zeros((D, n), F32)
    kr = w_in[:, C_KR:C_KR + MLA_ROPE]
    kr_slots = jnp.concatenate([jnp.concatenate([zcol(MLA_NOPE), kr, zcol(LANES - MLA_QK)], axis=1)] * 4, axis=1)
    gq_cols = w_in[:, C_GQ:C_GQ + 256].reshape(D, 4, HEAD_DIM)[:, jnp.array([0, 2, 1, 3])].reshape(D, 256)
    w_small = jnp.concatenate([
        w_in[:, C_CQ:C_CQ + MLA_Q_LORA], zcol(256 - MLA_Q_LORA),
        w_in[:, C_CKV:C_CKV + MLA_KV_LORA],
        kr_slots,
        w_in[:, C_NA:C_NA + 768],
        w_in[:, C_DF:C_DF + 768],
        gq_cols, w_in[:, C_GQ + 256:C_GQ + 512]], axis=1).astype(BF16)

    w_uq = p["w_mla_uq"][l].reshape(MLA_Q_LORA, 4, MLA_QK)
    w_uq = jnp.pad(w_uq, ((0, 256 - MLA_Q_LORA), (0, 0), (0, LANES - MLA_QK))).reshape(256, 512).astype(BF16)
    w_ukv = p["w_mla_ukv"][l].reshape(MLA_KV_LORA, 4, 2 * MLA_NOPE)
    w_uk = jnp.pad(w_ukv[:, :, :MLA_NOPE], ((0, 0), (0, 0), (0, LANES - MLA_NOPE))).reshape(MLA_KV_LORA, 512).astype(BF16)
    w_uv = w_ukv[:, :, MLA_NOPE:].reshape(MLA_KV_LORA, 256).astype(BF16)

    mla_gain = lambda g: jnp.tile(jnp.pad(g, (0, LANES - MLA_QK)), 4)
    row = lambda g: jnp.pad(g, (0, 512 - g.shape[0]))
    hd_scale = HEAD_DIM ** -0.5 * LOG2E
    gains = jnp.stack([
        row(mla_gain(p["g_mla_q"][l]) * (MLA_QK ** -0.5 * LOG2E)), row(mla_gain(p["g_mla_k"][l])),
        row(jnp.tile(p["g_na_q"][l], 4) * hd_scale), row(jnp.tile(p["g_na_k"][l], 4)),
        row(jnp.tile(p["g_diff_q"][l], 8) * (DIFF_DIM ** -0.5 * LOG2E)), row(jnp.tile(p["g_diff_k"][l], 8)),
        row(jnp.tile(p["g_gqa_q"][l], 4) * hd_scale), row(jnp.tile(p["g_gqa_k"][l], 2))])

    def score_bound(dim, g_q, g_k):
        return (LOG2E * dim ** 0.5 * jnp.max(jnp.abs(g_q)) * jnp.max(jnp.abs(g_k))).reshape(1).astype(F32)

    w_branch = p["w_branch"][l]
    wb_gqa = w_branch[3].reshape(4, HEAD_DIM, D)[jnp.array([0, 2, 1, 3])].reshape(256, D)
    w_branch = jnp.concatenate([w_branch[:3], wb_gqa[None]], axis=0).astype(BF16)

    return dict(
        g1=p["g_norm1"][l][None], g2=p["g_norm2"][l][None],
        w_small=w_small, w_gate=w_in[:, C_GATE:].astype(BF16),
        g_qa=jnp.pad(p["g_mla_qa"][l], (0, 256 - MLA_Q_LORA))[None], g_kva=p["g_mla_kva"][l][None],
        w_uq=w_uq, w_uk=w_uk, w_uv=w_uv, gains=gains,
        mla_bound=score_bound(MLA_QK, p["g_mla_q"][l], p["g_mla_k"][l]),
        diff_bound=score_bound(DIFF_DIM, p["g_diff_q"][l], p["g_diff_k"][l]),
        gqa_bound=score_bound(HEAD_DIM, p["g_gqa_q"][l], p["g_gqa_k"][l]),
        na_pair=_na_pair_table(p["na_rpb"][l]),
        lam=jnp.stack([p["diff_lq1"][l], p["diff_lk1"][l], p["diff_lq2"][l], p["diff_lk2"][l]]),
        g_sub=jnp.tile(p["g_diff_sub"][l], 4)[:, None],
        w_branch=w_branch, w_out=p["w_out"][l].astype(BF16),
        w_up=p["w_up"][l].astype(BF16), w_down=p["w_down"][l].astype(BF16),
    )


def kernel(x, c, ctx, c_ctx, w_ada, b_ada, g_norm1, g_norm2, w_in, g_mla_qa, w_mla_uq, g_mla_kva, w_mla_ukv, g_mla_q, g_mla_k, g_na_q, g_na_k, na_rpb, g_diff_q, g_diff_k, diff_lq1, diff_lk1, diff_lq2, diff_lk2, g_diff_sub, g_gqa_q, g_gqa_k, w_branch, w_out, w_up, w_down):
    p = dict(w_in=w_in, g_norm1=g_norm1, g_norm2=g_norm2, g_mla_qa=g_mla_qa, w_mla_uq=w_mla_uq,
             g_mla_kva=g_mla_kva, w_mla_ukv=w_mla_ukv, g_mla_q=g_mla_q, g_mla_k=g_mla_k,
             g_na_q=g_na_q, g_na_k=g_na_k, na_rpb=na_rpb, g_diff_q=g_diff_q, g_diff_k=g_diff_k,
             diff_lq1=diff_lq1, diff_lk1=diff_lk1, diff_lq2=diff_lq2, diff_lk2=diff_lk2,
             g_diff_sub=g_diff_sub, g_gqa_q=g_gqa_q, g_gqa_k=g_gqa_k, w_branch=w_branch,
             w_out=w_out, w_up=w_up, w_down=w_down)
    n_batch = x.shape[0]
    depth = w_ada.shape[0]
    assert x.shape[1:] == (SEQ, D) and ctx.shape[1:] == (CTX, D)

    mod_rows = -(-(n_batch + 1) // 8) * 8
    c_all = jnp.concatenate([c, c_ctx[None], jnp.zeros((mod_rows - n_batch - 1, D), F32)], axis=0)
    mods = _ada_call(c_all, w_ada, b_ada).reshape(depth, mod_rows, 1, 6 * D)
    lat_tiles = SEQ // TM
    lat_row = lambda i: i // lat_tiles
    ctx_row = lambda i: n_batch

    consts = dict(bd128=_block_diag_ones(256, 128), bd64=_block_diag_ones(256, 64),
                  bd32=_block_diag_ones(256, 32), bd64h=_block_diag_ones(128, 64))
    tabs = _rope_tables()

    xl = x.reshape(n_batch * SEQ, D)
    xc = ctx.reshape(n_batch * CTX, D)
    for l in range(depth):
        need_ctx = l < depth - 1
        lam_init = 0.8 - 0.6 * math.exp(-0.3 * l)
        lw = _layer_params(l, p)
        mod = mods[l]

        lat = _qkv_call(xl, mod, lambda b: b, lw, consts, tabs, n_batch, SEQ, TM)
        cx = _qkv_call(xc, mod, ctx_row, lw, consts, None, n_batch, CTX, CTX)
        mqt, mk, mvt, nq, nkt, nv, dqt, dk, dvt, gqt, gk, gvt = lat
        oa = _latent_attn_call(_mla_attn_kernel, "mla_attn", lw["mla_bound"], mqt, mk, mvt, cx[1], cx[2], [], n_batch)
        ob = _na_attn_call(nq, nkt, nv, cx[4], cx[5], lw["na_pair"], n_batch)
        oc = _latent_attn_call(functools.partial(_diff_attn_kernel, lam_init=lam_init), "diff_attn",
                               lw["diff_bound"], dqt, dk, dvt, cx[7], cx[8], [lw["lam"], lw["g_sub"]], n_batch)
        od = _latent_attn_call(_gqa_attn_kernel, "gqa_attn", lw["gqa_bound"], gqt, gk, gvt, cx[10], cx[11], [],
                               n_batch)
        if need_ctx:
            oc_all = _ctx_attn_call(cx, lw["lam"], lw["g_sub"], lam_init, n_batch)
            xc = _merge_call(xc, mod, ctx_row, lw["g1"], oc_all, lw["w_gate"], lw["w_branch"], lw["w_out"], TM)
            xc = _mlp_call(xc, mod, ctx_row, lw["g2"], lw["w_up"], lw["w_down"], TM)
        xl = _merge_call(xl, mod, lat_row, lw["g1"], (oa, ob, oc, od), lw["w_gate"], lw["w_branch"], lw["w_out"], TM)
        xl = _mlp_call(xl, mod, lat_row, lw["g2"], lw["w_up"], lw["w_down"], TM)
    return xl.reshape(n_batch, SEQ, D)
```

```python
import functools
import math

import jax
import jax.numpy as jnp
from jax import lax
from jax.experimental import pallas as pl
from jax.experimental.pallas import tpu as pltpu

F32 = jnp.float32
BF16 = jnp.bfloat16

D = 1024
SEQ = 2048
CTX = 256
GRID_W = 64
N_ROWS = SEQ // GRID_W
ROPE_THETA = 10000.0
EPS = 1e-6
HEAD_DIM = 64
MLA_Q_LORA = 192
MLA_KV_LORA = 128
MLA_NOPE = 64
MLA_ROPE = 32
MLA_QK = MLA_NOPE + MLA_ROPE
NA_ROWS = 8
NA_COLS = 16
DIFF_DIM = 32
D_FF = 4 * D

LANES = 128
TM = 512
TQ = 2048
QC = 1024
NA_TQ = 256
NA_WIN = 3
MASK_VALUE = -1e30
LOG2E = math.log2(math.e)
MAX_FIXED_SHIFT = 50.0

C_CQ, C_CKV, C_KR, C_NA, C_DF, C_GQ, C_GATE = 0, 192, 320, 352, 1120, 1888, 2400
Z_CQ, Z_CKV, Z_KR, Z_NA, Z_DF, Z_GQ, Z_W = 0, 256, 384, 896, 1664, 2432, 2944
G_MQ, G_MK, G_NQ, G_NK, G_DQ, G_DK, G_GQ, G_GK = range(8)
T_MLA, T_R32, T_R64 = 0, 3, 6

VMEM_LIMIT = 56 * 1024 * 1024


def _cparams(n_axes):
    return pltpu.CompilerParams(dimension_semantics=("arbitrary",) * n_axes,
                                vmem_limit_bytes=VMEM_LIMIT)


def _dot(a, b):
    return jnp.dot(a, b, preferred_element_type=F32)


def _const_spec(a):
    return pl.BlockSpec(a.shape, lambda *_: (0,) * a.ndim, pipeline_mode=pl.Buffered(1))


def _modnorm(x, g, shift, scale):
    y = x * lax.rsqrt(jnp.mean(x * x, axis=-1, keepdims=True) + EPS) * g
    return y * (1.0 + scale) + shift


def _ada_kernel(c_ref, w_ref, b_ref, o_ref):
    c = c_ref[...]
    s = c * jax.nn.sigmoid(c)
    o_ref[0] = _dot(s.astype(BF16), w_ref[0].astype(BF16)) + b_ref[0]


def _ada_call(c_all, w_ada, b_ada):
    n_layers = w_ada.shape[0]
    rows = c_all.shape[0]
    bn = 1536
    return pl.pallas_call(
        _ada_kernel,
        grid=(n_layers, 6 * D // bn),
        in_specs=[pl.BlockSpec((rows, D), lambda l, j: (0, 0)),
                  pl.BlockSpec((1, D, bn), lambda l, j: (l, 0, j)),
                  pl.BlockSpec((1, 1, bn), lambda l, j: (l, 0, j))],
        out_specs=pl.BlockSpec((1, rows, bn), lambda l, j: (l, 0, j)),
        out_shape=jax.ShapeDtypeStruct((n_layers, rows, 6 * D), F32),
        compiler_params=_cparams(2),
        name="ada",
    )(c_all, w_ada, b_ada.reshape(n_layers, 1, 6 * D))


def _seg_mean_sq(t, bd, inv_d):
    x2 = t * t
    hi = x2.astype(BF16)
    lo = (x2 - hi.astype(F32)).astype(BF16)
    return (_dot(hi, bd) + _dot(lo, bd)) * inv_d


def _qkv_kernel(*refs, rope):
    (x_ref, mod_ref, g1_ref, w_ref, bd128_ref, bd64_ref, bd32_ref, bd64h_ref,
     gqa_ref, gkva_ref, wuq_ref, wuk_ref, wuv_ref, gains_ref) = refs[:14]
    tab_ref = refs[14] if rope else None
    (mqt_ref, mk_ref, mvt_ref, nqt_ref, nk_ref, nvt_ref,
     dqt_ref, dk_ref, dvt_ref, gqt_ref, gk_ref, gvt_ref) = refs[-12:]

    mod = mod_ref[0]
    xn = _modnorm(x_ref[...], g1_ref[...], mod[:, 0:D], mod[:, D:2 * D]).astype(BF16)

    def cols(lo, hi):
        return _dot(xn, w_ref[:, lo:hi])

    def put(o_ref, lo, val):
        o_ref[:, lo:lo + LANES] = val.astype(BF16)

    def put_t(o_ref, lo, val):
        o_ref[0, lo:lo + LANES, :] = val.T.astype(BF16)

    def section(t, bd, inv_d, gain_row, tab, shift, o_ref, store):
        width = t.shape[1]
        y = t * lax.rsqrt(_seg_mean_sq(t, bd, inv_d) + EPS) * gains_ref[gain_row:gain_row + 1, 0:width]
        for j in range(width // LANES):
            yc = y[:, j * LANES:(j + 1) * LANES]
            if rope and tab is not None:
                yc = (yc * tab_ref[tab] + pltpu.roll(yc, LANES - shift, 1) * tab_ref[tab + 1]
                      + pltpu.roll(yc, shift, 1) * tab_ref[tab + 2])
            store(o_ref, j * LANES, yc)

    za = cols(Z_CQ, Z_NA)
    cq = za[:, Z_CQ:Z_CQ + 256]
    cqn = cq * lax.rsqrt(jnp.sum(cq * cq, axis=-1, keepdims=True) * (1.0 / MLA_Q_LORA) + EPS) * gqa_ref[...]
    q = _dot(cqn.astype(BF16), wuq_ref[...])
    ckv = za[:, Z_CKV:Z_CKV + 128]
    kvn = (ckv * lax.rsqrt(jnp.mean(ckv * ckv, axis=-1, keepdims=True) + EPS) * gkva_ref[...]).astype(BF16)
    k = _dot(kvn, wuk_ref[...]) + za[:, Z_KR:Z_KR + 512]
    mvt_ref[0] = _dot(kvn, wuv_ref[...]).T.astype(BF16)
    bd128 = bd128_ref[...]
    for c in range(2):
        sl = slice(256 * c, 256 * c + 256)
        section(q[:, sl], bd128, 1.0 / MLA_QK, G_MQ, T_MLA, MLA_ROPE // 4, mqt_ref.at[:, sl, :], put_t)
        section(k[:, sl], bd128, 1.0 / MLA_QK, G_MK, T_MLA, MLA_ROPE // 4, mk_ref.at[:, sl], put)

    zn = cols(Z_NA, Z_DF)
    bd64 = bd64_ref[...]
    section(zn[:, 0:256], bd64, 1.0 / HEAD_DIM, G_NQ, None, 0, nqt_ref, put_t)
    section(zn[:, 256:512], bd64, 1.0 / HEAD_DIM, G_NK, None, 0, nk_ref, put)
    nvt_ref[0] = zn[:, 512:768].T.astype(BF16)

    zd = cols(Z_DF, Z_GQ)
    bd32 = bd32_ref[...]
    section(zd[:, 0:256], bd32, 1.0 / DIFF_DIM, G_DQ, T_R32, DIFF_DIM // 4, dqt_ref, put_t)
    section(zd[:, 256:512], bd32, 1.0 / DIFF_DIM, G_DK, T_R32, DIFF_DIM // 4, dk_ref, put)
    dvt_ref[0] = zd[:, 512:768].T.astype(BF16)

    zg = cols(Z_GQ, Z_W)
    section(zg[:, 0:256], bd64, 1.0 / HEAD_DIM, G_GQ, T_R64, HEAD_DIM // 4, gqt_ref, put_t)
    section(zg[:, 256:384], bd64h_ref[...], 1.0 / HEAD_DIM, G_GK, T_R64, HEAD_DIM // 4, gk_ref, put)
    gvt_ref[0] = zg[:, 384:512].T.astype(BF16)


def _qkv_call(rows, mod, mod_row, lw, consts, tabs, n_batch, seq, tm):
    tiles = seq // tm
    row_map = lambda t, b: (b * tiles + t, 0)
    tok_spec = lambda w: pl.BlockSpec((tm, w), row_map)
    tr_spec = lambda w: pl.BlockSpec((1, w, tm), lambda t, b: (b, 0, t))
    tok_shape = lambda w: jax.ShapeDtypeStruct((n_batch * seq, w), BF16)
    tr_shape = lambda w: jax.ShapeDtypeStruct((n_batch, w, seq), BF16)
    small = [lw["g1"], lw["w_small"], consts["bd128"], consts["bd64"], consts["bd32"], consts["bd64h"],
             lw["g_qa"], lw["g_kva"], lw["w_uq"], lw["w_uk"], lw["w_uv"], lw["gains"]]
    in_specs = [tok_spec(D), pl.BlockSpec((1, 1, 6 * D), lambda t, b: (mod_row(b), 0, 0))]
    in_specs += [_const_spec(a) for a in small]
    args = [rows, mod] + small
    if tabs is not None:
        in_specs.append(pl.BlockSpec((tabs.shape[0], tm, LANES), lambda t, b: (0, t, 0)))
        args.append(tabs)
    return pl.pallas_call(
        functools.partial(_qkv_kernel, rope=tabs is not None),
        grid=(tiles, n_batch),
        in_specs=in_specs,
        out_specs=[tr_spec(512), tok_spec(512), tr_spec(256),
                   tr_spec(256), tok_spec(256), tr_spec(256),
                   tr_spec(256), tok_spec(256), tr_spec(256),
                   tr_spec(256), tok_spec(128), tr_spec(128)],
        out_shape=[tr_shape(512), tok_shape(512), tr_shape(256),
                   tr_shape(256), tok_shape(256), tr_shape(256),
                   tr_shape(256), tok_shape(256), tr_shape(256),
                   tr_shape(256), tok_shape(128), tr_shape(128)],
        compiler_params=_cparams(2),
        name="qkv",
    )(*args)


def _keep_rows(x, lo, hi):
    row = lax.broadcasted_iota(jnp.int32, x.shape, 0)
    return jnp.where(jnp.logical_and(row >= lo, row < hi), x, jnp.zeros_like(x))


def _sum_all(xs):
    return functools.reduce(lambda a, b: a + b, xs)


def _softmax_vt(qt, groups, shift):
    s = [_dot(k, qt) if bias is None else _dot(k, qt) + bias for k, _, bias in groups]
    if shift is None:
        shift = functools.reduce(jnp.maximum, [jnp.max(x, axis=0, keepdims=True) for x in s])
    p = [jnp.exp2(x - shift) for x in s]
    l = _sum_all([jnp.sum(x, axis=0, keepdims=True) for x in p])
    return _sum_all([_dot(g[1], x.astype(BF16)) for x, g in zip(p, groups)]) / l


def _mla_heads_t(qt_fn, groups_fn, shift):
    outs = []
    for h in range(4):
        c0 = 256 * (h // 2)
        lo = LANES * (h % 2)
        outs.append(_softmax_vt(_keep_rows(qt_fn(c0), lo, lo + LANES), groups_fn(h, c0), shift))
    return jnp.concatenate(outs, axis=0)


def _diff_heads_t(qt, groups_fn, lam, gsub_t, lam_init, shift):
    outs = []
    for h in range(4):
        maps = [_softmax_vt(_keep_rows(qt, 64 * h + DIFF_DIM * m, 64 * h + DIFF_DIM * (m + 1)), groups_fn(h), shift)
                for m in range(2)]
        oh = maps[0] - lam * maps[1]
        ms = jnp.mean(oh * oh, axis=0, keepdims=True)
        outs.append(oh * lax.rsqrt(ms + EPS) * gsub_t[64 * h:64 * h + 64, :])
    return jnp.concatenate(outs, axis=0) * (1.0 - lam_init)


def _gqa_heads_t(qt, groups_fn, shift):
    outs = []
    for g in range(2):
        qc = qt[LANES * g:LANES * (g + 1), :]
        for n in range(2):
            outs.append(_softmax_vt(_keep_rows(qc, 64 * n, 64 * n + 64), groups_fn(n), shift))
    return jnp.concatenate(outs, axis=0)


def _plain_heads_t(qt, groups_fn, shift):
    return jnp.concatenate([_softmax_vt(_keep_rows(qt, 64 * h, 64 * h + 64), groups_fn(h), shift) for h in range(4)],
                           axis=0)


def _diff_lambda(lam_ref, lam_init):
    lq1, lk1, lq2, lk2 = (lam_ref[i:i + 1, :] for i in range(4))
    return (jnp.exp(jnp.sum(lq1 * lk1, axis=-1, keepdims=True))
            - jnp.exp(jnp.sum(lq2 * lk2, axis=-1, keepdims=True)) + lam_init)


def _latent_attention(shift_ref, chunk_fn):
    bound = shift_ref[0]

    def run(shift):
        def step(c, carry):
            chunk_fn(pl.ds(pl.multiple_of(c * QC, QC), QC), shift)
            return carry

        lax.fori_loop(0, TQ // QC, step, 0)

    @pl.when(bound <= MAX_FIXED_SHIFT)
    def _():
        run(bound)

    @pl.when(jnp.logical_not(bound <= MAX_FIXED_SHIFT))
    def _():
        run(None)


def _mla_attn_kernel(shift_ref, qt_ref, kl_ref, vtl_ref, kc_ref, vtc_ref, o_ref):
    def chunk(cols, shift):
        def groups_fn(h, c0):
            dims = slice(64 * h, 64 * h + 64)
            return [(kl_ref[:, c0:c0 + 256], vtl_ref[0, dims, :], None),
                    (kc_ref[:, c0:c0 + 256], vtc_ref[0, dims, :], None)]

        ot = _mla_heads_t(lambda c0: qt_ref[0, c0:c0 + 256, cols], groups_fn, shift)
        o_ref[cols, :] = ot.T.astype(BF16)

    _latent_attention(shift_ref, chunk)


def _diff_attn_kernel(shift_ref, qt_ref, kl_ref, vtl_ref, kc_ref, vtc_ref, lam_ref, gsubt_ref, o_ref, *, lam_init):
    def chunk(cols, shift):
        def groups_fn(h):
            dims = slice(64 * h, 64 * h + 64)
            return [(kl_ref[...], vtl_ref[0, dims, :], None), (kc_ref[...], vtc_ref[0, dims, :], None)]

        lam = _diff_lambda(lam_ref, lam_init)
        ot = _diff_heads_t(qt_ref[0, :, cols], groups_fn, lam, gsubt_ref[...], lam_init, shift)
        o_ref[cols, :] = ot.T.astype(BF16)

    _latent_attention(shift_ref, chunk)


def _gqa_attn_kernel(shift_ref, qt_ref, kl_ref, vtl_ref, kc_ref, vtc_ref, o_ref):
    def chunk(cols, shift):
        def groups_fn(n):
            dims = slice(64 * n, 64 * n + 64)
            return [(kl_ref[...], vtl_ref[0, dims, :], None), (kc_ref[...], vtc_ref[0, dims, :], None)]

        ot = _gqa_heads_t(qt_ref[0, :, cols], groups_fn, shift)
        o_ref[cols, :] = ot.T.astype(BF16)

    _latent_attention(shift_ref, chunk)


def _latent_attn_call(kernel, name, shift, qt, kl, vtl, kc, vtc, extra, n_batch):
    wq, wk, wv = qt.shape[1], kl.shape[1], vtl.shape[1]
    tiles = SEQ // TQ
    return pl.pallas_call(
        kernel,
        grid=(n_batch, tiles),
        in_specs=[pl.BlockSpec(memory_space=pltpu.SMEM),
                  pl.BlockSpec((1, wq, TQ), lambda b, t: (b, 0, t)),
                  pl.BlockSpec((SEQ, wk), lambda b, t: (b, 0)),
                  pl.BlockSpec((1, wv, SEQ), lambda b, t: (b, 0, 0)),
                  pl.BlockSpec((CTX, wk), lambda b, t: (b, 0)),
                  pl.BlockSpec((1, wv, CTX), lambda b, t: (b, 0, 0))] + [_const_spec(a) for a in extra],
        out_specs=pl.BlockSpec((TQ, 256), lambda b, t: (b * tiles + t, 0)),
        out_shape=jax.ShapeDtypeStruct((n_batch * SEQ, 256), BF16),
        compiler_params=_cparams(2),
        name=name,
    )(shift, qt, kl, vtl, kc, vtc, *extra)


def _na_window(t):
    return jnp.clip(t - 1, 0, SEQ // NA_TQ - NA_WIN)


def _na_build_bias(t, pair_ref, bias_ref):
    rows_per_tile = NA_TQ // GRID_W
    k_row0 = rows_per_tile * _na_window(t)
    lane = lax.broadcasted_iota(jnp.int32, (GRID_W, LANES), 1)
    for i in range(rows_per_tile // 2):
        qr = rows_per_tile * t + 2 * i
        r0 = jnp.clip(qr - NA_ROWS // 2, 0, N_ROWS - NA_ROWS)
        r1 = jnp.clip(qr + 1 - NA_ROWS // 2, 0, N_ROWS - NA_ROWS)
        for kl in range(NA_WIN * rows_per_tile):
            kr = k_row0 + kl
            ok0 = jnp.logical_and(kr >= r0, kr < r0 + NA_ROWS).astype(jnp.int32)
            ok1 = jnp.logical_and(kr >= r1, kr < r1 + NA_ROWS).astype(jnp.int32)
            ok = jnp.where(lane < GRID_W, ok0, ok1) > 0
            d = jnp.clip(kr - qr + NA_ROWS - 1, 0, 2 * NA_ROWS - 1)
            for h in range(4):
                blk = jnp.where(ok, pair_ref[h, d], MASK_VALUE)
                bias_ref[h, GRID_W * kl:GRID_W * (kl + 1), LANES * i:LANES * (i + 1)] = blk


def _na_attn_kernel(shift_ref, qt_ref, k0_ref, k1_ref, k2_ref, kc_ref, vt0_ref, vt1_ref, vt2_ref, vtc_ref, pair_ref,
                    o_ref, bias_ref):
    t = pl.program_id(0)

    @pl.when(pl.program_id(1) == 0)
    def _():
        _na_build_bias(t, pair_ref, bias_ref)

    win = ((k0_ref, vt0_ref), (k1_ref, vt1_ref), (k2_ref, vt2_ref))

    def groups_fn(h):
        dims = slice(64 * h, 64 * h + 64)
        return ([(k[...], vt[0, dims, :], bias_ref[h, NA_TQ * j:NA_TQ * (j + 1), :]) for j, (k, vt) in enumerate(win)]
                + [(kc_ref[...], vtc_ref[0, dims, :], None)])

    bound = shift_ref[0]

    @pl.when(bound <= MAX_FIXED_SHIFT)
    def _():
        o_ref[...] = _plain_heads_t(qt_ref[0], groups_fn, bound).T.astype(BF16)

    @pl.when(jnp.logical_not(bound <= MAX_FIXED_SHIFT))
    def _():
        o_ref[...] = _plain_heads_t(qt_ref[0], groups_fn, None).T.astype(BF16)


def _na_attn_call(shift, qt, kl, vtl, kc, vtc, pair, n_batch):
    tiles = SEQ // NA_TQ
    k_spec = lambda j: pl.BlockSpec((NA_TQ, 256), lambda t, b: (b * tiles + _na_window(t) + j, 0))
    vt_spec = lambda j: pl.BlockSpec((1, 256, NA_TQ), lambda t, b: (b, 0, _na_window(t) + j))
    return pl.pallas_call(
        _na_attn_kernel,
        grid=(tiles, n_batch),
        in_specs=[pl.BlockSpec(memory_space=pltpu.SMEM),
                  pl.BlockSpec((1, 256, NA_TQ), lambda t, b: (b, 0, t)),
                  k_spec(0), k_spec(1), k_spec(2),
                  pl.BlockSpec((CTX, 256), lambda t, b: (b, 0)),
                  vt_spec(0), vt_spec(1), vt_spec(2),
                  pl.BlockSpec((1, 256, CTX), lambda t, b: (b, 0, 0)),
                  _const_spec(pair)],
        out_specs=pl.BlockSpec((NA_TQ, 256), lambda t, b: (b * tiles + t, 0)),
        out_shape=jax.ShapeDtypeStruct((n_batch * SEQ, 256), BF16),
        scratch_shapes=[pltpu.VMEM((4, NA_WIN * NA_TQ, NA_TQ), F32)],
        compiler_params=_cparams(2),
        name="na_attn",
    )(shift, qt, kl, kl, kl, kc, vtl, vtl, vtl, vtc, pair)


def _ctx_attn_kernel(mqt_ref, mk_ref, mvt_ref, nqt_ref, nk_ref, nvt_ref, dqt_ref, dk_ref, dvt_ref,
                     gqt_ref, gk_ref, gvt_ref, lam_ref, gsubt_ref, oa_ref, ob_ref, oc_ref, od_ref, *, lam_init):
    head_dims = lambda h: slice(64 * h, 64 * h + 64)
    oa_ref[...] = _mla_heads_t(lambda c0: mqt_ref[0, c0:c0 + 256, :],
                               lambda h, c0: [(mk_ref[:, c0:c0 + 256], mvt_ref[0, head_dims(h), :], None)],
                               None).T.astype(BF16)
    ob_ref[...] = _plain_heads_t(nqt_ref[0], lambda h: [(nk_ref[...], nvt_ref[0, head_dims(h), :], None)],
                                 None).T.astype(BF16)
    lam = _diff_lambda(lam_ref, lam_init)
    oc_ref[...] = _diff_heads_t(dqt_ref[0], lambda h: [(dk_ref[...], dvt_ref[0, head_dims(h), :], None)], lam,
                                gsubt_ref[...], lam_init, None).T.astype(BF16)
    od_ref[...] = _gqa_heads_t(gqt_ref[0], lambda n: [(gk_ref[...], gvt_ref[0, head_dims(n), :], None)],
                               None).T.astype(BF16)


def _ctx_attn_call(qkv_c, lam, gsub, lam_init, n_batch):
    specs = []
    for a in qkv_c:
        if a.ndim == 3:
            specs.append(pl.BlockSpec((1, a.shape[1], CTX), lambda b: (b, 0, 0)))
        else:
            specs.append(pl.BlockSpec((CTX, a.shape[1]), lambda b: (b, 0)))
    out = jax.ShapeDtypeStruct((n_batch * CTX, 256), BF16)
    return pl.pallas_call(
        functools.partial(_ctx_attn_kernel, lam_init=lam_init),
        grid=(n_batch,),
        in_specs=specs + [_const_spec(lam), _const_spec(gsub)],
        out_specs=[pl.BlockSpec((CTX, 256), lambda b: (b, 0))] * 4,
        out_shape=[out] * 4,
        compiler_params=_cparams(1),
        name="ctx_attn",
    )(*qkv_c, lam, gsub)


def _merge_kernel(x_ref, mod_ref, g_ref, oa_ref, ob_ref, oc_ref, od_ref, wg_ref, wb_ref, wo_ref, out_ref):
    x = x_ref[...]
    mod = mod_ref[0]
    xn = _modnorm(x, g_ref[...], mod[:, 0:D], mod[:, D:2 * D]).astype(BF16)
    acc = jnp.zeros(x.shape, F32)
    for n, o_ref in enumerate((oa_ref, ob_ref, oc_ref, od_ref)):
        gate = jax.nn.sigmoid(_dot(xn, wg_ref[:, n * D:(n + 1) * D]))
        acc = acc + gate * _dot(o_ref[...], wb_ref[n])
    y = _dot(acc.astype(BF16), wo_ref[...])
    out_ref[...] = x + mod[:, 2 * D:3 * D] * y


def _merge_call(rows, mod, mod_row, g1, branches, w_gate, w_branch, w_out, tm):
    n_rows = rows.shape[0]
    tm = min(tm, n_rows)
    tok = lambda w: pl.BlockSpec((tm, w), lambda i: (i, 0))
    return pl.pallas_call(
        _merge_kernel,
        grid=(n_rows // tm,),
        in_specs=[tok(D), pl.BlockSpec((1, 1, 6 * D), lambda i: (mod_row(i), 0, 0)), _const_spec(g1)]
                 + [tok(256)] * 4 + [_const_spec(w_gate), _const_spec(w_branch), _const_spec(w_out)],
        out_specs=tok(D),
        out_shape=jax.ShapeDtypeStruct((n_rows, D), F32),
        compiler_params=_cparams(1),
        name="merge",
    )(rows, mod, g1, *branches, w_gate, w_branch, w_out)


def _mlp_kernel(x_ref, mod_ref, g_ref, wu_ref, wd_ref, out_ref):
    x = x_ref[...]
    mod = mod_ref[0]
    xn = _modnorm(x, g_ref[...], mod[:, 3 * D:4 * D], mod[:, 4 * D:5 * D]).astype(BF16)
    h = jnp.square(jnp.maximum(_dot(xn, wu_ref[...]), 0.0))
    out_ref[...] = x + mod[:, 5 * D:6 * D] * _dot(h.astype(BF16), wd_ref[...])


def _mlp_call(rows, mod, mod_row, g2, w_up, w_down, tm):
    n_rows = rows.shape[0]
    tm = min(tm, n_rows)
    tok = lambda w: pl.BlockSpec((tm, w), lambda i: (i, 0))
    return pl.pallas_call(
        _mlp_kernel,
        grid=(n_rows // tm,),
        in_specs=[tok(D), pl.BlockSpec((1, 1, 6 * D), lambda i: (mod_row(i), 0, 0)), _const_spec(g2),
                  _const_spec(w_up), _const_spec(w_down)],
        out_specs=tok(D),
        out_shape=jax.ShapeDtypeStruct((n_rows, D), F32),
        compiler_params=_cparams(1),
        name="mlp",
    )(rows, mod, g2, w_up, w_down)


def _block_diag_ones(n, seg):
    i = jnp.arange(n) // seg
    return (i[:, None] == i[None, :]).astype(BF16)


def _rope_parts(rot_dim):
    t = jnp.arange(SEQ)
    rows, cols = t // GRID_W, t % GRID_W
    n = rot_dim // 4
    inv_freq = jnp.power(ROPE_THETA, -jnp.arange(n, dtype=F32) / n)
    ang_r = rows.astype(F32)[:, None] * inv_freq
    ang_c = cols.astype(F32)[:, None] * inv_freq
    ang = jnp.concatenate([ang_r, ang_r, ang_c, ang_c], axis=-1)
    cos, sin = jnp.cos(ang), jnp.sin(ang)
    even = (jnp.arange(rot_dim) // n) % 2 == 0
    return cos, jnp.where(even, -sin, 0.0), jnp.where(even, 0.0, sin)


def _rope_tables():
    r32, r64 = _rope_parts(MLA_ROPE), _rope_parts(HEAD_DIM)
    ones = jnp.ones((SEQ, MLA_NOPE), F32)
    zeros_n = jnp.zeros((SEQ, MLA_NOPE), F32)
    pad = jnp.zeros((SEQ, LANES - MLA_QK), F32)
    mla = [jnp.concatenate([lead, part, pad], axis=1) for lead, part in zip((ones, zeros_n, zeros_n), r32)]
    return jnp.stack(mla + [jnp.tile(a, (1, LANES // MLA_ROPE)) for a in r32]
                     + [jnp.tile(a, (1, LANES // HEAD_DIM)) for a in r64])


def _na_pair_table(rpb):
    kc = jnp.arange(GRID_W)[:, None]
    qc = jnp.arange(GRID_W)[None, :]
    onehot = (kc - qc + NA_COLS - 1)[None] == jnp.arange(2 * NA_COLS - 1)[:, None, None]
    toeplitz = jnp.einsum("hrd,dkq->hrkq", rpb, onehot.astype(F32), precision=lax.Precision.HIGHEST)
    c0 = jnp.clip(qc - NA_COLS // 2, 0, GRID_W - NA_COLS)
    in_window = (kc >= c0) & (kc < c0 + NA_COLS)
    masked = jnp.where(in_window, toeplitz * LOG2E, MASK_VALUE)
    ext = jnp.pad(masked, ((0, 0), (1, 1), (0, 0), (0, 0)), constant_values=MASK_VALUE)
    return jnp.concatenate([ext[:, 1:], ext[:, :-1]], axis=-1)


def _layer_params(l, p):
    w_in = p["w_in"][l]
    zcol = lambda n: jnp.zeros((D, n), F32)
    kr = w_in[:, C_KR:C_KR + MLA_ROPE]
    kr_slots = jnp.concatenate([jnp.concatenate([zcol(MLA_NOPE), kr, zcol(LANES - MLA_QK)], axis=1)] * 4, axis=1)
    gq_cols = w_in[:, C_GQ:C_GQ + 256].reshape(D, 4, HEAD_DIM)[:, jnp.array([0, 2, 1, 3])].reshape(D, 256)
    w_small = jnp.concatenate([
        w_in[:, C_CQ:C_CQ + MLA_Q_LORA], zcol(256 - MLA_Q_LORA),
        w_in[:, C_CKV:C_CKV + MLA_KV_LORA],
        kr_slots,
        w_in[:, C_NA:C_NA + 768],
        w_in[:, C_DF:C_DF + 768],
        gq_cols, w_in[:, C_GQ + 256:C_GQ + 512]], axis=1).astype(BF16)

    w_uq = p["w_mla_uq"][l].reshape(MLA_Q_LORA, 4, MLA_QK)
    w_uq = jnp.pad(w_uq, ((0, 256 - MLA_Q_LORA), (0, 0), (0, LANES - MLA_QK))).reshape(256, 512).astype(BF16)
    w_ukv = p["w_mla_ukv"][l].reshape(MLA_KV_LORA, 4, 2 * MLA_NOPE)
    w_uk = jnp.pad(w_ukv[:, :, :MLA_NOPE], ((0, 0), (0, 0), (0, LANES - MLA_NOPE))).reshape(MLA_KV_LORA, 512).astype(BF16)
    w_uv = w_ukv[:, :, MLA_NOPE:].reshape(MLA_KV_LORA, 256).astype(BF16)

    mla_gain = lambda g: jnp.tile(jnp.pad(g, (0, LANES - MLA_QK)), 4)
    row = lambda g: jnp.pad(g, (0, 512 - g.shape[0]))
    hd_scale = HEAD_DIM ** -0.5 * LOG2E
    gains = jnp.stack([
        row(mla_gain(p["g_mla_q"][l]) * (MLA_QK ** -0.5 * LOG2E)), row(mla_gain(p["g_mla_k"][l])),
        row(jnp.tile(p["g_na_q"][l], 4) * hd_scale), row(jnp.tile(p["g_na_k"][l], 4)),
        row(jnp.tile(p["g_diff_q"][l], 8) * (DIFF_DIM ** -0.5 * LOG2E)), row(jnp.tile(p["g_diff_k"][l], 8)),
        row(jnp.tile(p["g_gqa_q"][l], 4) * hd_scale), row(jnp.tile(p["g_gqa_k"][l], 2))])

    def score_bound(dim, g_q, g_k):
        return (LOG2E * dim ** 0.5 * jnp.max(jnp.abs(g_q)) * jnp.max(jnp.abs(g_k))).reshape(1).astype(F32)

    w_branch = p["w_branch"][l]
    wb_gqa = w_branch[3].reshape(4, HEAD_DIM, D)[jnp.array([0, 2, 1, 3])].reshape(256, D)
    w_branch = jnp.concatenate([w_branch[:3], wb_gqa[None]], axis=0).astype(BF16)

    return dict(
        g1=p["g_norm1"][l][None], g2=p["g_norm2"][l][None],
        w_small=w_small, w_gate=w_in[:, C_GATE:].astype(BF16),
        g_qa=jnp.pad(p["g_mla_qa"][l], (0, 256 - MLA_Q_LORA))[None], g_kva=p["g_mla_kva"][l][None],
        w_uq=w_uq, w_uk=w_uk, w_uv=w_uv, gains=gains,
        mla_bound=score_bound(MLA_QK, p["g_mla_q"][l], p["g_mla_k"][l]),
        na_bound=(score_bound(HEAD_DIM, p["g_na_q"][l], p["g_na_k"][l])
                  + LOG2E * jnp.max(jnp.abs(p["na_rpb"][l]))),
        diff_bound=score_bound(DIFF_DIM, p["g_diff_q"][l], p["g_diff_k"][l]),
        gqa_bound=score_bound(HEAD_DIM, p["g_gqa_q"][l], p["g_gqa_k"][l]),
        na_pair=_na_pair_table(p["na_rpb"][l]),
        lam=jnp.stack([p["diff_lq1"][l], p["diff_lk1"][l], p["diff_lq2"][l], p["diff_lk2"][l]]),
        g_sub=jnp.tile(p["g_diff_sub"][l], 4)[:, None],
        w_branch=w_branch, w_out=p["w_out"][l].astype(BF16),
        w_up=p["w_up"][l].astype(BF16), w_down=p["w_down"][l].astype(BF16),
    )


def kernel(x, c, ctx, c_ctx, w_ada, b_ada, g_norm1, g_norm2, w_in, g_mla_qa, w_mla_uq, g_mla_kva, w_mla_ukv, g_mla_q, g_mla_k, g_na_q, g_na_k, na_rpb, g_diff_q, g_diff_k, diff_lq1, diff_lk1, diff_lq2, diff_lk2, g_diff_sub, g_gqa_q, g_gqa_k, w_branch, w_out, w_up, w_down):
    p = dict(w_in=w_in, g_norm1=g_norm1, g_norm2=g_norm2, g_mla_qa=g_mla_qa, w_mla_uq=w_mla_uq,
             g_mla_kva=g_mla_kva, w_mla_ukv=w_mla_ukv, g_mla_q=g_mla_q, g_mla_k=g_mla_k,
             g_na_q=g_na_q, g_na_k=g_na_k, na_rpb=na_rpb, g_diff_q=g_diff_q, g_diff_k=g_diff_k,
             diff_lq1=diff_lq1, diff_lk1=diff_lk1, diff_lq2=diff_lq2, diff_lk2=diff_lk2,
             g_diff_sub=g_diff_sub, g_gqa_q=g_gqa_q, g_gqa_k=g_gqa_k, w_branch=w_branch,
             w_out=w_out, w_up=w_up, w_down=w_down)
    n_batch = x.shape[0]
    depth = w_ada.shape[0]
    assert x.shape[1:] == (SEQ, D) and ctx.shape[1:] == (CTX, D)

    mod_rows = -(-(n_batch + 1) // 8) * 8
    c_all = jnp.concatenate([c, c_ctx[None], jnp.zeros((mod_rows - n_batch - 1, D), F32)], axis=0)
    mods = _ada_call(c_all, w_ada, b_ada).reshape(depth, mod_rows, 1, 6 * D)
    lat_tiles = SEQ // TM
    lat_row = lambda i: i // lat_tiles
    ctx_row = lambda i: n_batch

    consts = dict(bd128=_block_diag_ones(256, 128), bd64=_block_diag_ones(256, 64),
                  bd32=_block_diag_ones(256, 32), bd64h=_block_diag_ones(128, 64))
    tabs = _rope_tables()

    xl = x.reshape(n_batch * SEQ, D)
    xc = ctx.reshape(n_batch * CTX, D)
    for l in range(depth):
        need_ctx = l < depth - 1
        lam_init = 0.8 - 0.6 * math.exp(-0.3 * l)
        lw = _layer_params(l, p)
        mod = mods[l]

        lat = _qkv_call(xl, mod, lambda b: b, lw, consts, tabs, n_batch, SEQ, TM)
        cx = _qkv_call(xc, mod, ctx_row, lw, consts, None, n_batch, CTX, CTX)
        mqt, mk, mvt, nqt, nk, nvt, dqt, dk, dvt, gqt, gk, gvt = lat
        oa = _latent_attn_call(_mla_attn_kernel, "mla_attn", lw["mla_bound"], mqt, mk, mvt, cx[1], cx[2], [], n_batch)
        ob = _na_attn_call(lw["na_bound"], nqt, nk, nvt, cx[4], cx[5], lw["na_pair"], n_batch)
        oc = _latent_attn_call(functools.partial(_diff_attn_kernel, lam_init=lam_init), "diff_attn",
                               lw["diff_bound"], dqt, dk, dvt, cx[7], cx[8], [lw["lam"], lw["g_sub"]], n_batch)
        od = _latent_attn_call(_gqa_attn_kernel, "gqa_attn", lw["gqa_bound"], gqt, gk, gvt, cx[10], cx[11], [],
                               n_batch)
        if need_ctx:
            oc_all = _ctx_attn_call(cx, lw["lam"], lw["g_sub"], lam_init, n_batch)
            xc = _merge_call(xc, mod, ctx_row, lw["g1"], oc_all, lw["w_gate"], lw["w_branch"], lw["w_out"], TM)
            xc = _mlp_call(xc, mod, ctx_row, lw["g2"], lw["w_up"], lw["w_down"], TM)
        xl = _merge_call(xl, mod, lat_row, lw["g1"], (oa, ob, oc, od), lw["w_gate"], lw["w_branch"], lw["w_out"], TM)
        xl = _mlp_call(xl, mod, lat_row, lw["g2"], lw["w_up"], lw["w_down"], TM)
    return xl.reshape(n_batch, SEQ, D)
```

```python
import functools
import math
from typing import NamedTuple

import jax
import jax.numpy as jnp
from jax import lax
from jax.experimental import pallas as pl
from jax.experimental.pallas import tpu as pltpu

F32 = jnp.float32
BF16 = jnp.bfloat16

D = 1024
SEQ = 2048
CTX = 256
GRID_W = 64
N_ROWS = SEQ // GRID_W
ROPE_THETA = 10000.0
EPS = 1e-6
HEAD_DIM = 64
MLA_Q_LORA = 192
MLA_KV_LORA = 128
MLA_NOPE = 64
MLA_ROPE = 32
MLA_QK = MLA_NOPE + MLA_ROPE
NA_ROWS = 8
NA_COLS = 16
DIFF_DIM = 32
D_FF = 4 * D

LANES = 128
TM = 512
TQ = 2048
QC = 1024
NA_TQ = 256
NA_WIN = 3
MASK_VALUE = -1e30
LOG2E = math.log2(math.e)
MAX_FIXED_SHIFT = 50.0

C_CQ, C_CKV, C_KR, C_NA, C_DF, C_GQ, C_GATE = 0, 192, 320, 352, 1120, 1888, 2400
Z_CQ, Z_CKV, Z_KR, Z_NA, Z_DF, Z_GQ, Z_W = 0, 256, 384, 896, 1664, 2432, 2944
G_MQ, G_MK, G_NQ, G_NK, G_DQ, G_DK, G_GQ, G_GK = range(8)
T_MLA, T_R32, T_R64 = 0, 3, 6
B_MLA, B_NA, B_DIFF, B_GQA, N_BOUNDS = 0, 1, 2, 3, 4

VMEM_LIMIT = 56 * 1024 * 1024


def _cparams(n_axes):
    return pltpu.CompilerParams(dimension_semantics=("arbitrary",) * n_axes,
                                vmem_limit_bytes=VMEM_LIMIT)


def _dot(a, b):
    return jnp.dot(a, b, preferred_element_type=F32)


class _Layer(NamedTuple):
    stacked: jax.Array
    index: int


def _operand(a):
    return a.stacked if isinstance(a, _Layer) else a


def _const_spec(a):
    if isinstance(a, _Layer):
        shape = a.stacked.shape[1:]
        return pl.BlockSpec((None,) + shape, lambda *_: (a.index,) + (0,) * len(shape),
                            pipeline_mode=pl.Buffered(1))
    return pl.BlockSpec(a.shape, lambda *_: (0,) * a.ndim, pipeline_mode=pl.Buffered(1))


def _mod_spec(layer, row_fn):
    return pl.BlockSpec((None, 1, 1, 6 * D), lambda *idx: (layer, row_fn(*idx), 0, 0))


def _modnorm(x, g, shift, scale):
    y = x * lax.rsqrt(jnp.mean(x * x, axis=-1, keepdims=True) + EPS) * g
    return y * (1.0 + scale) + shift


def _ada_kernel(c_ref, w_ref, b_ref, o_ref):
    c = c_ref[...]
    s = c * jax.nn.sigmoid(c)
    o_ref[0] = _dot(s.astype(BF16), w_ref[0].astype(BF16)) + b_ref[0]


def _ada_call(c_all, w_ada, b_ada):
    n_layers = w_ada.shape[0]
    rows = c_all.shape[0]
    bn = 1536
    return pl.pallas_call(
        _ada_kernel,
        grid=(n_layers, 6 * D // bn),
        in_specs=[pl.BlockSpec((rows, D), lambda l, j: (0, 0)),
                  pl.BlockSpec((1, D, bn), lambda l, j: (l, 0, j)),
                  pl.BlockSpec((1, 1, bn), lambda l, j: (l, 0, j))],
        out_specs=pl.BlockSpec((1, rows, bn), lambda l, j: (l, 0, j)),
        out_shape=jax.ShapeDtypeStruct((n_layers, rows, 6 * D), F32),
        compiler_params=_cparams(2),
        name="ada",
    )(c_all, w_ada, b_ada.reshape(n_layers, 1, 6 * D))


def _seg_mean_sq(t, bd, inv_d):
    x2 = t * t
    hi = x2.astype(BF16)
    lo = (x2 - hi.astype(F32)).astype(BF16)
    return (_dot(hi, bd) + _dot(lo, bd)) * inv_d


def _qkv_kernel(*refs, rope):
    (x_ref, mod_ref, g1_ref, w_ref, bd128_ref, bd64_ref, bd32_ref, bd64h_ref,
     gqa_ref, gkva_ref, wuq_ref, wuk_ref, wuv_ref, gains_ref) = refs[:14]
    tab_ref = refs[14] if rope else None
    (mqt_ref, mk_ref, mvt_ref, nqt_ref, nk_ref, nvt_ref,
     dqt_ref, dk_ref, dvt_ref, gqt_ref, gk_ref, gvt_ref) = refs[-12:]

    mod = mod_ref[0]
    xn = _modnorm(x_ref[...], g1_ref[...], mod[:, 0:D], mod[:, D:2 * D]).astype(BF16)

    def cols(lo, hi):
        return _dot(xn, w_ref[:, lo:hi])

    def put(o_ref, lo, val):
        o_ref[:, lo:lo + LANES] = val.astype(BF16)

    def put_t(o_ref, lo, val):
        o_ref[0, lo:lo + LANES, :] = val.T.astype(BF16)

    def section(t, bd, inv_d, gain_row, tab, shift, o_ref, store):
        width = t.shape[1]
        y = t * lax.rsqrt(_seg_mean_sq(t, bd, inv_d) + EPS) * gains_ref[gain_row:gain_row + 1, 0:width]
        for j in range(width // LANES):
            yc = y[:, j * LANES:(j + 1) * LANES]
            if rope and tab is not None:
                yc = (yc * tab_ref[tab] + pltpu.roll(yc, LANES - shift, 1) * tab_ref[tab + 1]
                      + pltpu.roll(yc, shift, 1) * tab_ref[tab + 2])
            store(o_ref, j * LANES, yc)

    za = cols(Z_CQ, Z_NA)
    cq = za[:, Z_CQ:Z_CQ + 256]
    cqn = cq * lax.rsqrt(jnp.sum(cq * cq, axis=-1, keepdims=True) * (1.0 / MLA_Q_LORA) + EPS) * gqa_ref[...]
    q = _dot(cqn.astype(BF16), wuq_ref[...])
    ckv = za[:, Z_CKV:Z_CKV + 128]
    kvn = (ckv * lax.rsqrt(jnp.mean(ckv * ckv, axis=-1, keepdims=True) + EPS) * gkva_ref[...]).astype(BF16)
    k = _dot(kvn, wuk_ref[...]) + za[:, Z_KR:Z_KR + 512]
    mvt_ref[0] = _dot(kvn, wuv_ref[...]).T.astype(BF16)
    bd128 = bd128_ref[...]
    for c in range(2):
        sl = slice(256 * c, 256 * c + 256)
        section(q[:, sl], bd128, 1.0 / MLA_QK, G_MQ, T_MLA, MLA_ROPE // 4, mqt_ref.at[:, sl, :], put_t)
        section(k[:, sl], bd128, 1.0 / MLA_QK, G_MK, T_MLA, MLA_ROPE // 4, mk_ref.at[:, sl], put)

    zn = cols(Z_NA, Z_DF)
    bd64 = bd64_ref[...]
    section(zn[:, 0:256], bd64, 1.0 / HEAD_DIM, G_NQ, None, 0, nqt_ref, put_t)
    section(zn[:, 256:512], bd64, 1.0 / HEAD_DIM, G_NK, None, 0, nk_ref, put)
    nvt_ref[0] = zn[:, 512:768].T.astype(BF16)

    zd = cols(Z_DF, Z_GQ)
    bd32 = bd32_ref[...]
    section(zd[:, 0:256], bd32, 1.0 / DIFF_DIM, G_DQ, T_R32, DIFF_DIM // 4, dqt_ref, put_t)
    section(zd[:, 256:512], bd32, 1.0 / DIFF_DIM, G_DK, T_R32, DIFF_DIM // 4, dk_ref, put)
    dvt_ref[0] = zd[:, 512:768].T.astype(BF16)

    zg = cols(Z_GQ, Z_W)
    section(zg[:, 0:256], bd64, 1.0 / HEAD_DIM, G_GQ, T_R64, HEAD_DIM // 4, gqt_ref, put_t)
    section(zg[:, 256:384], bd64h_ref[...], 1.0 / HEAD_DIM, G_GK, T_R64, HEAD_DIM // 4, gk_ref, put)
    gvt_ref[0] = zg[:, 384:512].T.astype(BF16)


def _qkv_call(rows, mods, layer, mod_row, lw, consts, tabs, n_batch, seq, tm):
    tiles = seq // tm
    row_map = lambda t, b: (b * tiles + t, 0)
    tok_spec = lambda w: pl.BlockSpec((tm, w), row_map)
    tr_spec = lambda w: pl.BlockSpec((1, w, tm), lambda t, b: (b, 0, t))
    tok_shape = lambda w: jax.ShapeDtypeStruct((n_batch * seq, w), BF16)
    tr_shape = lambda w: jax.ShapeDtypeStruct((n_batch, w, seq), BF16)
    small = [lw["g1"], lw["w_small"], consts["bd128"], consts["bd64"], consts["bd32"], consts["bd64h"],
             lw["g_qa"], lw["g_kva"], lw["w_uq"], lw["w_uk"], lw["w_uv"], lw["gains"]]
    in_specs = [tok_spec(D), _mod_spec(layer, lambda t, b: mod_row(b))]
    in_specs += [_const_spec(a) for a in small]
    args = [rows, mods] + [_operand(a) for a in small]
    if tabs is not None:
        in_specs.append(pl.BlockSpec((tabs.shape[0], tm, LANES), lambda t, b: (0, t, 0)))
        args.append(tabs)
    return pl.pallas_call(
        functools.partial(_qkv_kernel, rope=tabs is not None),
        grid=(tiles, n_batch),
        in_specs=in_specs,
        out_specs=[tr_spec(512), tok_spec(512), tr_spec(256),
                   tr_spec(256), tok_spec(256), tr_spec(256),
                   tr_spec(256), tok_spec(256), tr_spec(256),
                   tr_spec(256), tok_spec(128), tr_spec(128)],
        out_shape=[tr_shape(512), tok_shape(512), tr_shape(256),
                   tr_shape(256), tok_shape(256), tr_shape(256),
                   tr_shape(256), tok_shape(256), tr_shape(256),
                   tr_shape(256), tok_shape(128), tr_shape(128)],
        compiler_params=_cparams(2),
        name="qkv",
    )(*args)


def _keep_rows(x, lo, hi):
    row = lax.broadcasted_iota(jnp.int32, x.shape, 0)
    return jnp.where(jnp.logical_and(row >= lo, row < hi), x, jnp.zeros_like(x))


def _sum_all(xs):
    return functools.reduce(lambda a, b: a + b, xs)


def _softmax_vt(qt, groups, shift):
    s = [_dot(k, qt) if bias is None else _dot(k, qt) + bias for k, _, bias in groups]
    if shift is None:
        shift = functools.reduce(jnp.maximum, [jnp.max(x, axis=0, keepdims=True) for x in s])
    p = [jnp.exp2(x - shift) for x in s]
    l = _sum_all([jnp.sum(x, axis=0, keepdims=True) for x in p])
    return _sum_all([_dot(g[1], x.astype(BF16)) for x, g in zip(p, groups)]) / l


def _mla_heads_t(qt_fn, groups_fn, shift):
    outs = []
    for h in range(4):
        c0 = 256 * (h // 2)
        lo = LANES * (h % 2)
        outs.append(_softmax_vt(_keep_rows(qt_fn(c0), lo, lo + LANES), groups_fn(h, c0), shift))
    return jnp.concatenate(outs, axis=0)


def _diff_heads_t(qt, groups_fn, lam, gsub_t, lam_init, shift):
    outs = []
    for h in range(4):
        maps = [_softmax_vt(_keep_rows(qt, 64 * h + DIFF_DIM * m, 64 * h + DIFF_DIM * (m + 1)), groups_fn(h), shift)
                for m in range(2)]
        oh = maps[0] - lam * maps[1]
        ms = jnp.mean(oh * oh, axis=0, keepdims=True)
        outs.append(oh * lax.rsqrt(ms + EPS) * gsub_t[64 * h:64 * h + 64, :])
    return jnp.concatenate(outs, axis=0) * (1.0 - lam_init)


def _gqa_heads_t(qt, groups_fn, shift):
    outs = []
    for g in range(2):
        qc = qt[LANES * g:LANES * (g + 1), :]
        for n in range(2):
            outs.append(_softmax_vt(_keep_rows(qc, 64 * n, 64 * n + 64), groups_fn(n), shift))
    return jnp.concatenate(outs, axis=0)


def _plain_heads_t(qt, groups_fn, shift):
    return jnp.concatenate([_softmax_vt(_keep_rows(qt, 64 * h, 64 * h + 64), groups_fn(h), shift) for h in range(4)],
                           axis=0)


def _diff_lambda(lam_ref, lam_init):
    lq1, lk1, lq2, lk2 = (lam_ref[i:i + 1, :] for i in range(4))
    return (jnp.exp(jnp.sum(lq1 * lk1, axis=-1, keepdims=True))
            - jnp.exp(jnp.sum(lq2 * lk2, axis=-1, keepdims=True)) + lam_init)


def _latent_attention(shift_ref, bound_index, chunk_fn):
    bound = shift_ref[bound_index]

    def run(shift):
        def step(c, carry):
            chunk_fn(pl.ds(pl.multiple_of(c * QC, QC), QC), shift)
            return carry

        lax.fori_loop(0, TQ // QC, step, 0)

    @pl.when(bound <= MAX_FIXED_SHIFT)
    def _():
        run(bound)

    @pl.when(jnp.logical_not(bound <= MAX_FIXED_SHIFT))
    def _():
        run(None)


def _mla_attn_kernel(shift_ref, qt_ref, kl_ref, vtl_ref, kc_ref, vtc_ref, o_ref, *, bound_index):
    def chunk(cols, shift):
        def groups_fn(h, c0):
            dims = slice(64 * h, 64 * h + 64)
            return [(kl_ref[:, c0:c0 + 256], vtl_ref[0, dims, :], None),
                    (kc_ref[:, c0:c0 + 256], vtc_ref[0, dims, :], None)]

        ot = _mla_heads_t(lambda c0: qt_ref[0, c0:c0 + 256, cols], groups_fn, shift)
        o_ref[cols, :] = ot.T.astype(BF16)

    _latent_attention(shift_ref, bound_index, chunk)


def _diff_attn_kernel(shift_ref, qt_ref, kl_ref, vtl_ref, kc_ref, vtc_ref, lam_ref, gsubt_ref, o_ref, *, lam_init,
                      bound_index):
    def chunk(cols, shift):
        def groups_fn(h):
            dims = slice(64 * h, 64 * h + 64)
            return [(kl_ref[...], vtl_ref[0, dims, :], None), (kc_ref[...], vtc_ref[0, dims, :], None)]

        lam = _diff_lambda(lam_ref, lam_init)
        ot = _diff_heads_t(qt_ref[0, :, cols], groups_fn, lam, gsubt_ref[...], lam_init, shift)
        o_ref[cols, :] = ot.T.astype(BF16)

    _latent_attention(shift_ref, bound_index, chunk)


def _gqa_attn_kernel(shift_ref, qt_ref, kl_ref, vtl_ref, kc_ref, vtc_ref, o_ref, *, bound_index):
    def chunk(cols, shift):
        def groups_fn(n):
            dims = slice(64 * n, 64 * n + 64)
            return [(kl_ref[...], vtl_ref[0, dims, :], None), (kc_ref[...], vtc_ref[0, dims, :], None)]

        ot = _gqa_heads_t(qt_ref[0, :, cols], groups_fn, shift)
        o_ref[cols, :] = ot.T.astype(BF16)

    _latent_attention(shift_ref, bound_index, chunk)


def _latent_attn_call(kernel, name, bounds, bound_index, qt, kl, vtl, kc, vtc, extra, n_batch):
    wq, wk, wv = qt.shape[1], kl.shape[1], vtl.shape[1]
    tiles = SEQ // TQ
    return pl.pallas_call(
        functools.partial(kernel, bound_index=bound_index),
        grid=(n_batch, tiles),
        in_specs=[pl.BlockSpec(memory_space=pltpu.SMEM),
                  pl.BlockSpec((1, wq, TQ), lambda b, t: (b, 0, t)),
                  pl.BlockSpec((SEQ, wk), lambda b, t: (b, 0)),
                  pl.BlockSpec((1, wv, SEQ), lambda b, t: (b, 0, 0)),
                  pl.BlockSpec((CTX, wk), lambda b, t: (b, 0)),
                  pl.BlockSpec((1, wv, CTX), lambda b, t: (b, 0, 0))] + [_const_spec(a) for a in extra],
        out_specs=pl.BlockSpec((TQ, 256), lambda b, t: (b * tiles + t, 0)),
        out_shape=jax.ShapeDtypeStruct((n_batch * SEQ, 256), BF16),
        compiler_params=_cparams(2),
        name=name,
    )(bounds, qt, kl, vtl, kc, vtc, *map(_operand, extra))


def _na_window(t):
    return jnp.clip(t - 1, 0, SEQ // NA_TQ - NA_WIN)


def _na_build_bias(t, pair_ref, bias_ref):
    rows_per_tile = NA_TQ // GRID_W
    k_row0 = rows_per_tile * _na_window(t)
    lane = lax.broadcasted_iota(jnp.int32, (GRID_W, LANES), 1)
    for i in range(rows_per_tile // 2):
        qr = rows_per_tile * t + 2 * i
        r0 = jnp.clip(qr - NA_ROWS // 2, 0, N_ROWS - NA_ROWS)
        r1 = jnp.clip(qr + 1 - NA_ROWS // 2, 0, N_ROWS - NA_ROWS)
        for kl in range(NA_WIN * rows_per_tile):
            kr = k_row0 + kl
            ok0 = jnp.logical_and(kr >= r0, kr < r0 + NA_ROWS).astype(jnp.int32)
            ok1 = jnp.logical_and(kr >= r1, kr < r1 + NA_ROWS).astype(jnp.int32)
            ok = jnp.where(lane < GRID_W, ok0, ok1) > 0
            d = jnp.clip(kr - qr + NA_ROWS - 1, 0, 2 * NA_ROWS - 1)
            for h in range(4):
                blk = jnp.where(ok, pair_ref[h, d], MASK_VALUE)
                bias_ref[h, GRID_W * kl:GRID_W * (kl + 1), LANES * i:LANES * (i + 1)] = blk


def _na_attn_kernel(shift_ref, qt_ref, k0_ref, k1_ref, k2_ref, kc_ref, vt0_ref, vt1_ref, vt2_ref, vtc_ref, pair_ref,
                    o_ref, bias_ref, *, bound_index):
    t = pl.program_id(0)

    @pl.when(pl.program_id(1) == 0)
    def _():
        _na_build_bias(t, pair_ref, bias_ref)

    win = ((k0_ref, vt0_ref), (k1_ref, vt1_ref), (k2_ref, vt2_ref))

    def groups_fn(h):
        dims = slice(64 * h, 64 * h + 64)
        return ([(k[...], vt[0, dims, :], bias_ref[h, NA_TQ * j:NA_TQ * (j + 1), :]) for j, (k, vt) in enumerate(win)]
                + [(kc_ref[...], vtc_ref[0, dims, :], None)])

    bound = shift_ref[bound_index]

    @pl.when(bound <= MAX_FIXED_SHIFT)
    def _():
        o_ref[...] = _plain_heads_t(qt_ref[0], groups_fn, bound).T.astype(BF16)

    @pl.when(jnp.logical_not(bound <= MAX_FIXED_SHIFT))
    def _():
        o_ref[...] = _plain_heads_t(qt_ref[0], groups_fn, None).T.astype(BF16)


def _na_attn_call(bounds, bound_index, qt, kl, vtl, kc, vtc, pair, n_batch):
    tiles = SEQ // NA_TQ
    k_spec = lambda j: pl.BlockSpec((NA_TQ, 256), lambda t, b: (b * tiles + _na_window(t) + j, 0))
    vt_spec = lambda j: pl.BlockSpec((1, 256, NA_TQ), lambda t, b: (b, 0, _na_window(t) + j))
    return pl.pallas_call(
        functools.partial(_na_attn_kernel, bound_index=bound_index),
        grid=(tiles, n_batch),
        in_specs=[pl.BlockSpec(memory_space=pltpu.SMEM),
                  pl.BlockSpec((1, 256, NA_TQ), lambda t, b: (b, 0, t)),
                  k_spec(0), k_spec(1), k_spec(2),
                  pl.BlockSpec((CTX, 256), lambda t, b: (b, 0)),
                  vt_spec(0), vt_spec(1), vt_spec(2),
                  pl.BlockSpec((1, 256, CTX), lambda t, b: (b, 0, 0)),
                  _const_spec(pair)],
        out_specs=pl.BlockSpec((NA_TQ, 256), lambda t, b: (b * tiles + t, 0)),
        out_shape=jax.ShapeDtypeStruct((n_batch * SEQ, 256), BF16),
        scratch_shapes=[pltpu.VMEM((4, NA_WIN * NA_TQ, NA_TQ), F32)],
        compiler_params=_cparams(2),
        name="na_attn",
    )(bounds, qt, kl, kl, kl, kc, vtl, vtl, vtl, vtc, _operand(pair))


def _ctx_attn_kernel(mqt_ref, mk_ref, mvt_ref, nqt_ref, nk_ref, nvt_ref, dqt_ref, dk_ref, dvt_ref,
                     gqt_ref, gk_ref, gvt_ref, lam_ref, gsubt_ref, oa_ref, ob_ref, oc_ref, od_ref, *, lam_init):
    head_dims = lambda h: slice(64 * h, 64 * h + 64)
    oa_ref[...] = _mla_heads_t(lambda c0: mqt_ref[0, c0:c0 + 256, :],
                               lambda h, c0: [(mk_ref[:, c0:c0 + 256], mvt_ref[0, head_dims(h), :], None)],
                               None).T.astype(BF16)
    ob_ref[...] = _plain_heads_t(nqt_ref[0], lambda h: [(nk_ref[...], nvt_ref[0, head_dims(h), :], None)],
                                 None).T.astype(BF16)
    lam = _diff_lambda(lam_ref, lam_init)
    oc_ref[...] = _diff_heads_t(dqt_ref[0], lambda h: [(dk_ref[...], dvt_ref[0, head_dims(h), :], None)], lam,
                                gsubt_ref[...], lam_init, None).T.astype(BF16)
    od_ref[...] = _gqa_heads_t(gqt_ref[0], lambda n: [(gk_ref[...], gvt_ref[0, head_dims(n), :], None)],
                               None).T.astype(BF16)


def _ctx_attn_call(qkv_c, lam, gsub, lam_init, n_batch):
    specs = []
    for a in qkv_c:
        if a.ndim == 3:
            specs.append(pl.BlockSpec((1, a.shape[1], CTX), lambda b: (b, 0, 0)))
        else:
            specs.append(pl.BlockSpec((CTX, a.shape[1]), lambda b: (b, 0)))
    out = jax.ShapeDtypeStruct((n_batch * CTX, 256), BF16)
    return pl.pallas_call(
        functools.partial(_ctx_attn_kernel, lam_init=lam_init),
        grid=(n_batch,),
        in_specs=specs + [_const_spec(lam), _const_spec(gsub)],
        out_specs=[pl.BlockSpec((CTX, 256), lambda b: (b, 0))] * 4,
        out_shape=[out] * 4,
        compiler_params=_cparams(1),
        name="ctx_attn",
    )(*qkv_c, _operand(lam), _operand(gsub))


def _merge_kernel(x_ref, mod_ref, g_ref, oa_ref, ob_ref, oc_ref, od_ref, wg_ref, wb_ref, wo_ref, out_ref):
    x = x_ref[...]
    mod = mod_ref[0]
    xn = _modnorm(x, g_ref[...], mod[:, 0:D], mod[:, D:2 * D]).astype(BF16)
    acc = jnp.zeros(x.shape, F32)
    for n, o_ref in enumerate((oa_ref, ob_ref, oc_ref, od_ref)):
        gate = jax.nn.sigmoid(_dot(xn, wg_ref[:, n * D:(n + 1) * D]))
        acc = acc + gate * _dot(o_ref[...], wb_ref[n])
    y = _dot(acc.astype(BF16), wo_ref[...])
    out_ref[...] = x + mod[:, 2 * D:3 * D] * y


def _merge_call(rows, mods, layer, mod_row, g1, branches, w_gate, w_branch, w_out, tm):
    n_rows = rows.shape[0]
    tm = min(tm, n_rows)
    tok = lambda w: pl.BlockSpec((tm, w), lambda i: (i, 0))
    return pl.pallas_call(
        _merge_kernel,
        grid=(n_rows // tm,),
        in_specs=[tok(D), _mod_spec(layer, mod_row), _const_spec(g1)]
                 + [tok(256)] * 4 + [_const_spec(w_gate), _const_spec(w_branch), _const_spec(w_out)],
        out_specs=tok(D),
        out_shape=jax.ShapeDtypeStruct((n_rows, D), F32),
        compiler_params=_cparams(1),
        name="merge",
    )(rows, mods, _operand(g1), *branches, _operand(w_gate), _operand(w_branch), _operand(w_out))


def _mlp_kernel(x_ref, mod_ref, g_ref, wu_ref, wd_ref, out_ref):
    x = x_ref[...]
    mod = mod_ref[0]
    xn = _modnorm(x, g_ref[...], mod[:, 3 * D:4 * D], mod[:, 4 * D:5 * D]).astype(BF16)
    h = jnp.square(jnp.maximum(_dot(xn, wu_ref[...]), 0.0))
    out_ref[...] = x + mod[:, 5 * D:6 * D] * _dot(h.astype(BF16), wd_ref[...])


def _mlp_call(rows, mods, layer, mod_row, g2, w_up, w_down, tm):
    n_rows = rows.shape[0]
    tm = min(tm, n_rows)
    tok = lambda w: pl.BlockSpec((tm, w), lambda i: (i, 0))
    return pl.pallas_call(
        _mlp_kernel,
        grid=(n_rows // tm,),
        in_specs=[tok(D), _mod_spec(layer, mod_row), _const_spec(g2),
                  _const_spec(w_up), _const_spec(w_down)],
        out_specs=tok(D),
        out_shape=jax.ShapeDtypeStruct((n_rows, D), F32),
        compiler_params=_cparams(1),
        name="mlp",
    )(rows, mods, _operand(g2), _operand(w_up), _operand(w_down))


def _block_diag_ones(n, seg):
    i = jnp.arange(n) // seg
    return (i[:, None] == i[None, :]).astype(BF16)


def _rope_parts(rot_dim):
    t = jnp.arange(SEQ)
    rows, cols = t // GRID_W, t % GRID_W
    n = rot_dim // 4
    inv_freq = jnp.power(ROPE_THETA, -jnp.arange(n, dtype=F32) / n)
    ang_r = rows.astype(F32)[:, None] * inv_freq
    ang_c = cols.astype(F32)[:, None] * inv_freq
    ang = jnp.concatenate([ang_r, ang_r, ang_c, ang_c], axis=-1)
    cos, sin = jnp.cos(ang), jnp.sin(ang)
    even = (jnp.arange(rot_dim) // n) % 2 == 0
    return cos, jnp.where(even, -sin, 0.0), jnp.where(even, 0.0, sin)


def _rope_tables():
    r32, r64 = _rope_parts(MLA_ROPE), _rope_parts(HEAD_DIM)
    ones = jnp.ones((SEQ, MLA_NOPE), F32)
    zeros_n = jnp.zeros((SEQ, MLA_NOPE), F32)
    pad = jnp.zeros((SEQ, LANES - MLA_QK), F32)
    mla = [jnp.concatenate([lead, part, pad], axis=1) for lead, part in zip((ones, zeros_n, zeros_n), r32)]
    return jnp.stack(mla + [jnp.tile(a, (1, LANES // MLA_ROPE)) for a in r32]
                     + [jnp.tile(a, (1, LANES // HEAD_DIM)) for a in r64])


def _na_pair_table(rpb):
    kc = jnp.arange(GRID_W)[:, None]
    qc = jnp.arange(GRID_W)[None, :]
    onehot = (kc - qc + NA_COLS - 1)[None] == jnp.arange(2 * NA_COLS - 1)[:, None, None]
    toeplitz = jnp.einsum("lhrd,dkq->lhrkq", rpb, onehot.astype(F32), precision=lax.Precision.HIGHEST)
    c0 = jnp.clip(qc - NA_COLS // 2, 0, GRID_W - NA_COLS)
    in_window = (kc >= c0) & (kc < c0 + NA_COLS)
    masked = jnp.where(in_window, toeplitz * LOG2E, MASK_VALUE)
    ext = jnp.pad(masked, ((0, 0), (0, 0), (1, 1), (0, 0), (0, 0)), constant_values=MASK_VALUE)
    return jnp.concatenate([ext[:, :, 1:], ext[:, :, :-1]], axis=-1)


def _prepare_params(p):
    w_in = p["w_in"]
    n_layers = w_in.shape[0]
    zcol = lambda n: jnp.zeros((n_layers, D, n), F32)
    kr = w_in[:, :, C_KR:C_KR + MLA_ROPE]
    kr_slots = jnp.concatenate([zcol(MLA_NOPE), kr, zcol(LANES - MLA_QK)] * 4, axis=2)
    head_order = jnp.array([0, 2, 1, 3])
    gq_cols = w_in[:, :, C_GQ:C_GQ + 256].reshape(n_layers, D, 4, HEAD_DIM)[:, :, head_order].reshape(n_layers, D, 256)
    w_small = jnp.concatenate([
        w_in[:, :, C_CQ:C_CQ + MLA_Q_LORA], zcol(256 - MLA_Q_LORA),
        w_in[:, :, C_CKV:C_CKV + MLA_KV_LORA],
        kr_slots,
        w_in[:, :, C_NA:C_NA + 768],
        w_in[:, :, C_DF:C_DF + 768],
        gq_cols, w_in[:, :, C_GQ + 256:C_GQ + 512]], axis=2).astype(BF16)

    w_uq = p["w_mla_uq"].reshape(n_layers, MLA_Q_LORA, 4, MLA_QK)
    w_uq = jnp.pad(w_uq, ((0, 0), (0, 256 - MLA_Q_LORA), (0, 0), (0, LANES - MLA_QK)))
    w_ukv = p["w_mla_ukv"].reshape(n_layers, MLA_KV_LORA, 4, 2 * MLA_NOPE)
    w_uk = jnp.pad(w_ukv[..., :MLA_NOPE], ((0, 0), (0, 0), (0, 0), (0, LANES - MLA_NOPE)))

    mla_gain = lambda g: jnp.tile(jnp.pad(g, ((0, 0), (0, LANES - MLA_QK))), (1, 4))
    rep = lambda g, n: jnp.tile(g, (1, n))
    row = lambda g: jnp.pad(g, ((0, 0), (0, 512 - g.shape[1])))
    hd_scale = HEAD_DIM ** -0.5 * LOG2E
    gains = jnp.stack([
        row(mla_gain(p["g_mla_q"]) * (MLA_QK ** -0.5 * LOG2E)), row(mla_gain(p["g_mla_k"])),
        row(rep(p["g_na_q"], 4) * hd_scale), row(rep(p["g_na_k"], 4)),
        row(rep(p["g_diff_q"], 8) * (DIFF_DIM ** -0.5 * LOG2E)), row(rep(p["g_diff_k"], 8)),
        row(rep(p["g_gqa_q"], 4) * hd_scale), row(rep(p["g_gqa_k"], 2))], axis=1)

    def score_bound(dim, g_q, g_k):
        return LOG2E * dim ** 0.5 * jnp.max(jnp.abs(g_q), axis=-1) * jnp.max(jnp.abs(g_k), axis=-1)

    bounds = jnp.stack([
        score_bound(MLA_QK, p["g_mla_q"], p["g_mla_k"]),
        score_bound(HEAD_DIM, p["g_na_q"], p["g_na_k"]) + LOG2E * jnp.max(jnp.abs(p["na_rpb"]), axis=(1, 2, 3)),
        score_bound(DIFF_DIM, p["g_diff_q"], p["g_diff_k"]),
        score_bound(HEAD_DIM, p["g_gqa_q"], p["g_gqa_k"])], axis=1).astype(F32).reshape(n_layers * N_BOUNDS)

    w_branch = p["w_branch"]
    wb_gqa = w_branch[:, 3].reshape(n_layers, 4, HEAD_DIM, D)[:, head_order].reshape(n_layers, 1, 256, D)
    w_branch = jnp.concatenate([w_branch[:, :3], wb_gqa], axis=1).astype(BF16)

    return dict(
        g1=p["g_norm1"][:, None], g2=p["g_norm2"][:, None],
        w_small=w_small, w_gate=w_in[:, :, C_GATE:].astype(BF16),
        g_qa=jnp.pad(p["g_mla_qa"], ((0, 0), (0, 256 - MLA_Q_LORA)))[:, None], g_kva=p["g_mla_kva"][:, None],
        w_uq=w_uq.reshape(n_layers, 256, 512).astype(BF16), w_uk=w_uk.reshape(n_layers, MLA_KV_LORA, 512).astype(BF16),
        w_uv=w_ukv[..., MLA_NOPE:].reshape(n_layers, MLA_KV_LORA, 256).astype(BF16),
        gains=gains, bounds=bounds, na_pair=_na_pair_table(p["na_rpb"]),
        lam=jnp.stack([p["diff_lq1"], p["diff_lk1"], p["diff_lq2"], p["diff_lk2"]], axis=1),
        g_sub=rep(p["g_diff_sub"], 4)[:, :, None],
        w_branch=w_branch, w_out=p["w_out"].astype(BF16),
        w_up=p["w_up"].astype(BF16), w_down=p["w_down"].astype(BF16),
    )


def kernel(x, c, ctx, c_ctx, w_ada, b_ada, g_norm1, g_norm2, w_in, g_mla_qa, w_mla_uq, g_mla_kva, w_mla_ukv, g_mla_q, g_mla_k, g_na_q, g_na_k, na_rpb, g_diff_q, g_diff_k, diff_lq1, diff_lk1, diff_lq2, diff_lk2, g_diff_sub, g_gqa_q, g_gqa_k, w_branch, w_out, w_up, w_down):
    p = dict(w_in=w_in, g_norm1=g_norm1, g_norm2=g_norm2, g_mla_qa=g_mla_qa, w_mla_uq=w_mla_uq,
             g_mla_kva=g_mla_kva, w_mla_ukv=w_mla_ukv, g_mla_q=g_mla_q, g_mla_k=g_mla_k,
             g_na_q=g_na_q, g_na_k=g_na_k, na_rpb=na_rpb, g_diff_q=g_diff_q, g_diff_k=g_diff_k,
             diff_lq1=diff_lq1, diff_lk1=diff_lk1, diff_lq2=diff_lq2, diff_lk2=diff_lk2,
             g_diff_sub=g_diff_sub, g_gqa_q=g_gqa_q, g_gqa_k=g_gqa_k, w_branch=w_branch,
             w_out=w_out, w_up=w_up, w_down=w_down)
    n_batch = x.shape[0]
    depth = w_ada.shape[0]
    assert x.shape[1:] == (SEQ, D) and ctx.shape[1:] == (CTX, D)

    mod_rows = -(-(n_batch + 1) // 8) * 8
    c_all = jnp.concatenate([c, c_ctx[None], jnp.zeros((mod_rows - n_batch - 1, D), F32)], axis=0)
    mods = _ada_call(c_all, w_ada, b_ada).reshape(depth, mod_rows, 1, 6 * D)
    lat_tiles = SEQ // TM
    lat_row = lambda i: i // lat_tiles
    ctx_row = lambda i: n_batch

    consts = dict(bd128=_block_diag_ones(256, 128), bd64=_block_diag_ones(256, 64),
                  bd32=_block_diag_ones(256, 32), bd64h=_block_diag_ones(128, 64))
    tabs = _rope_tables()
    stacked = _prepare_params(p)
    bounds = stacked.pop("bounds")

    xl = x.reshape(n_batch * SEQ, D)
    xc = ctx.reshape(n_batch * CTX, D)
    for l in range(depth):
        need_ctx = l < depth - 1
        lam_init = 0.8 - 0.6 * math.exp(-0.3 * l)
        lw = {name: _Layer(a, l) for name, a in stacked.items()}
        bound = lambda which: N_BOUNDS * l + which

        lat = _qkv_call(xl, mods, l, lambda b: b, lw, consts, tabs, n_batch, SEQ, TM)
        cx = _qkv_call(xc, mods, l, ctx_row, lw, consts, None, n_batch, CTX, CTX)
        mqt, mk, mvt, nqt, nk, nvt, dqt, dk, dvt, gqt, gk, gvt = lat
        oa = _latent_attn_call(_mla_attn_kernel, "mla_attn", bounds, bound(B_MLA), mqt, mk, mvt, cx[1], cx[2], [],
                               n_batch)
        ob = _na_attn_call(bounds, bound(B_NA), nqt, nk, nvt, cx[4], cx[5], lw["na_pair"], n_batch)
        oc = _latent_attn_call(functools.partial(_diff_attn_kernel, lam_init=lam_init), "diff_attn",
                               bounds, bound(B_DIFF), dqt, dk, dvt, cx[7], cx[8], [lw["lam"], lw["g_sub"]], n_batch)
        od = _latent_attn_call(_gqa_attn_kernel, "gqa_attn", bounds, bound(B_GQA), gqt, gk, gvt, cx[10], cx[11], [],
                               n_batch)
        if need_ctx:
            oc_all = _ctx_attn_call(cx, lw["lam"], lw["g_sub"], lam_init, n_batch)
            xc = _merge_call(xc, mods, l, ctx_row, lw["g1"], oc_all, lw["w_gate"], lw["w_branch"], lw["w_out"], TM)
            xc = _mlp_call(xc, mods, l, ctx_row, lw["g2"], lw["w_up"], lw["w_down"], TM)
        xl = _merge_call(xl, mods, l, lat_row, lw["g1"], (oa, ob, oc, od), lw["w_gate"], lw["w_branch"], lw["w_out"], TM)
        xl = _mlp_call(xl, mods, l, lat_row, lw["g2"], lw["w_up"], lw["w_down"], TM)
    return xl.reshape(n_batch, SEQ, D)
```

```python
import functools
import math
from typing import NamedTuple

import jax
import jax.numpy as jnp
from jax import lax
from jax.experimental import pallas as pl
from jax.experimental.pallas import tpu as pltpu

F32 = jnp.float32
BF16 = jnp.bfloat16

D = 1024
SEQ = 2048
CTX = 256
GRID_W = 64
N_ROWS = SEQ // GRID_W
ROPE_THETA = 10000.0
EPS = 1e-6
HEAD_DIM = 64
MLA_Q_LORA = 192
MLA_KV_LORA = 128
MLA_NOPE = 64
MLA_ROPE = 32
MLA_QK = MLA_NOPE + MLA_ROPE
NA_ROWS = 8
NA_COLS = 16
DIFF_DIM = 32
D_FF = 4 * D

LANES = 128
TM = 512
TQ = 2048
QC = 1024
NA_TQ = 256
NA_WIN = 3
MASK_VALUE = -1e30
LOG2E = math.log2(math.e)
MAX_FIXED_SHIFT = 50.0

C_CQ, C_CKV, C_KR, C_NA, C_DF, C_GQ, C_GATE = 0, 192, 320, 352, 1120, 1888, 2400
Z_CQ, Z_CKV, Z_KR, Z_NA, Z_DF, Z_GQ, Z_W = 0, 256, 384, 896, 1664, 2432, 2944
G_MQ, G_MK, G_NQ, G_NK, G_DQ, G_DK, G_GQ, G_GK = range(8)
T_MLA, T_R32, T_R64 = 0, 3, 6
B_MLA, B_NA, B_DIFF, B_GQA, N_BOUNDS = 0, 1, 2, 3, 4

VMEM_LIMIT = 56 * 1024 * 1024


def _cparams(n_axes):
    return pltpu.CompilerParams(dimension_semantics=("arbitrary",) * n_axes,
                                vmem_limit_bytes=VMEM_LIMIT)


def _dot(a, b):
    return jnp.dot(a, b, preferred_element_type=F32)


class _Layer(NamedTuple):
    stacked: jax.Array
    index: int


def _operand(a):
    return a.stacked if isinstance(a, _Layer) else a


def _const_spec(a):
    if isinstance(a, _Layer):
        shape = a.stacked.shape[1:]
        return pl.BlockSpec((None,) + shape, lambda *_: (a.index,) + (0,) * len(shape),
                            pipeline_mode=pl.Buffered(1))
    return pl.BlockSpec(a.shape, lambda *_: (0,) * a.ndim, pipeline_mode=pl.Buffered(1))


def _mod_spec(layer, row_fn):
    return pl.BlockSpec((None, 1, 1, 6 * D), lambda *idx: (layer, row_fn(*idx), 0, 0))


def _modnorm(x, g, shift, scale):
    y = x * lax.rsqrt(jnp.mean(x * x, axis=-1, keepdims=True) + EPS) * g
    return y * (1.0 + scale) + shift


def _ada_kernel(c_ref, w_ref, b_ref, o_ref):
    c = c_ref[...]
    s = c * jax.nn.sigmoid(c)
    o_ref[0] = _dot(s.astype(BF16), w_ref[0].astype(BF16)) + b_ref[0]


def _ada_call(c_all, w_ada, b_ada):
    n_layers = w_ada.shape[0]
    rows = c_all.shape[0]
    bn = 1536
    return pl.pallas_call(
        _ada_kernel,
        grid=(n_layers, 6 * D // bn),
        in_specs=[pl.BlockSpec((rows, D), lambda l, j: (0, 0)),
                  pl.BlockSpec((1, D, bn), lambda l, j: (l, 0, j)),
                  pl.BlockSpec((1, 1, bn), lambda l, j: (l, 0, j))],
        out_specs=pl.BlockSpec((1, rows, bn), lambda l, j: (l, 0, j)),
        out_shape=jax.ShapeDtypeStruct((n_layers, rows, 6 * D), F32),
        compiler_params=_cparams(2),
        name="ada",
    )(c_all, w_ada, b_ada.reshape(n_layers, 1, 6 * D))


def _seg_mean_sq(t, bd, inv_d):
    x2 = t * t
    hi = x2.astype(BF16)
    lo = (x2 - hi.astype(F32)).astype(BF16)
    return (_dot(hi, bd) + _dot(lo, bd)) * inv_d


def _qkv_kernel(*refs, rope):
    (x_ref, mod_ref, g1_ref, w_ref, bd128_ref, bd64_ref, bd32_ref, bd64h_ref,
     gqa_ref, gkva_ref, wuq_ref, wuk_ref, wuv_ref, gains_ref) = refs[:14]
    tab_ref = refs[14] if rope else None
    (mqt_ref, mk_ref, mvt_ref, nqt_ref, nk_ref, nvt_ref,
     dqt_ref, dk_ref, dvt_ref, gqt_ref, gk_ref, gvt_ref) = refs[-12:]

    mod = mod_ref[0]
    xn = _modnorm(x_ref[...], g1_ref[...], mod[:, 0:D], mod[:, D:2 * D]).astype(BF16)

    def cols(lo, hi):
        return _dot(xn, w_ref[:, lo:hi])

    def put(o_ref, lo, val):
        o_ref[:, lo:lo + LANES] = val.astype(BF16)

    def put_t(o_ref, lo, val):
        o_ref[0, lo:lo + LANES, :] = val.T.astype(BF16)

    def section(t, bd, inv_d, gain_row, tab, shift, o_ref, store):
        width = t.shape[1]
        y = t * lax.rsqrt(_seg_mean_sq(t, bd, inv_d) + EPS) * gains_ref[gain_row:gain_row + 1, 0:width]
        for j in range(width // LANES):
            yc = y[:, j * LANES:(j + 1) * LANES]
            if rope and tab is not None:
                yc = (yc * tab_ref[tab] + pltpu.roll(yc, LANES - shift, 1) * tab_ref[tab + 1]
                      + pltpu.roll(yc, shift, 1) * tab_ref[tab + 2])
            store(o_ref, j * LANES, yc)

    za = cols(Z_CQ, Z_NA)
    cq = za[:, Z_CQ:Z_CQ + 256]
    cqn = cq * lax.rsqrt(jnp.sum(cq * cq, axis=-1, keepdims=True) * (1.0 / MLA_Q_LORA) + EPS) * gqa_ref[...]
    q = _dot(cqn.astype(BF16), wuq_ref[...])
    ckv = za[:, Z_CKV:Z_CKV + 128]
    kvn = (ckv * lax.rsqrt(jnp.mean(ckv * ckv, axis=-1, keepdims=True) + EPS) * gkva_ref[...]).astype(BF16)
    k = _dot(kvn, wuk_ref[...]) + za[:, Z_KR:Z_KR + 512]
    mvt_ref[0] = _dot(kvn, wuv_ref[...]).T.astype(BF16)
    bd128 = bd128_ref[...]
    for c in range(2):
        sl = slice(256 * c, 256 * c + 256)
        section(q[:, sl], bd128, 1.0 / MLA_QK, G_MQ, T_MLA, MLA_ROPE // 4, mqt_ref.at[:, sl, :], put_t)
        section(k[:, sl], bd128, 1.0 / MLA_QK, G_MK, T_MLA, MLA_ROPE // 4, mk_ref.at[:, sl], put)

    zn = cols(Z_NA, Z_DF)
    bd64 = bd64_ref[...]
    section(zn[:, 0:256], bd64, 1.0 / HEAD_DIM, G_NQ, None, 0, nqt_ref, put_t)
    section(zn[:, 256:512], bd64, 1.0 / HEAD_DIM, G_NK, None, 0, nk_ref, put)
    nvt_ref[0] = zn[:, 512:768].T.astype(BF16)

    zd = cols(Z_DF, Z_GQ)
    bd32 = bd32_ref[...]
    section(zd[:, 0:256], bd32, 1.0 / DIFF_DIM, G_DQ, T_R32, DIFF_DIM // 4, dqt_ref, put_t)
    section(zd[:, 256:512], bd32, 1.0 / DIFF_DIM, G_DK, T_R32, DIFF_DIM // 4, dk_ref, put)
    dvt_ref[0] = zd[:, 512:768].T.astype(BF16)

    zg = cols(Z_GQ, Z_W)
    section(zg[:, 0:256], bd64, 1.0 / HEAD_DIM, G_GQ, T_R64, HEAD_DIM // 4, gqt_ref, put_t)
    section(zg[:, 256:384], bd64h_ref[...], 1.0 / HEAD_DIM, G_GK, T_R64, HEAD_DIM // 4, gk_ref, put)
    gvt_ref[0] = zg[:, 384:512].T.astype(BF16)


def _qkv_call(rows, mods, layer, mod_row, lw, consts, tabs, n_batch, seq, tm):
    tiles = seq // tm
    row_map = lambda t, b: (b * tiles + t, 0)
    tok_spec = lambda w: pl.BlockSpec((tm, w), row_map)
    tr_spec = lambda w: pl.BlockSpec((1, w, tm), lambda t, b: (b, 0, t))
    tok_shape = lambda w: jax.ShapeDtypeStruct((n_batch * seq, w), BF16)
    tr_shape = lambda w: jax.ShapeDtypeStruct((n_batch, w, seq), BF16)
    small = [lw["g1"], lw["w_small"], consts["bd128"], consts["bd64"], consts["bd32"], consts["bd64h"],
             lw["g_qa"], lw["g_kva"], lw["w_uq"], lw["w_uk"], lw["w_uv"], lw["gains"]]
    in_specs = [tok_spec(D), _mod_spec(layer, lambda t, b: mod_row(b))]
    in_specs += [_const_spec(a) for a in small]
    args = [rows, mods] + [_operand(a) for a in small]
    if tabs is not None:
        in_specs.append(pl.BlockSpec((tabs.shape[0], tm, LANES), lambda t, b: (0, t, 0)))
        args.append(tabs)
    return pl.pallas_call(
        functools.partial(_qkv_kernel, rope=tabs is not None),
        grid=(tiles, n_batch),
        in_specs=in_specs,
        out_specs=[tr_spec(512), tok_spec(512), tr_spec(256),
                   tr_spec(256), tok_spec(256), tr_spec(256),
                   tr_spec(256), tok_spec(256), tr_spec(256),
                   tr_spec(256), tok_spec(128), tr_spec(128)],
        out_shape=[tr_shape(512), tok_shape(512), tr_shape(256),
                   tr_shape(256), tok_shape(256), tr_shape(256),
                   tr_shape(256), tok_shape(256), tr_shape(256),
                   tr_shape(256), tok_shape(128), tr_shape(128)],
        compiler_params=_cparams(2),
        name="qkv",
    )(*args)


def _keep_rows(x, lo, hi):
    row = lax.broadcasted_iota(jnp.int32, x.shape, 0)
    return jnp.where(jnp.logical_and(row >= lo, row < hi), x, jnp.zeros_like(x))


def _sum_all(xs):
    return functools.reduce(lambda a, b: a + b, xs)


def _exp_scores(qt, groups, shift):
    s = [_dot(k, qt) if bias is None else _dot(k, qt) + bias for k, _, bias in groups]
    if shift is None:
        shift = functools.reduce(jnp.maximum, [jnp.max(x, axis=0, keepdims=True) for x in s])
    p = [jnp.exp2(x - shift) for x in s]
    return [x.astype(BF16) for x in p], _sum_all([jnp.sum(x, axis=0, keepdims=True) for x in p])


def _softmax_vt(qt, groups, shift):
    p, l = _exp_scores(qt, groups, shift)
    return _sum_all([_dot(g[1], x) for x, g in zip(p, groups)]) / l


def _mla_heads_t(qt_fn, groups_fn, shift):
    outs = []
    for h in range(4):
        c0 = 256 * (h // 2)
        lo = LANES * (h % 2)
        outs.append(_softmax_vt(_keep_rows(qt_fn(c0), lo, lo + LANES), groups_fn(h, c0), shift))
    return jnp.concatenate(outs, axis=0)


def _diff_heads_t(qt, groups_fn, lam, gsub_t, lam_init, shift):
    outs = []
    for h in range(4):
        groups = groups_fn(h)
        (p1, l1), (p2, l2) = [_exp_scores(_keep_rows(qt, 64 * h + DIFF_DIM * m, 64 * h + DIFF_DIM * (m + 1)),
                                          groups, shift) for m in range(2)]
        w1 = (1.0 / l1).astype(BF16)
        w2 = (lam / l2).astype(BF16)
        oh = _sum_all([_dot(g[1], a * w1 - b * w2) for a, b, g in zip(p1, p2, groups)])
        ms = jnp.mean(oh * oh, axis=0, keepdims=True)
        outs.append(oh * lax.rsqrt(ms + EPS) * gsub_t[64 * h:64 * h + 64, :])
    return jnp.concatenate(outs, axis=0) * (1.0 - lam_init)


def _gqa_heads_t(qt, groups_fn, shift):
    outs = []
    for g in range(2):
        qc = qt[LANES * g:LANES * (g + 1), :]
        for n in range(2):
            outs.append(_softmax_vt(_keep_rows(qc, 64 * n, 64 * n + 64), groups_fn(n), shift))
    return jnp.concatenate(outs, axis=0)


def _plain_heads_t(qt, groups_fn, shift):
    return jnp.concatenate([_softmax_vt(_keep_rows(qt, 64 * h, 64 * h + 64), groups_fn(h), shift) for h in range(4)],
                           axis=0)


def _diff_lambda(lam_ref, lam_init):
    lq1, lk1, lq2, lk2 = (lam_ref[i:i + 1, :] for i in range(4))
    return (jnp.exp(jnp.sum(lq1 * lk1, axis=-1, keepdims=True))
            - jnp.exp(jnp.sum(lq2 * lk2, axis=-1, keepdims=True)) + lam_init)


def _latent_attention(shift_ref, bound_index, chunk_fn):
    bound = shift_ref[bound_index]

    def run(shift):
        def step(c, carry):
            chunk_fn(pl.ds(pl.multiple_of(c * QC, QC), QC), shift)
            return carry

        lax.fori_loop(0, TQ // QC, step, 0)

    @pl.when(bound <= MAX_FIXED_SHIFT)
    def _():
        run(bound)

    @pl.when(jnp.logical_not(bound <= MAX_FIXED_SHIFT))
    def _():
        run(None)


def _mla_attn_kernel(shift_ref, qt_ref, kl_ref, vtl_ref, kc_ref, vtc_ref, o_ref, *, bound_index):
    def chunk(cols, shift):
        def groups_fn(h, c0):
            dims = slice(64 * h, 64 * h + 64)
            return [(kl_ref[:, c0:c0 + 256], vtl_ref[0, dims, :], None),
                    (kc_ref[:, c0:c0 + 256], vtc_ref[0, dims, :], None)]

        ot = _mla_heads_t(lambda c0: qt_ref[0, c0:c0 + 256, cols], groups_fn, shift)
        o_ref[cols, :] = ot.T.astype(BF16)

    _latent_attention(shift_ref, bound_index, chunk)


def _diff_attn_kernel(shift_ref, qt_ref, kl_ref, vtl_ref, kc_ref, vtc_ref, lam_ref, gsubt_ref, o_ref, *, lam_init,
                      bound_index):
    def chunk(cols, shift):
        def groups_fn(h):
            dims = slice(64 * h, 64 * h + 64)
            return [(kl_ref[...], vtl_ref[0, dims, :], None), (kc_ref[...], vtc_ref[0, dims, :], None)]

        lam = _diff_lambda(lam_ref, lam_init)
        ot = _diff_heads_t(qt_ref[0, :, cols], groups_fn, lam, gsubt_ref[...], lam_init, shift)
        o_ref[cols, :] = ot.T.astype(BF16)

    _latent_attention(shift_ref, bound_index, chunk)


def _gqa_attn_kernel(shift_ref, qt_ref, kl_ref, vtl_ref, kc_ref, vtc_ref, o_ref, *, bound_index):
    def chunk(cols, shift):
        def groups_fn(n):
            dims = slice(64 * n, 64 * n + 64)
            return [(kl_ref[...], vtl_ref[0, dims, :], None), (kc_ref[...], vtc_ref[0, dims, :], None)]

        ot = _gqa_heads_t(qt_ref[0, :, cols], groups_fn, shift)
        o_ref[cols, :] = ot.T.astype(BF16)

    _latent_attention(shift_ref, bound_index, chunk)


def _latent_attn_call(kernel, name, bounds, bound_index, qt, kl, vtl, kc, vtc, extra, n_batch):
    wq, wk, wv = qt.shape[1], kl.shape[1], vtl.shape[1]
    tiles = SEQ // TQ
    return pl.pallas_call(
        functools.partial(kernel, bound_index=bound_index),
        grid=(n_batch, tiles),
        in_specs=[pl.BlockSpec(memory_space=pltpu.SMEM),
                  pl.BlockSpec((1, wq, TQ), lambda b, t: (b, 0, t)),
                  pl.BlockSpec((SEQ, wk), lambda b, t: (b, 0)),
                  pl.BlockSpec((1, wv, SEQ), lambda b, t: (b, 0, 0)),
                  pl.BlockSpec((CTX, wk), lambda b, t: (b, 0)),
                  pl.BlockSpec((1, wv, CTX), lambda b, t: (b, 0, 0))] + [_const_spec(a) for a in extra],
        out_specs=pl.BlockSpec((TQ, 256), lambda b, t: (b * tiles + t, 0)),
        out_shape=jax.ShapeDtypeStruct((n_batch * SEQ, 256), BF16),
        compiler_params=_cparams(2),
        name=name,
    )(bounds, qt, kl, vtl, kc, vtc, *map(_operand, extra))


def _na_window(t):
    return jnp.clip(t - 1, 0, SEQ // NA_TQ - NA_WIN)


def _na_build_bias(t, pair_ref, bias_ref):
    rows_per_tile = NA_TQ // GRID_W
    k_row0 = rows_per_tile * _na_window(t)
    lane = lax.broadcasted_iota(jnp.int32, (GRID_W, LANES), 1)
    for i in range(rows_per_tile // 2):
        qr = rows_per_tile * t + 2 * i
        r0 = jnp.clip(qr - NA_ROWS // 2, 0, N_ROWS - NA_ROWS)
        r1 = jnp.clip(qr + 1 - NA_ROWS // 2, 0, N_ROWS - NA_ROWS)
        for kl in range(NA_WIN * rows_per_tile):
            kr = k_row0 + kl
            ok0 = jnp.logical_and(kr >= r0, kr < r0 + NA_ROWS).astype(jnp.int32)
            ok1 = jnp.logical_and(kr >= r1, kr < r1 + NA_ROWS).astype(jnp.int32)
            ok = jnp.where(lane < GRID_W, ok0, ok1) > 0
            d = jnp.clip(kr - qr + NA_ROWS - 1, 0, 2 * NA_ROWS - 1)
            for h in range(4):
                blk = jnp.where(ok, pair_ref[h, d], MASK_VALUE)
                bias_ref[h, GRID_W * kl:GRID_W * (kl + 1), LANES * i:LANES * (i + 1)] = blk


def _na_attn_kernel(shift_ref, qt_ref, k0_ref, k1_ref, k2_ref, kc_ref, vt0_ref, vt1_ref, vt2_ref, vtc_ref, pair_ref,
                    o_ref, bias_ref, *, bound_index):
    t = pl.program_id(0)

    @pl.when(pl.program_id(1) == 0)
    def _():
        _na_build_bias(t, pair_ref, bias_ref)

    win = ((k0_ref, vt0_ref), (k1_ref, vt1_ref), (k2_ref, vt2_ref))

    def groups_fn(h):
        dims = slice(64 * h, 64 * h + 64)
        return ([(k[...], vt[0, dims, :], bias_ref[h, NA_TQ * j:NA_TQ * (j + 1), :]) for j, (k, vt) in enumerate(win)]
                + [(kc_ref[...], vtc_ref[0, dims, :], None)])

    bound = shift_ref[bound_index]

    @pl.when(bound <= MAX_FIXED_SHIFT)
    def _():
        o_ref[...] = _plain_heads_t(qt_ref[0], groups_fn, bound).T.astype(BF16)

    @pl.when(jnp.logical_not(bound <= MAX_FIXED_SHIFT))
    def _():
        o_ref[...] = _plain_heads_t(qt_ref[0], groups_fn, None).T.astype(BF16)


def _na_attn_call(bounds, bound_index, qt, kl, vtl, kc, vtc, pair, n_batch):
    tiles = SEQ // NA_TQ
    k_spec = lambda j: pl.BlockSpec((NA_TQ, 256), lambda t, b: (b * tiles + _na_window(t) + j, 0))
    vt_spec = lambda j: pl.BlockSpec((1, 256, NA_TQ), lambda t, b: (b, 0, _na_window(t) + j))
    return pl.pallas_call(
        functools.partial(_na_attn_kernel, bound_index=bound_index),
        grid=(tiles, n_batch),
        in_specs=[pl.BlockSpec(memory_space=pltpu.SMEM),
                  pl.BlockSpec((1, 256, NA_TQ), lambda t, b: (b, 0, t)),
                  k_spec(0), k_spec(1), k_spec(2),
                  pl.BlockSpec((CTX, 256), lambda t, b: (b, 0)),
                  vt_spec(0), vt_spec(1), vt_spec(2),
                  pl.BlockSpec((1, 256, CTX), lambda t, b: (b, 0, 0)),
                  _const_spec(pair)],
        out_specs=pl.BlockSpec((NA_TQ, 256), lambda t, b: (b * tiles + t, 0)),
        out_shape=jax.ShapeDtypeStruct((n_batch * SEQ, 256), BF16),
        scratch_shapes=[pltpu.VMEM((4, NA_WIN * NA_TQ, NA_TQ), F32)],
        compiler_params=_cparams(2),
        name="na_attn",
    )(bounds, qt, kl, kl, kl, kc, vtl, vtl, vtl, vtc, _operand(pair))


def _ctx_attn_kernel(mqt_ref, mk_ref, mvt_ref, nqt_ref, nk_ref, nvt_ref, dqt_ref, dk_ref, dvt_ref,
                     gqt_ref, gk_ref, gvt_ref, lam_ref, gsubt_ref, oa_ref, ob_ref, oc_ref, od_ref, *, lam_init):
    head_dims = lambda h: slice(64 * h, 64 * h + 64)
    oa_ref[...] = _mla_heads_t(lambda c0: mqt_ref[0, c0:c0 + 256, :],
                               lambda h, c0: [(mk_ref[:, c0:c0 + 256], mvt_ref[0, head_dims(h), :], None)],
                               None).T.astype(BF16)
    ob_ref[...] = _plain_heads_t(nqt_ref[0], lambda h: [(nk_ref[...], nvt_ref[0, head_dims(h), :], None)],
                                 None).T.astype(BF16)
    lam = _diff_lambda(lam_ref, lam_init)
    oc_ref[...] = _diff_heads_t(dqt_ref[0], lambda h: [(dk_ref[...], dvt_ref[0, head_dims(h), :], None)], lam,
                                gsubt_ref[...], lam_init, None).T.astype(BF16)
    od_ref[...] = _gqa_heads_t(gqt_ref[0], lambda n: [(gk_ref[...], gvt_ref[0, head_dims(n), :], None)],
                               None).T.astype(BF16)


def _ctx_attn_call(qkv_c, lam, gsub, lam_init, n_batch):
    specs = []
    for a in qkv_c:
        if a.ndim == 3:
            specs.append(pl.BlockSpec((1, a.shape[1], CTX), lambda b: (b, 0, 0)))
        else:
            specs.append(pl.BlockSpec((CTX, a.shape[1]), lambda b: (b, 0)))
    out = jax.ShapeDtypeStruct((n_batch * CTX, 256), BF16)
    return pl.pallas_call(
        functools.partial(_ctx_attn_kernel, lam_init=lam_init),
        grid=(n_batch,),
        in_specs=specs + [_const_spec(lam), _const_spec(gsub)],
        out_specs=[pl.BlockSpec((CTX, 256), lambda b: (b, 0))] * 4,
        out_shape=[out] * 4,
        compiler_params=_cparams(1),
        name="ctx_attn",
    )(*qkv_c, _operand(lam), _operand(gsub))


def _merge_kernel(x_ref, mod_ref, g_ref, oa_ref, ob_ref, oc_ref, od_ref, wg_ref, wb_ref, wo_ref, out_ref):
    x = x_ref[...]
    mod = mod_ref[0]
    xn = _modnorm(x, g_ref[...], mod[:, 0:D], mod[:, D:2 * D]).astype(BF16)
    acc = jnp.zeros(x.shape, F32)
    for n, o_ref in enumerate((oa_ref, ob_ref, oc_ref, od_ref)):
        gate = jax.nn.sigmoid(_dot(xn, wg_ref[:, n * D:(n + 1) * D]))
        acc = acc + gate * _dot(o_ref[...], wb_ref[n])
    y = _dot(acc.astype(BF16), wo_ref[...])
    out_ref[...] = x + mod[:, 2 * D:3 * D] * y


def _merge_call(rows, mods, layer, mod_row, g1, branches, w_gate, w_branch, w_out, tm):
    n_rows = rows.shape[0]
    tm = min(tm, n_rows)
    tok = lambda w: pl.BlockSpec((tm, w), lambda i: (i, 0))
    return pl.pallas_call(
        _merge_kernel,
        grid=(n_rows // tm,),
        in_specs=[tok(D), _mod_spec(layer, mod_row), _const_spec(g1)]
                 + [tok(256)] * 4 + [_const_spec(w_gate), _const_spec(w_branch), _const_spec(w_out)],
        out_specs=tok(D),
        out_shape=jax.ShapeDtypeStruct((n_rows, D), F32),
        compiler_params=_cparams(1),
        name="merge",
    )(rows, mods, _operand(g1), *branches, _operand(w_gate), _operand(w_branch), _operand(w_out))


def _mlp_kernel(x_ref, mod_ref, g_ref, wu_ref, wd_ref, out_ref):
    x = x_ref[...]
    mod = mod_ref[0]
    xn = _modnorm(x, g_ref[...], mod[:, 3 * D:4 * D], mod[:, 4 * D:5 * D]).astype(BF16)
    h = jnp.square(jnp.maximum(_dot(xn, wu_ref[...]), 0.0))
    out_ref[...] = x + mod[:, 5 * D:6 * D] * _dot(h.astype(BF16), wd_ref[...])


def _mlp_call(rows, mods, layer, mod_row, g2, w_up, w_down, tm):
    n_rows = rows.shape[0]
    tm = min(tm, n_rows)
    tok = lambda w: pl.BlockSpec((tm, w), lambda i: (i, 0))
    return pl.pallas_call(
        _mlp_kernel,
        grid=(n_rows // tm,),
        in_specs=[tok(D), _mod_spec(layer, mod_row), _const_spec(g2),
                  _const_spec(w_up), _const_spec(w_down)],
        out_specs=tok(D),
        out_shape=jax.ShapeDtypeStruct((n_rows, D), F32),
        compiler_params=_cparams(1),
        name="mlp",
    )(rows, mods, _operand(g2), _operand(w_up), _operand(w_down))


def _block_diag_ones(n, seg):
    i = jnp.arange(n) // seg
    return (i[:, None] == i[None, :]).astype(BF16)


def _rope_parts(rot_dim):
    t = jnp.arange(SEQ)
    rows, cols = t // GRID_W, t % GRID_W
    n = rot_dim // 4
    inv_freq = jnp.power(ROPE_THETA, -jnp.arange(n, dtype=F32) / n)
    ang_r = rows.astype(F32)[:, None] * inv_freq
    ang_c = cols.astype(F32)[:, None] * inv_freq
    ang = jnp.concatenate([ang_r, ang_r, ang_c, ang_c], axis=-1)
    cos, sin = jnp.cos(ang), jnp.sin(ang)
    even = (jnp.arange(rot_dim) // n) % 2 == 0
    return cos, jnp.where(even, -sin, 0.0), jnp.where(even, 0.0, sin)


def _rope_tables():
    r32, r64 = _rope_parts(MLA_ROPE), _rope_parts(HEAD_DIM)
    ones = jnp.ones((SEQ, MLA_NOPE), F32)
    zeros_n = jnp.zeros((SEQ, MLA_NOPE), F32)
    pad = jnp.zeros((SEQ, LANES - MLA_QK), F32)
    mla = [jnp.concatenate([lead, part, pad], axis=1) for lead, part in zip((ones, zeros_n, zeros_n), r32)]
    return jnp.stack(mla + [jnp.tile(a, (1, LANES // MLA_ROPE)) for a in r32]
                     + [jnp.tile(a, (1, LANES // HEAD_DIM)) for a in r64])


def _na_pair_table(rpb):
    kc = jnp.arange(GRID_W)[:, None]
    qc = jnp.arange(GRID_W)[None, :]
    onehot = (kc - qc + NA_COLS - 1)[None] == jnp.arange(2 * NA_COLS - 1)[:, None, None]
    toeplitz = jnp.einsum("lhrd,dkq->lhrkq", rpb, onehot.astype(F32), precision=lax.Precision.HIGHEST)
    c0 = jnp.clip(qc - NA_COLS // 2, 0, GRID_W - NA_COLS)
    in_window = (kc >= c0) & (kc < c0 + NA_COLS)
    masked = jnp.where(in_window, toeplitz * LOG2E, MASK_VALUE)
    ext = jnp.pad(masked, ((0, 0), (0, 0), (1, 1), (0, 0), (0, 0)), constant_values=MASK_VALUE)
    return jnp.concatenate([ext[:, :, 1:], ext[:, :, :-1]], axis=-1)


def _prepare_params(p):
    w_in = p["w_in"]
    n_layers = w_in.shape[0]
    zcol = lambda n: jnp.zeros((n_layers, D, n), F32)
    kr = w_in[:, :, C_KR:C_KR + MLA_ROPE]
    kr_slots = jnp.concatenate([zcol(MLA_NOPE), kr, zcol(LANES - MLA_QK)] * 4, axis=2)
    head_order = jnp.array([0, 2, 1, 3])
    gq_cols = w_in[:, :, C_GQ:C_GQ + 256].reshape(n_layers, D, 4, HEAD_DIM)[:, :, head_order].reshape(n_layers, D, 256)
    w_small = jnp.concatenate([
        w_in[:, :, C_CQ:C_CQ + MLA_Q_LORA], zcol(256 - MLA_Q_LORA),
        w_in[:, :, C_CKV:C_CKV + MLA_KV_LORA],
        kr_slots,
        w_in[:, :, C_NA:C_NA + 768],
        w_in[:, :, C_DF:C_DF + 768],
        gq_cols, w_in[:, :, C_GQ + 256:C_GQ + 512]], axis=2).astype(BF16)

    w_uq = p["w_mla_uq"].reshape(n_layers, MLA_Q_LORA, 4, MLA_QK)
    w_uq = jnp.pad(w_uq, ((0, 0), (0, 256 - MLA_Q_LORA), (0, 0), (0, LANES - MLA_QK)))
    w_ukv = p["w_mla_ukv"].reshape(n_layers, MLA_KV_LORA, 4, 2 * MLA_NOPE)
    w_uk = jnp.pad(w_ukv[..., :MLA_NOPE], ((0, 0), (0, 0), (0, 0), (0, LANES - MLA_NOPE)))

    mla_gain = lambda g: jnp.tile(jnp.pad(g, ((0, 0), (0, LANES - MLA_QK))), (1, 4))
    rep = lambda g, n: jnp.tile(g, (1, n))
    row = lambda g: jnp.pad(g, ((0, 0), (0, 512 - g.shape[1])))
    hd_scale = HEAD_DIM ** -0.5 * LOG2E
    gains = jnp.stack([
        row(mla_gain(p["g_mla_q"]) * (MLA_QK ** -0.5 * LOG2E)), row(mla_gain(p["g_mla_k"])),
        row(rep(p["g_na_q"], 4) * hd_scale), row(rep(p["g_na_k"], 4)),
        row(rep(p["g_diff_q"], 8) * (DIFF_DIM ** -0.5 * LOG2E)), row(rep(p["g_diff_k"], 8)),
        row(rep(p["g_gqa_q"], 4) * hd_scale), row(rep(p["g_gqa_k"], 2))], axis=1)

    def score_bound(dim, g_q, g_k):
        return LOG2E * dim ** 0.5 * jnp.max(jnp.abs(g_q), axis=-1) * jnp.max(jnp.abs(g_k), axis=-1)

    bounds = jnp.stack([
        score_bound(MLA_QK, p["g_mla_q"], p["g_mla_k"]),
        score_bound(HEAD_DIM, p["g_na_q"], p["g_na_k"]) + LOG2E * jnp.max(jnp.abs(p["na_rpb"]), axis=(1, 2, 3)),
        score_bound(DIFF_DIM, p["g_diff_q"], p["g_diff_k"]),
        score_bound(HEAD_DIM, p["g_gqa_q"], p["g_gqa_k"])], axis=1).astype(F32).reshape(n_layers * N_BOUNDS)

    w_branch = p["w_branch"]
    wb_gqa = w_branch[:, 3].reshape(n_layers, 4, HEAD_DIM, D)[:, head_order].reshape(n_layers, 1, 256, D)
    w_branch = jnp.concatenate([w_branch[:, :3], wb_gqa], axis=1).astype(BF16)

    return dict(
        g1=p["g_norm1"][:, None], g2=p["g_norm2"][:, None],
        w_small=w_small, w_gate=w_in[:, :, C_GATE:].astype(BF16),
        g_qa=jnp.pad(p["g_mla_qa"], ((0, 0), (0, 256 - MLA_Q_LORA)))[:, None], g_kva=p["g_mla_kva"][:, None],
        w_uq=w_uq.reshape(n_layers, 256, 512).astype(BF16), w_uk=w_uk.reshape(n_layers, MLA_KV_LORA, 512).astype(BF16),
        w_uv=w_ukv[..., MLA_NOPE:].reshape(n_layers, MLA_KV_LORA, 256).astype(BF16),
        gains=gains, bounds=bounds, na_pair=_na_pair_table(p["na_rpb"]),
        lam=jnp.stack([p["diff_lq1"], p["diff_lk1"], p["diff_lq2"], p["diff_lk2"]], axis=1),
        g_sub=rep(p["g_diff_sub"], 4)[:, :, None],
        w_branch=w_branch, w_out=p["w_out"].astype(BF16),
        w_up=p["w_up"].astype(BF16), w_down=p["w_down"].astype(BF16),
    )


def kernel(x, c, ctx, c_ctx, w_ada, b_ada, g_norm1, g_norm2, w_in, g_mla_qa, w_mla_uq, g_mla_kva, w_mla_ukv, g_mla_q, g_mla_k, g_na_q, g_na_k, na_rpb, g_diff_q, g_diff_k, diff_lq1, diff_lk1, diff_lq2, diff_lk2, g_diff_sub, g_gqa_q, g_gqa_k, w_branch, w_out, w_up, w_down):
    p = dict(w_in=w_in, g_norm1=g_norm1, g_norm2=g_norm2, g_mla_qa=g_mla_qa, w_mla_uq=w_mla_uq,
             g_mla_kva=g_mla_kva, w_mla_ukv=w_mla_ukv, g_mla_q=g_mla_q, g_mla_k=g_mla_k,
             g_na_q=g_na_q, g_na_k=g_na_k, na_rpb=na_rpb, g_diff_q=g_diff_q, g_diff_k=g_diff_k,
             diff_lq1=diff_lq1, diff_lk1=diff_lk1, diff_lq2=diff_lq2, diff_lk2=diff_lk2,
             g_diff_sub=g_diff_sub, g_gqa_q=g_gqa_q, g_gqa_k=g_gqa_k, w_branch=w_branch,
             w_out=w_out, w_up=w_up, w_down=w_down)
    n_batch = x.shape[0]
    depth = w_ada.shape[0]
    assert x.shape[1:] == (SEQ, D) and ctx.shape[1:] == (CTX, D)

    mod_rows = -(-(n_batch + 1) // 8) * 8
    c_all = jnp.concatenate([c, c_ctx[None], jnp.zeros((mod_rows - n_batch - 1, D), F32)], axis=0)
    mods = _ada_call(c_all, w_ada, b_ada).reshape(depth, mod_rows, 1, 6 * D)
    lat_tiles = SEQ // TM
    lat_row = lambda i: i // lat_tiles
    ctx_row = lambda i: n_batch

    consts = dict(bd128=_block_diag_ones(256, 128), bd64=_block_diag_ones(256, 64),
                  bd32=_block_diag_ones(256, 32), bd64h=_block_diag_ones(128, 64))
    tabs = _rope_tables()
    stacked = _prepare_params(p)
    bounds = stacked.pop("bounds")

    xl = x.reshape(n_batch * SEQ, D)
    xc = ctx.reshape(n_batch * CTX, D)
    for l in range(depth):
        need_ctx = l < depth - 1
        lam_init = 0.8 - 0.6 * math.exp(-0.3 * l)
        lw = {name: _Layer(a, l) for name, a in stacked.items()}
        bound = lambda which: N_BOUNDS * l + which

        lat = _qkv_call(xl, mods, l, lambda b: b, lw, consts, tabs, n_batch, SEQ, TM)
        cx = _qkv_call(xc, mods, l, ctx_row, lw, consts, None, n_batch, CTX, CTX)
        mqt, mk, mvt, nqt, nk, nvt, dqt, dk, dvt, gqt, gk, gvt = lat
        oa = _latent_attn_call(_mla_attn_kernel, "mla_attn", bounds, bound(B_MLA), mqt, mk, mvt, cx[1], cx[2], [],
                               n_batch)
        ob = _na_attn_call(bounds, bound(B_NA), nqt, nk, nvt, cx[4], cx[5], lw["na_pair"], n_batch)
        oc = _latent_attn_call(functools.partial(_diff_attn_kernel, lam_init=lam_init), "diff_attn",
                               bounds, bound(B_DIFF), dqt, dk, dvt, cx[7], cx[8], [lw["lam"], lw["g_sub"]], n_batch)
        od = _latent_attn_call(_gqa_attn_kernel, "gqa_attn", bounds, bound(B_GQA), gqt, gk, gvt, cx[10], cx[11], [],
                               n_batch)
        if need_ctx:
            oc_all = _ctx_attn_call(cx, lw["lam"], lw["g_sub"], lam_init, n_batch)
            xc = _merge_call(xc, mods, l, ctx_row, lw["g1"], oc_all, lw["w_gate"], lw["w_branch"], lw["w_out"], TM)
            xc = _mlp_call(xc, mods, l, ctx_row, lw["g2"], lw["w_up"], lw["w_down"], TM)
        xl = _merge_call(xl, mods, l, lat_row, lw["g1"], (oa, ob, oc, od), lw["w_gate"], lw["w_branch"], lw["w_out"], TM)
        xl = _mlp_call(xl, mods, l, lat_row, lw["g2"], lw["w_up"], lw["w_down"], TM)
    return xl.reshape(n_batch, SEQ, D)
```

```python
import functools
import math
from typing import NamedTuple

import jax
import jax.numpy as jnp
from jax import lax
from jax.experimental import pallas as pl
from jax.experimental.pallas import tpu as pltpu

F32 = jnp.float32
BF16 = jnp.bfloat16

D = 1024
SEQ = 2048
CTX = 256
GRID_W = 64
N_ROWS = SEQ // GRID_W
ROPE_THETA = 10000.0
EPS = 1e-6
HEAD_DIM = 64
MLA_Q_LORA = 192
MLA_KV_LORA = 128
MLA_NOPE = 64
MLA_ROPE = 32
MLA_QK = MLA_NOPE + MLA_ROPE
NA_ROWS = 8
NA_COLS = 16
DIFF_DIM = 32
D_FF = 4 * D

LANES = 128
TM = 512
TQ = 2048
QC = 1024
NA_TQ = 256
NA_WIN = 3
MASK_VALUE = -1e30
LOG2E = math.log2(math.e)
MAX_FIXED_SHIFT = 50.0

C_CQ, C_CKV, C_KR, C_NA, C_DF, C_GQ, C_GATE = 0, 192, 320, 352, 1120, 1888, 2400
Z_CQ, Z_CKV, Z_KR, Z_NA, Z_DF, Z_GQ, Z_W = 0, 256, 384, 896, 1664, 2432, 2944
G_MQ, G_MK, G_NQ, G_NK, G_DQ, G_DK, G_GQ, G_GK = range(8)
T_MLA, T_R32, T_R64 = 0, 3, 6
B_MLA, B_NA, B_DIFF, B_GQA, N_BOUNDS = 0, 1, 2, 3, 4

VMEM_LIMIT = 56 * 1024 * 1024


def _cparams(n_axes):
    return pltpu.CompilerParams(dimension_semantics=("arbitrary",) * n_axes,
                                vmem_limit_bytes=VMEM_LIMIT)


def _dot(a, b):
    return jnp.dot(a, b, preferred_element_type=F32)


class _Layer(NamedTuple):
    stacked: jax.Array
    index: int


def _operand(a):
    return a.stacked if isinstance(a, _Layer) else a


def _const_spec(a):
    if isinstance(a, _Layer):
        shape = a.stacked.shape[1:]
        return pl.BlockSpec((None,) + shape, lambda *_: (a.index,) + (0,) * len(shape),
                            pipeline_mode=pl.Buffered(1))
    return pl.BlockSpec(a.shape, lambda *_: (0,) * a.ndim, pipeline_mode=pl.Buffered(1))


def _mod_spec(layer, row_fn):
    return pl.BlockSpec((None, 1, 1, 6 * D), lambda *idx: (layer, row_fn(*idx), 0, 0))


def _modnorm(x, g, shift, scale):
    y = x * lax.rsqrt(jnp.mean(x * x, axis=-1, keepdims=True) + EPS) * g
    return y * (1.0 + scale) + shift


def _ada_kernel(c_ref, w_ref, b_ref, o_ref):
    c = c_ref[...]
    s = c * jax.nn.sigmoid(c)
    o_ref[0] = _dot(s.astype(BF16), w_ref[0].astype(BF16)) + b_ref[0]


def _ada_call(c_all, w_ada, b_ada):
    n_layers = w_ada.shape[0]
    rows = c_all.shape[0]
    bn = 1536
    return pl.pallas_call(
        _ada_kernel,
        grid=(n_layers, 6 * D // bn),
        in_specs=[pl.BlockSpec((rows, D), lambda l, j: (0, 0)),
                  pl.BlockSpec((1, D, bn), lambda l, j: (l, 0, j)),
                  pl.BlockSpec((1, 1, bn), lambda l, j: (l, 0, j))],
        out_specs=pl.BlockSpec((1, rows, bn), lambda l, j: (l, 0, j)),
        out_shape=jax.ShapeDtypeStruct((n_layers, rows, 6 * D), F32),
        compiler_params=_cparams(2),
        name="ada",
    )(c_all, w_ada, b_ada.reshape(n_layers, 1, 6 * D))


def _seg_mean_sq(t, bd, inv_d):
    x2 = t * t
    hi = x2.astype(BF16)
    lo = (x2 - hi.astype(F32)).astype(BF16)
    return (_dot(hi, bd) + _dot(lo, bd)) * inv_d


def _qkv_kernel(*refs, rope):
    (x_ref, mod_ref, g1_ref, w_ref, bd128_ref, bd64_ref, bd32_ref, bd64h_ref,
     gqa_ref, gkva_ref, wuq_ref, wuk_ref, wuv_ref, gains_ref) = refs[:14]
    tab_ref = refs[14] if rope else None
    (mqt_ref, mk_ref, mvt_ref, nqt_ref, nk_ref, nvt_ref,
     dqt_ref, dk_ref, dvt_ref, gqt_ref, gk_ref, gvt_ref) = refs[-12:]

    mod = mod_ref[0]
    xn = _modnorm(x_ref[...], g1_ref[...], mod[:, 0:D], mod[:, D:2 * D]).astype(BF16)

    def cols(lo, hi):
        return _dot(xn, w_ref[:, lo:hi])

    def put(o_ref, lo, val):
        o_ref[:, lo:lo + LANES] = val.astype(BF16)

    def put_t(o_ref, lo, val):
        o_ref[0, lo:lo + LANES, :] = val.T.astype(BF16)

    def section(t, bd, inv_d, gain_row, tab, shift, o_ref, store):
        width = t.shape[1]
        y = t * lax.rsqrt(_seg_mean_sq(t, bd, inv_d) + EPS) * gains_ref[gain_row:gain_row + 1, 0:width]
        for j in range(width // LANES):
            yc = y[:, j * LANES:(j + 1) * LANES]
            if rope and tab is not None:
                yc = (yc * tab_ref[tab] + pltpu.roll(yc, LANES - shift, 1) * tab_ref[tab + 1]
                      + pltpu.roll(yc, shift, 1) * tab_ref[tab + 2])
            store(o_ref, j * LANES, yc)

    za = cols(Z_CQ, Z_NA)
    cq = za[:, Z_CQ:Z_CQ + 256]
    cqn = cq * lax.rsqrt(jnp.sum(cq * cq, axis=-1, keepdims=True) * (1.0 / MLA_Q_LORA) + EPS) * gqa_ref[...]
    q = _dot(cqn.astype(BF16), wuq_ref[...])
    ckv = za[:, Z_CKV:Z_CKV + 128]
    kvn = (ckv * lax.rsqrt(jnp.mean(ckv * ckv, axis=-1, keepdims=True) + EPS) * gkva_ref[...]).astype(BF16)
    k = _dot(kvn, wuk_ref[...]) + za[:, Z_KR:Z_KR + 512]
    mvt_ref[0] = _dot(kvn, wuv_ref[...]).T.astype(BF16)
    bd128 = bd128_ref[...]
    for c in range(2):
        sl = slice(256 * c, 256 * c + 256)
        section(q[:, sl], bd128, 1.0 / MLA_QK, G_MQ, T_MLA, MLA_ROPE // 4, mqt_ref.at[:, sl, :], put_t)
        section(k[:, sl], bd128, 1.0 / MLA_QK, G_MK, T_MLA, MLA_ROPE // 4, mk_ref.at[:, sl], put)

    zn = cols(Z_NA, Z_DF)
    bd64 = bd64_ref[...]
    section(zn[:, 0:256], bd64, 1.0 / HEAD_DIM, G_NQ, None, 0, nqt_ref, put_t)
    section(zn[:, 256:512], bd64, 1.0 / HEAD_DIM, G_NK, None, 0, nk_ref, put)
    nvt_ref[0] = zn[:, 512:768].T.astype(BF16)

    zd = cols(Z_DF, Z_GQ)
    bd32 = bd32_ref[...]
    section(zd[:, 0:256], bd32, 1.0 / DIFF_DIM, G_DQ, T_R32, DIFF_DIM // 4, dqt_ref, put_t)
    section(zd[:, 256:512], bd32, 1.0 / DIFF_DIM, G_DK, T_R32, DIFF_DIM // 4, dk_ref, put)
    dvt_ref[0] = zd[:, 512:768].T.astype(BF16)

    zg = cols(Z_GQ, Z_W)
    section(zg[:, 0:256], bd64, 1.0 / HEAD_DIM, G_GQ, T_R64, HEAD_DIM // 4, gqt_ref, put_t)
    section(zg[:, 256:384], bd64h_ref[...], 1.0 / HEAD_DIM, G_GK, T_R64, HEAD_DIM // 4, gk_ref, put)
    gvt_ref[0] = zg[:, 384:512].T.astype(BF16)


def _qkv_call(rows, mods, layer, mod_row, lw, consts, tabs, n_batch, seq, tm):
    tiles = seq // tm
    row_map = lambda t, b: (b * tiles + t, 0)
    tok_spec = lambda w: pl.BlockSpec((tm, w), row_map)
    tr_spec = lambda w: pl.BlockSpec((1, w, tm), lambda t, b: (b, 0, t))
    tok_shape = lambda w: jax.ShapeDtypeStruct((n_batch * seq, w), BF16)
    tr_shape = lambda w: jax.ShapeDtypeStruct((n_batch, w, seq), BF16)
    small = [lw["g1"], lw["w_proj"], consts["bd128"], consts["bd64"], consts["bd32"], consts["bd64h"],
             lw["g_qa"], lw["g_kva"], lw["w_uq"], lw["w_uk"], lw["w_uv"], lw["gains"]]
    in_specs = [tok_spec(D), _mod_spec(layer, lambda t, b: mod_row(b))]
    in_specs += [_const_spec(a) for a in small]
    args = [rows, mods] + [_operand(a) for a in small]
    if tabs is not None:
        in_specs.append(pl.BlockSpec((tabs.shape[0], tm, LANES), lambda t, b: (0, t, 0)))
        args.append(tabs)
    return pl.pallas_call(
        functools.partial(_qkv_kernel, rope=tabs is not None),
        grid=(tiles, n_batch),
        in_specs=in_specs,
        out_specs=[tr_spec(512), tok_spec(512), tr_spec(256),
                   tr_spec(256), tok_spec(256), tr_spec(256),
                   tr_spec(256), tok_spec(256), tr_spec(256),
                   tr_spec(256), tok_spec(128), tr_spec(128)],
        out_shape=[tr_shape(512), tok_shape(512), tr_shape(256),
                   tr_shape(256), tok_shape(256), tr_shape(256),
                   tr_shape(256), tok_shape(256), tr_shape(256),
                   tr_shape(256), tok_shape(128), tr_shape(128)],
        compiler_params=_cparams(2),
        name="qkv",
    )(*args)


def _keep_rows(x, lo, hi):
    row = lax.broadcasted_iota(jnp.int32, x.shape, 0)
    return jnp.where(jnp.logical_and(row >= lo, row < hi), x, jnp.zeros_like(x))


def _sum_all(xs):
    return functools.reduce(lambda a, b: a + b, xs)


def _exp_scores(qt, groups, shift):
    s = [_dot(k, qt) if bias is None else _dot(k, qt) + bias for k, _, bias in groups]
    if shift is None:
        shift = functools.reduce(jnp.maximum, [jnp.max(x, axis=0, keepdims=True) for x in s])
    p = [jnp.exp2(x - shift) for x in s]
    return [x.astype(BF16) for x in p], _sum_all([jnp.sum(x, axis=0, keepdims=True) for x in p])


def _softmax_vt(qt, groups, shift):
    p, l = _exp_scores(qt, groups, shift)
    return _sum_all([_dot(g[1], x) for x, g in zip(p, groups)]) / l


def _mla_heads_t(qt_fn, groups_fn, shift):
    outs = []
    for h in range(4):
        c0 = 256 * (h // 2)
        lo = LANES * (h % 2)
        outs.append(_softmax_vt(_keep_rows(qt_fn(c0), lo, lo + LANES), groups_fn(h, c0), shift))
    return jnp.concatenate(outs, axis=0)


def _diff_heads_t(qt, groups_fn, lam, gsub_t, lam_init, shift):
    outs = []
    for h in range(4):
        groups = groups_fn(h)
        (p1, l1), (p2, l2) = [_exp_scores(_keep_rows(qt, 64 * h + DIFF_DIM * m, 64 * h + DIFF_DIM * (m + 1)),
                                          groups, shift) for m in range(2)]
        w1 = (1.0 / l1).astype(BF16)
        w2 = (lam / l2).astype(BF16)
        oh = _sum_all([_dot(g[1], a * w1 - b * w2) for a, b, g in zip(p1, p2, groups)])
        ms = jnp.mean(oh * oh, axis=0, keepdims=True)
        outs.append(oh * lax.rsqrt(ms + EPS) * gsub_t[64 * h:64 * h + 64, :])
    return jnp.concatenate(outs, axis=0) * (1.0 - lam_init)


def _gqa_heads_t(qt, groups_fn, shift):
    outs = []
    for g in range(2):
        qc = qt[LANES * g:LANES * (g + 1), :]
        for n in range(2):
            outs.append(_softmax_vt(_keep_rows(qc, 64 * n, 64 * n + 64), groups_fn(n), shift))
    return jnp.concatenate(outs, axis=0)


def _plain_heads_t(qt, groups_fn, shift):
    return jnp.concatenate([_softmax_vt(_keep_rows(qt, 64 * h, 64 * h + 64), groups_fn(h), shift) for h in range(4)],
                           axis=0)


def _diff_lambda(lam_ref, lam_init):
    lq1, lk1, lq2, lk2 = (lam_ref[i:i + 1, :] for i in range(4))
    return (jnp.exp(jnp.sum(lq1 * lk1, axis=-1, keepdims=True))
            - jnp.exp(jnp.sum(lq2 * lk2, axis=-1, keepdims=True)) + lam_init)


def _latent_attention(shift_ref, bound_index, chunk_fn):
    bound = shift_ref[bound_index]

    def run(shift):
        def step(c, carry):
            chunk_fn(pl.ds(pl.multiple_of(c * QC, QC), QC), shift)
            return carry

        lax.fori_loop(0, TQ // QC, step, 0)

    @pl.when(bound <= MAX_FIXED_SHIFT)
    def _():
        run(bound)

    @pl.when(jnp.logical_not(bound <= MAX_FIXED_SHIFT))
    def _():
        run(None)


def _mla_attn_kernel(shift_ref, qt_ref, kl_ref, vtl_ref, kc_ref, vtc_ref, o_ref, *, bound_index):
    def chunk(cols, shift):
        def groups_fn(h, c0):
            dims = slice(64 * h, 64 * h + 64)
            return [(kl_ref[:, c0:c0 + 256], vtl_ref[0, dims, :], None),
                    (kc_ref[:, c0:c0 + 256], vtc_ref[0, dims, :], None)]

        ot = _mla_heads_t(lambda c0: qt_ref[0, c0:c0 + 256, cols], groups_fn, shift)
        o_ref[cols, :] = ot.T.astype(BF16)

    _latent_attention(shift_ref, bound_index, chunk)


def _diff_attn_kernel(shift_ref, qt_ref, kl_ref, vtl_ref, kc_ref, vtc_ref, lam_ref, gsubt_ref, o_ref, *, lam_init,
                      bound_index):
    def chunk(cols, shift):
        def groups_fn(h):
            dims = slice(64 * h, 64 * h + 64)
            return [(kl_ref[...], vtl_ref[0, dims, :], None), (kc_ref[...], vtc_ref[0, dims, :], None)]

        lam = _diff_lambda(lam_ref, lam_init)
        ot = _diff_heads_t(qt_ref[0, :, cols], groups_fn, lam, gsubt_ref[...], lam_init, shift)
        o_ref[cols, :] = ot.T.astype(BF16)

    _latent_attention(shift_ref, bound_index, chunk)


def _gqa_attn_kernel(shift_ref, qt_ref, kl_ref, vtl_ref, kc_ref, vtc_ref, o_ref, *, bound_index):
    def chunk(cols, shift):
        def groups_fn(n):
            dims = slice(64 * n, 64 * n + 64)
            return [(kl_ref[...], vtl_ref[0, dims, :], None), (kc_ref[...], vtc_ref[0, dims, :], None)]

        ot = _gqa_heads_t(qt_ref[0, :, cols], groups_fn, shift)
        o_ref[cols, :] = ot.T.astype(BF16)

    _latent_attention(shift_ref, bound_index, chunk)


def _latent_attn_call(kernel, name, bounds, bound_index, qt, kl, vtl, kc, vtc, extra, n_batch):
    wq, wk, wv = qt.shape[1], kl.shape[1], vtl.shape[1]
    tiles = SEQ // TQ
    return pl.pallas_call(
        functools.partial(kernel, bound_index=bound_index),
        grid=(n_batch, tiles),
        in_specs=[pl.BlockSpec(memory_space=pltpu.SMEM),
                  pl.BlockSpec((1, wq, TQ), lambda b, t: (b, 0, t)),
                  pl.BlockSpec((SEQ, wk), lambda b, t: (b, 0)),
                  pl.BlockSpec((1, wv, SEQ), lambda b, t: (b, 0, 0)),
                  pl.BlockSpec((CTX, wk), lambda b, t: (b, 0)),
                  pl.BlockSpec((1, wv, CTX), lambda b, t: (b, 0, 0))] + [_const_spec(a) for a in extra],
        out_specs=pl.BlockSpec((TQ, 256), lambda b, t: (b * tiles + t, 0)),
        out_shape=jax.ShapeDtypeStruct((n_batch * SEQ, 256), BF16),
        compiler_params=_cparams(2),
        name=name,
    )(bounds, qt, kl, vtl, kc, vtc, *map(_operand, extra))


def _na_window(t):
    return jnp.clip(t - 1, 0, SEQ // NA_TQ - NA_WIN)


def _na_build_bias(t, pair_ref, bias_ref):
    rows_per_tile = NA_TQ // GRID_W
    k_row0 = rows_per_tile * _na_window(t)
    lane = lax.broadcasted_iota(jnp.int32, (GRID_W, LANES), 1)
    for i in range(rows_per_tile // 2):
        qr = rows_per_tile * t + 2 * i
        r0 = jnp.clip(qr - NA_ROWS // 2, 0, N_ROWS - NA_ROWS)
        r1 = jnp.clip(qr + 1 - NA_ROWS // 2, 0, N_ROWS - NA_ROWS)
        for kl in range(NA_WIN * rows_per_tile):
            kr = k_row0 + kl
            ok0 = jnp.logical_and(kr >= r0, kr < r0 + NA_ROWS).astype(jnp.int32)
            ok1 = jnp.logical_and(kr >= r1, kr < r1 + NA_ROWS).astype(jnp.int32)
            ok = jnp.where(lane < GRID_W, ok0, ok1) > 0
            d = jnp.clip(kr - qr + NA_ROWS - 1, 0, 2 * NA_ROWS - 1)
            for h in range(4):
                blk = jnp.where(ok, pair_ref[h, d], MASK_VALUE)
                bias_ref[h, GRID_W * kl:GRID_W * (kl + 1), LANES * i:LANES * (i + 1)] = blk


def _na_attn_kernel(shift_ref, qt_ref, k0_ref, k1_ref, k2_ref, kc_ref, vt0_ref, vt1_ref, vt2_ref, vtc_ref, pair_ref,
                    o_ref, bias_ref, *, bound_index):
    t = pl.program_id(0)

    @pl.when(pl.program_id(1) == 0)
    def _():
        _na_build_bias(t, pair_ref, bias_ref)

    win = ((k0_ref, vt0_ref), (k1_ref, vt1_ref), (k2_ref, vt2_ref))

    def groups_fn(h):
        dims = slice(64 * h, 64 * h + 64)
        return ([(k[...], vt[0, dims, :], bias_ref[h, NA_TQ * j:NA_TQ * (j + 1), :]) for j, (k, vt) in enumerate(win)]
                + [(kc_ref[...], vtc_ref[0, dims, :], None)])

    bound = shift_ref[bound_index]

    @pl.when(bound <= MAX_FIXED_SHIFT)
    def _():
        o_ref[...] = _plain_heads_t(qt_ref[0], groups_fn, bound).T.astype(BF16)

    @pl.when(jnp.logical_not(bound <= MAX_FIXED_SHIFT))
    def _():
        o_ref[...] = _plain_heads_t(qt_ref[0], groups_fn, None).T.astype(BF16)


def _na_attn_call(bounds, bound_index, qt, kl, vtl, kc, vtc, pair, n_batch):
    tiles = SEQ // NA_TQ
    k_spec = lambda j: pl.BlockSpec((NA_TQ, 256), lambda t, b: (b * tiles + _na_window(t) + j, 0))
    vt_spec = lambda j: pl.BlockSpec((1, 256, NA_TQ), lambda t, b: (b, 0, _na_window(t) + j))
    return pl.pallas_call(
        functools.partial(_na_attn_kernel, bound_index=bound_index),
        grid=(tiles, n_batch),
        in_specs=[pl.BlockSpec(memory_space=pltpu.SMEM),
                  pl.BlockSpec((1, 256, NA_TQ), lambda t, b: (b, 0, t)),
                  k_spec(0), k_spec(1), k_spec(2),
                  pl.BlockSpec((CTX, 256), lambda t, b: (b, 0)),
                  vt_spec(0), vt_spec(1), vt_spec(2),
                  pl.BlockSpec((1, 256, CTX), lambda t, b: (b, 0, 0)),
                  _const_spec(pair)],
        out_specs=pl.BlockSpec((NA_TQ, 256), lambda t, b: (b * tiles + t, 0)),
        out_shape=jax.ShapeDtypeStruct((n_batch * SEQ, 256), BF16),
        scratch_shapes=[pltpu.VMEM((4, NA_WIN * NA_TQ, NA_TQ), F32)],
        compiler_params=_cparams(2),
        name="na_attn",
    )(bounds, qt, kl, kl, kl, kc, vtl, vtl, vtl, vtc, _operand(pair))


def _ctx_attn_kernel(mqt_ref, mk_ref, mvt_ref, nqt_ref, nk_ref, nvt_ref, dqt_ref, dk_ref, dvt_ref,
                     gqt_ref, gk_ref, gvt_ref, lam_ref, gsubt_ref, oa_ref, ob_ref, oc_ref, od_ref, *, lam_init):
    head_dims = lambda h: slice(64 * h, 64 * h + 64)
    oa_ref[...] = _mla_heads_t(lambda c0: mqt_ref[0, c0:c0 + 256, :],
                               lambda h, c0: [(mk_ref[:, c0:c0 + 256], mvt_ref[0, head_dims(h), :], None)],
                               None).T.astype(BF16)
    ob_ref[...] = _plain_heads_t(nqt_ref[0], lambda h: [(nk_ref[...], nvt_ref[0, head_dims(h), :], None)],
                                 None).T.astype(BF16)
    lam = _diff_lambda(lam_ref, lam_init)
    oc_ref[...] = _diff_heads_t(dqt_ref[0], lambda h: [(dk_ref[...], dvt_ref[0, head_dims(h), :], None)], lam,
                                gsubt_ref[...], lam_init, None).T.astype(BF16)
    od_ref[...] = _gqa_heads_t(gqt_ref[0], lambda n: [(gk_ref[...], gvt_ref[0, head_dims(n), :], None)],
                               None).T.astype(BF16)


def _ctx_attn_call(qkv_c, lam, gsub, lam_init, n_batch):
    specs = []
    for a in qkv_c:
        if a.ndim == 3:
            specs.append(pl.BlockSpec((1, a.shape[1], CTX), lambda b: (b, 0, 0)))
        else:
            specs.append(pl.BlockSpec((CTX, a.shape[1]), lambda b: (b, 0)))
    out = jax.ShapeDtypeStruct((n_batch * CTX, 256), BF16)
    return pl.pallas_call(
        functools.partial(_ctx_attn_kernel, lam_init=lam_init),
        grid=(n_batch,),
        in_specs=specs + [_const_spec(lam), _const_spec(gsub)],
        out_specs=[pl.BlockSpec((CTX, 256), lambda b: (b, 0))] * 4,
        out_shape=[out] * 4,
        compiler_params=_cparams(1),
        name="ctx_attn",
    )(*qkv_c, _operand(lam), _operand(gsub))


def _merge_kernel(x_ref, mod_ref, g_ref, oa_ref, ob_ref, oc_ref, od_ref, wp_ref, wb_ref, wo_ref, out_ref):
    x = x_ref[...]
    mod = mod_ref[0]
    xn = _modnorm(x, g_ref[...], mod[:, 0:D], mod[:, D:2 * D]).astype(BF16)
    acc = jnp.zeros(x.shape, F32)
    for n, o_ref in enumerate((oa_ref, ob_ref, oc_ref, od_ref)):
        gate = jax.nn.sigmoid(_dot(xn, wp_ref[:, Z_W + n * D:Z_W + (n + 1) * D]))
        acc = acc + gate * _dot(o_ref[...], wb_ref[n])
    y = _dot(acc.astype(BF16), wo_ref[...])
    out_ref[...] = x + mod[:, 2 * D:3 * D] * y


def _merge_call(rows, mods, layer, mod_row, g1, branches, w_proj, w_branch, w_out, tm):
    n_rows = rows.shape[0]
    tm = min(tm, n_rows)
    tok = lambda w: pl.BlockSpec((tm, w), lambda i: (i, 0))
    return pl.pallas_call(
        _merge_kernel,
        grid=(n_rows // tm,),
        in_specs=[tok(D), _mod_spec(layer, mod_row), _const_spec(g1)]
                 + [tok(256)] * 4 + [_const_spec(w_proj), _const_spec(w_branch), _const_spec(w_out)],
        out_specs=tok(D),
        out_shape=jax.ShapeDtypeStruct((n_rows, D), F32),
        compiler_params=_cparams(1),
        name="merge",
    )(rows, mods, _operand(g1), *branches, _operand(w_proj), _operand(w_branch), _operand(w_out))


def _mlp_kernel(x_ref, mod_ref, g_ref, wu_ref, wd_ref, out_ref):
    x = x_ref[...]
    mod = mod_ref[0]
    xn = _modnorm(x, g_ref[...], mod[:, 3 * D:4 * D], mod[:, 4 * D:5 * D]).astype(BF16)
    h = jnp.square(jnp.maximum(_dot(xn, wu_ref[...]), 0.0))
    out_ref[...] = x + mod[:, 5 * D:6 * D] * _dot(h.astype(BF16), wd_ref[...])


def _mlp_call(rows, mods, layer, mod_row, g2, w_up, w_down, tm):
    n_rows = rows.shape[0]
    tm = min(tm, n_rows)
    tok = lambda w: pl.BlockSpec((tm, w), lambda i: (i, 0))
    return pl.pallas_call(
        _mlp_kernel,
        grid=(n_rows // tm,),
        in_specs=[tok(D), _mod_spec(layer, mod_row), _const_spec(g2),
                  _const_spec(w_up), _const_spec(w_down)],
        out_specs=tok(D),
        out_shape=jax.ShapeDtypeStruct((n_rows, D), F32),
        compiler_params=_cparams(1),
        name="mlp",
    )(rows, mods, _operand(g2), _operand(w_up), _operand(w_down))


def _block_diag_ones(n, seg):
    i = jnp.arange(n) // seg
    return (i[:, None] == i[None, :]).astype(BF16)


def _rope_parts(rot_dim):
    t = jnp.arange(SEQ)
    rows, cols = t // GRID_W, t % GRID_W
    n = rot_dim // 4
    inv_freq = jnp.power(ROPE_THETA, -jnp.arange(n, dtype=F32) / n)
    ang_r = rows.astype(F32)[:, None] * inv_freq
    ang_c = cols.astype(F32)[:, None] * inv_freq
    ang = jnp.concatenate([ang_r, ang_r, ang_c, ang_c], axis=-1)
    cos, sin = jnp.cos(ang), jnp.sin(ang)
    even = (jnp.arange(rot_dim) // n) % 2 == 0
    return cos, jnp.where(even, -sin, 0.0), jnp.where(even, 0.0, sin)


def _rope_tables():
    r32, r64 = _rope_parts(MLA_ROPE), _rope_parts(HEAD_DIM)
    ones = jnp.ones((SEQ, MLA_NOPE), F32)
    zeros_n = jnp.zeros((SEQ, MLA_NOPE), F32)
    pad = jnp.zeros((SEQ, LANES - MLA_QK), F32)
    mla = [jnp.concatenate([lead, part, pad], axis=1) for lead, part in zip((ones, zeros_n, zeros_n), r32)]
    return jnp.stack(mla + [jnp.tile(a, (1, LANES // MLA_ROPE)) for a in r32]
                     + [jnp.tile(a, (1, LANES // HEAD_DIM)) for a in r64])


def _na_pair_table(rpb):
    kc = jnp.arange(GRID_W)[:, None]
    qc = jnp.arange(GRID_W)[None, :]
    onehot = (kc - qc + NA_COLS - 1)[None] == jnp.arange(2 * NA_COLS - 1)[:, None, None]
    toeplitz = jnp.einsum("lhrd,dkq->lhrkq", rpb, onehot.astype(F32), precision=lax.Precision.HIGHEST)
    c0 = jnp.clip(qc - NA_COLS // 2, 0, GRID_W - NA_COLS)
    in_window = (kc >= c0) & (kc < c0 + NA_COLS)
    masked = jnp.where(in_window, toeplitz * LOG2E, MASK_VALUE)
    ext = jnp.pad(masked, ((0, 0), (0, 0), (1, 1), (0, 0), (0, 0)), constant_values=MASK_VALUE)
    return jnp.concatenate([ext[:, :, 1:], ext[:, :, :-1]], axis=-1)


def _prepare_params(p):
    w_in = p["w_in"]
    n_layers = w_in.shape[0]
    zcol = lambda n: jnp.zeros((n_layers, D, n), F32)
    kr = w_in[:, :, C_KR:C_KR + MLA_ROPE]
    kr_slots = jnp.concatenate([zcol(MLA_NOPE), kr, zcol(LANES - MLA_QK)] * 4, axis=2)
    head_order = jnp.array([0, 2, 1, 3])
    gq_cols = w_in[:, :, C_GQ:C_GQ + 256].reshape(n_layers, D, 4, HEAD_DIM)[:, :, head_order].reshape(n_layers, D, 256)
    w_proj = jnp.concatenate([
        w_in[:, :, C_CQ:C_CQ + MLA_Q_LORA], zcol(256 - MLA_Q_LORA),
        w_in[:, :, C_CKV:C_CKV + MLA_KV_LORA],
        kr_slots,
        w_in[:, :, C_NA:C_NA + 768],
        w_in[:, :, C_DF:C_DF + 768],
        gq_cols, w_in[:, :, C_GQ + 256:C_GQ + 512],
        w_in[:, :, C_GATE:]], axis=2).astype(BF16)

    w_uq = p["w_mla_uq"].reshape(n_layers, MLA_Q_LORA, 4, MLA_QK)
    w_uq = jnp.pad(w_uq, ((0, 0), (0, 256 - MLA_Q_LORA), (0, 0), (0, LANES - MLA_QK)))
    w_ukv = p["w_mla_ukv"].reshape(n_layers, MLA_KV_LORA, 4, 2 * MLA_NOPE)
    w_uk = jnp.pad(w_ukv[..., :MLA_NOPE], ((0, 0), (0, 0), (0, 0), (0, LANES - MLA_NOPE)))

    mla_gain = lambda g: jnp.tile(jnp.pad(g, ((0, 0), (0, LANES - MLA_QK))), (1, 4))
    rep = lambda g, n: jnp.tile(g, (1, n))
    row = lambda g: jnp.pad(g, ((0, 0), (0, 512 - g.shape[1])))
    hd_scale = HEAD_DIM ** -0.5 * LOG2E
    gains = jnp.stack([
        row(mla_gain(p["g_mla_q"]) * (MLA_QK ** -0.5 * LOG2E)), row(mla_gain(p["g_mla_k"])),
        row(rep(p["g_na_q"], 4) * hd_scale), row(rep(p["g_na_k"], 4)),
        row(rep(p["g_diff_q"], 8) * (DIFF_DIM ** -0.5 * LOG2E)), row(rep(p["g_diff_k"], 8)),
        row(rep(p["g_gqa_q"], 4) * hd_scale), row(rep(p["g_gqa_k"], 2))], axis=1)

    def score_bound(dim, g_q, g_k):
        return LOG2E * dim ** 0.5 * jnp.max(jnp.abs(g_q), axis=-1) * jnp.max(jnp.abs(g_k), axis=-1)

    bounds = jnp.stack([
        score_bound(MLA_QK, p["g_mla_q"], p["g_mla_k"]),
        score_bound(HEAD_DIM, p["g_na_q"], p["g_na_k"]) + LOG2E * jnp.max(jnp.abs(p["na_rpb"]), axis=(1, 2, 3)),
        score_bound(DIFF_DIM, p["g_diff_q"], p["g_diff_k"]),
        score_bound(HEAD_DIM, p["g_gqa_q"], p["g_gqa_k"])], axis=1).astype(F32).reshape(n_layers * N_BOUNDS)

    w_branch = p["w_branch"]
    wb_gqa = w_branch[:, 3].reshape(n_layers, 4, HEAD_DIM, D)[:, head_order].reshape(n_layers, 1, 256, D)
    w_branch = jnp.concatenate([w_branch[:, :3], wb_gqa], axis=1).astype(BF16)

    return dict(
        g1=p["g_norm1"][:, None], g2=p["g_norm2"][:, None],
        w_proj=w_proj,
        g_qa=jnp.pad(p["g_mla_qa"], ((0, 0), (0, 256 - MLA_Q_LORA)))[:, None], g_kva=p["g_mla_kva"][:, None],
        w_uq=w_uq.reshape(n_layers, 256, 512).astype(BF16), w_uk=w_uk.reshape(n_layers, MLA_KV_LORA, 512).astype(BF16),
        w_uv=w_ukv[..., MLA_NOPE:].reshape(n_layers, MLA_KV_LORA, 256).astype(BF16),
        gains=gains, bounds=bounds, na_pair=_na_pair_table(p["na_rpb"]),
        lam=jnp.stack([p["diff_lq1"], p["diff_lk1"], p["diff_lq2"], p["diff_lk2"]], axis=1),
        g_sub=rep(p["g_diff_sub"], 4)[:, :, None],
        w_branch=w_branch, w_out=p["w_out"].astype(BF16),
        w_up=p["w_up"].astype(BF16), w_down=p["w_down"].astype(BF16),
    )


def kernel(x, c, ctx, c_ctx, w_ada, b_ada, g_norm1, g_norm2, w_in, g_mla_qa, w_mla_uq, g_mla_kva, w_mla_ukv, g_mla_q, g_mla_k, g_na_q, g_na_k, na_rpb, g_diff_q, g_diff_k, diff_lq1, diff_lk1, diff_lq2, diff_lk2, g_diff_sub, g_gqa_q, g_gqa_k, w_branch, w_out, w_up, w_down):
    p = dict(w_in=w_in, g_norm1=g_norm1, g_norm2=g_norm2, g_mla_qa=g_mla_qa, w_mla_uq=w_mla_uq,
             g_mla_kva=g_mla_kva, w_mla_ukv=w_mla_ukv, g_mla_q=g_mla_q, g_mla_k=g_mla_k,
             g_na_q=g_na_q, g_na_k=g_na_k, na_rpb=na_rpb, g_diff_q=g_diff_q, g_diff_k=g_diff_k,
             diff_lq1=diff_lq1, diff_lk1=diff_lk1, diff_lq2=diff_lq2, diff_lk2=diff_lk2,
             g_diff_sub=g_diff_sub, g_gqa_q=g_gqa_q, g_gqa_k=g_gqa_k, w_branch=w_branch,
             w_out=w_out, w_up=w_up, w_down=w_down)
    n_batch = x.shape[0]
    depth = w_ada.shape[0]
    assert x.shape[1:] == (SEQ, D) and ctx.shape[1:] == (CTX, D)

    mod_rows = -(-(n_batch + 1) // 8) * 8
    c_all = jnp.concatenate([c, c_ctx[None], jnp.zeros((mod_rows - n_batch - 1, D), F32)], axis=0)
    mods = _ada_call(c_all, w_ada, b_ada).reshape(depth, mod_rows, 1, 6 * D)
    lat_tiles = SEQ // TM
    lat_row = lambda i: i // lat_tiles
    ctx_row = lambda i: n_batch

    consts = dict(bd128=_block_diag_ones(256, 128), bd64=_block_diag_ones(256, 64),
                  bd32=_block_diag_ones(256, 32), bd64h=_block_diag_ones(128, 64))
    tabs = _rope_tables()
    stacked = _prepare_params(p)
    bounds = stacked.pop("bounds")

    xl = x.reshape(n_batch * SEQ, D)
    xc = ctx.reshape(n_batch * CTX, D)
    for l in range(depth):
        need_ctx = l < depth - 1
        lam_init = 0.8 - 0.6 * math.exp(-0.3 * l)
        lw = {name: _Layer(a, l) for name, a in stacked.items()}
        bound = lambda which: N_BOUNDS * l + which

        lat = _qkv_call(xl, mods, l, lambda b: b, lw, consts, tabs, n_batch, SEQ, TM)
        cx = _qkv_call(xc, mods, l, ctx_row, lw, consts, None, n_batch, CTX, CTX)
        mqt, mk, mvt, nqt, nk, nvt, dqt, dk, dvt, gqt, gk, gvt = lat
        oa = _latent_attn_call(_mla_attn_kernel, "mla_attn", bounds, bound(B_MLA), mqt, mk, mvt, cx[1], cx[2], [],
                               n_batch)
        ob = _na_attn_call(bounds, bound(B_NA), nqt, nk, nvt, cx[4], cx[5], lw["na_pair"], n_batch)
        oc = _latent_attn_call(functools.partial(_diff_attn_kernel, lam_init=lam_init), "diff_attn",
                               bounds, bound(B_DIFF), dqt, dk, dvt, cx[7], cx[8], [lw["lam"], lw["g_sub"]], n_batch)
        od = _latent_attn_call(_gqa_attn_kernel, "gqa_attn", bounds, bound(B_GQA), gqt, gk, gvt, cx[10], cx[11], [],
                               n_batch)
        if need_ctx:
            oc_all = _ctx_attn_call(cx, lw["lam"], lw["g_sub"], lam_init, n_batch)
            xc = _merge_call(xc, mods, l, ctx_row, lw["g1"], oc_all, lw["w_proj"], lw["w_branch"], lw["w_out"], TM)
            xc = _mlp_call(xc, mods, l, ctx_row, lw["g2"], lw["w_up"], lw["w_down"], TM)
        xl = _merge_call(xl, mods, l, lat_row, lw["g1"], (oa, ob, oc, od), lw["w_proj"], lw["w_branch"], lw["w_out"], TM)
        xl = _mlp_call(xl, mods, l, lat_row, lw["g2"], lw["w_up"], lw["w_down"], TM)
    return xl.reshape(n_batch, SEQ, D)
```

```python
import functools
import math
from typing import NamedTuple

import jax
import jax.numpy as jnp
from jax import lax
from jax.experimental import pallas as pl
from jax.experimental.pallas import tpu as pltpu

F32 = jnp.float32
BF16 = jnp.bfloat16

D = 1024
SEQ = 2048
CTX = 256
GRID_W = 64
N_ROWS = SEQ // GRID_W
ROPE_THETA = 10000.0
EPS = 1e-6
HEAD_DIM = 64
MLA_Q_LORA = 192
MLA_KV_LORA = 128
MLA_NOPE = 64
MLA_ROPE = 32
MLA_QK = MLA_NOPE + MLA_ROPE
NA_ROWS = 8
NA_COLS = 16
DIFF_DIM = 32
D_FF = 4 * D

LANES = 128
TM = 512
TQ = 2048
QC = 1024
NA_TQ = 256
NA_WIN = 3
NA_SAMPLES = 2
MASK_VALUE = -1e30
LOG2E = math.log2(math.e)
MAX_FIXED_SHIFT = 50.0

C_CQ, C_CKV, C_KR, C_NA, C_DF, C_GQ, C_GATE = 0, 192, 320, 352, 1120, 1888, 2400
Z_CQ, Z_CKV, Z_KR, Z_NA, Z_DF, Z_GQ, Z_W = 0, 256, 384, 896, 1664, 2432, 2944
G_MQ, G_MK, G_NQ, G_NK, G_DQ, G_DK, G_GQ, G_GK = range(8)
T_MLA, T_R32, T_R64 = 0, 3, 6
B_MLA, B_NA, B_DIFF, B_GQA, N_BOUNDS = 0, 1, 2, 3, 4

VMEM_LIMIT = 56 * 1024 * 1024


def _cparams(n_axes):
    return pltpu.CompilerParams(dimension_semantics=("arbitrary",) * n_axes,
                                vmem_limit_bytes=VMEM_LIMIT)


def _dot(a, b):
    return jnp.dot(a, b, preferred_element_type=F32)


class _Layer(NamedTuple):
    stacked: jax.Array
    index: int


def _operand(a):
    return a.stacked if isinstance(a, _Layer) else a


def _const_spec(a):
    if isinstance(a, _Layer):
        shape = a.stacked.shape[1:]
        return pl.BlockSpec((None,) + shape, lambda *_: (a.index,) + (0,) * len(shape),
                            pipeline_mode=pl.Buffered(1))
    return pl.BlockSpec(a.shape, lambda *_: (0,) * a.ndim, pipeline_mode=pl.Buffered(1))


def _mod_spec(layer, row_fn):
    return pl.BlockSpec((None, 1, 1, 6 * D), lambda *idx: (layer, row_fn(*idx), 0, 0))


def _modnorm(x, g, shift, scale):
    y = x * lax.rsqrt(jnp.mean(x * x, axis=-1, keepdims=True) + EPS) * g
    return y * (1.0 + scale) + shift


def _ada_kernel(c_ref, w_ref, b_ref, o_ref):
    c = c_ref[...]
    s = c * jax.nn.sigmoid(c)
    o_ref[0] = _dot(s.astype(BF16), w_ref[0].astype(BF16)) + b_ref[0]


def _ada_call(c_all, w_ada, b_ada):
    n_layers = w_ada.shape[0]
    rows = c_all.shape[0]
    bn = 1536
    return pl.pallas_call(
        _ada_kernel,
        grid=(n_layers, 6 * D // bn),
        in_specs=[pl.BlockSpec((rows, D), lambda l, j: (0, 0)),
                  pl.BlockSpec((1, D, bn), lambda l, j: (l, 0, j)),
                  pl.BlockSpec((1, 1, bn), lambda l, j: (l, 0, j))],
        out_specs=pl.BlockSpec((1, rows, bn), lambda l, j: (l, 0, j)),
        out_shape=jax.ShapeDtypeStruct((n_layers, rows, 6 * D), F32),
        compiler_params=_cparams(2),
        name="ada",
    )(c_all, w_ada, b_ada.reshape(n_layers, 1, 6 * D))


def _seg_mean_sq(t, bd, inv_d):
    x2 = t * t
    hi = x2.astype(BF16)
    lo = (x2 - hi.astype(F32)).astype(BF16)
    return (_dot(hi, bd) + _dot(lo, bd)) * inv_d


def _qkv_kernel(*refs, rope):
    (x_ref, mod_ref, g1_ref, w_ref, bd128_ref, bd64_ref, bd32_ref, bd64h_ref,
     gqa_ref, gkva_ref, wuq_ref, wuk_ref, wuv_ref, gains_ref) = refs[:14]
    tab_ref = refs[14] if rope else None
    (mqt_ref, mk_ref, mvt_ref, nqt_ref, nk_ref, nvt_ref,
     dqt_ref, dk_ref, dvt_ref, gqt_ref, gk_ref, gvt_ref) = refs[-12:]

    mod = mod_ref[0]
    xn = _modnorm(x_ref[...], g1_ref[...], mod[:, 0:D], mod[:, D:2 * D]).astype(BF16)

    def cols(lo, hi):
        return _dot(xn, w_ref[:, lo:hi])

    def put(o_ref, lo, val):
        o_ref[:, lo:lo + LANES] = val.astype(BF16)

    def put_t(o_ref, lo, val):
        o_ref[0, lo:lo + LANES, :] = val.T.astype(BF16)

    def section(t, bd, inv_d, gain_row, tab, shift, o_ref, store):
        width = t.shape[1]
        y = t * lax.rsqrt(_seg_mean_sq(t, bd, inv_d) + EPS) * gains_ref[gain_row:gain_row + 1, 0:width]
        for j in range(width // LANES):
            yc = y[:, j * LANES:(j + 1) * LANES]
            if rope and tab is not None:
                yc = (yc * tab_ref[tab] + pltpu.roll(yc, LANES - shift, 1) * tab_ref[tab + 1]
                      + pltpu.roll(yc, shift, 1) * tab_ref[tab + 2])
            store(o_ref, j * LANES, yc)

    za = cols(Z_CQ, Z_NA)
    cq = za[:, Z_CQ:Z_CQ + 256]
    cqn = cq * lax.rsqrt(jnp.sum(cq * cq, axis=-1, keepdims=True) * (1.0 / MLA_Q_LORA) + EPS) * gqa_ref[...]
    q = _dot(cqn.astype(BF16), wuq_ref[...])
    ckv = za[:, Z_CKV:Z_CKV + 128]
    kvn = (ckv * lax.rsqrt(jnp.mean(ckv * ckv, axis=-1, keepdims=True) + EPS) * gkva_ref[...]).astype(BF16)
    k = _dot(kvn, wuk_ref[...]) + za[:, Z_KR:Z_KR + 512]
    mvt_ref[0] = _dot(kvn, wuv_ref[...]).T.astype(BF16)
    bd128 = bd128_ref[...]
    for c in range(2):
        sl = slice(256 * c, 256 * c + 256)
        section(q[:, sl], bd128, 1.0 / MLA_QK, G_MQ, T_MLA, MLA_ROPE // 4, mqt_ref.at[:, sl, :], put_t)
        section(k[:, sl], bd128, 1.0 / MLA_QK, G_MK, T_MLA, MLA_ROPE // 4, mk_ref.at[:, sl], put)

    zn = cols(Z_NA, Z_DF)
    bd64 = bd64_ref[...]
    section(zn[:, 0:256], bd64, 1.0 / HEAD_DIM, G_NQ, None, 0, nqt_ref, put_t)
    section(zn[:, 256:512], bd64, 1.0 / HEAD_DIM, G_NK, None, 0, nk_ref, put)
    nvt_ref[0] = zn[:, 512:768].T.astype(BF16)

    zd = cols(Z_DF, Z_GQ)
    bd32 = bd32_ref[...]
    section(zd[:, 0:256], bd32, 1.0 / DIFF_DIM, G_DQ, T_R32, DIFF_DIM // 4, dqt_ref, put_t)
    section(zd[:, 256:512], bd32, 1.0 / DIFF_DIM, G_DK, T_R32, DIFF_DIM // 4, dk_ref, put)
    dvt_ref[0] = zd[:, 512:768].T.astype(BF16)

    zg = cols(Z_GQ, Z_W)
    section(zg[:, 0:256], bd64, 1.0 / HEAD_DIM, G_GQ, T_R64, HEAD_DIM // 4, gqt_ref, put_t)
    section(zg[:, 256:384], bd64h_ref[...], 1.0 / HEAD_DIM, G_GK, T_R64, HEAD_DIM // 4, gk_ref, put)
    gvt_ref[0] = zg[:, 384:512].T.astype(BF16)


def _qkv_call(rows, mods, layer, mod_row, lw, consts, tabs, n_batch, seq, tm):
    tiles = seq // tm
    row_map = lambda t, b: (b * tiles + t, 0)
    tok_spec = lambda w: pl.BlockSpec((tm, w), row_map)
    tr_spec = lambda w: pl.BlockSpec((1, w, tm), lambda t, b: (b, 0, t))
    tok_shape = lambda w: jax.ShapeDtypeStruct((n_batch * seq, w), BF16)
    tr_shape = lambda w: jax.ShapeDtypeStruct((n_batch, w, seq), BF16)
    small = [lw["g1"], lw["w_proj"], consts["bd128"], consts["bd64"], consts["bd32"], consts["bd64h"],
             lw["g_qa"], lw["g_kva"], lw["w_uq"], lw["w_uk"], lw["w_uv"], lw["gains"]]
    in_specs = [tok_spec(D), _mod_spec(layer, lambda t, b: mod_row(b))]
    in_specs += [_const_spec(a) for a in small]
    args = [rows, mods] + [_operand(a) for a in small]
    if tabs is not None:
        in_specs.append(pl.BlockSpec((tabs.shape[0], tm, LANES), lambda t, b: (0, t, 0)))
        args.append(tabs)
    return pl.pallas_call(
        functools.partial(_qkv_kernel, rope=tabs is not None),
        grid=(tiles, n_batch),
        in_specs=in_specs,
        out_specs=[tr_spec(512), tok_spec(512), tr_spec(256),
                   tr_spec(256), tok_spec(256), tr_spec(256),
                   tr_spec(256), tok_spec(256), tr_spec(256),
                   tr_spec(256), tok_spec(128), tr_spec(128)],
        out_shape=[tr_shape(512), tok_shape(512), tr_shape(256),
                   tr_shape(256), tok_shape(256), tr_shape(256),
                   tr_shape(256), tok_shape(256), tr_shape(256),
                   tr_shape(256), tok_shape(128), tr_shape(128)],
        compiler_params=_cparams(2),
        name="qkv",
    )(*args)


def _keep_rows(x, lo, hi):
    row = lax.broadcasted_iota(jnp.int32, x.shape, 0)
    return jnp.where(jnp.logical_and(row >= lo, row < hi), x, jnp.zeros_like(x))


def _sum_all(xs):
    return functools.reduce(lambda a, b: a + b, xs)


def _exp_scores(qt, groups, shift):
    s = [_dot(k, qt) if bias is None else _dot(k, qt) + bias for k, _, bias in groups]
    if shift is None:
        shift = functools.reduce(jnp.maximum, [jnp.max(x, axis=0, keepdims=True) for x in s])
    p = [jnp.exp2(x - shift) for x in s]
    return [x.astype(BF16) for x in p], _sum_all([jnp.sum(x, axis=0, keepdims=True) for x in p])


def _softmax_vt(qt, groups, shift):
    p, l = _exp_scores(qt, groups, shift)
    return _sum_all([_dot(g[1], x) for x, g in zip(p, groups)]) / l


def _mla_heads_t(qt_fn, groups_fn, shift):
    outs = []
    for h in range(4):
        c0 = 256 * (h // 2)
        lo = LANES * (h % 2)
        outs.append(_softmax_vt(_keep_rows(qt_fn(c0), lo, lo + LANES), groups_fn(h, c0), shift))
    return jnp.concatenate(outs, axis=0)


def _diff_heads_t(qt, groups_fn, lam, gsub_t, lam_init, shift):
    outs = []
    for h in range(4):
        groups = groups_fn(h)
        (p1, l1), (p2, l2) = [_exp_scores(_keep_rows(qt, 64 * h + DIFF_DIM * m, 64 * h + DIFF_DIM * (m + 1)),
                                          groups, shift) for m in range(2)]
        w1 = (1.0 / l1).astype(BF16)
        w2 = (lam / l2).astype(BF16)
        oh = _sum_all([_dot(g[1], a * w1 - b * w2) for a, b, g in zip(p1, p2, groups)])
        ms = jnp.mean(oh * oh, axis=0, keepdims=True)
        outs.append(oh * lax.rsqrt(ms + EPS) * gsub_t[64 * h:64 * h + 64, :])
    return jnp.concatenate(outs, axis=0) * (1.0 - lam_init)


def _gqa_heads_t(qt, groups_fn, shift):
    outs = []
    for g in range(2):
        qc = qt[LANES * g:LANES * (g + 1), :]
        for n in range(2):
            outs.append(_softmax_vt(_keep_rows(qc, 64 * n, 64 * n + 64), groups_fn(n), shift))
    return jnp.concatenate(outs, axis=0)


def _plain_heads_t(qt, groups_fn, shift):
    return jnp.concatenate([_softmax_vt(_keep_rows(qt, 64 * h, 64 * h + 64), groups_fn(h), shift) for h in range(4)],
                           axis=0)


def _diff_lambda(lam_ref, lam_init):
    lq1, lk1, lq2, lk2 = (lam_ref[i:i + 1, :] for i in range(4))
    return (jnp.exp(jnp.sum(lq1 * lk1, axis=-1, keepdims=True))
            - jnp.exp(jnp.sum(lq2 * lk2, axis=-1, keepdims=True)) + lam_init)


def _latent_attention(shift_ref, bound_index, chunk_fn):
    bound = shift_ref[bound_index]

    def run(shift):
        def step(c, carry):
            chunk_fn(pl.ds(pl.multiple_of(c * QC, QC), QC), shift)
            return carry

        lax.fori_loop(0, TQ // QC, step, 0)

    @pl.when(bound <= MAX_FIXED_SHIFT)
    def _():
        run(bound)

    @pl.when(jnp.logical_not(bound <= MAX_FIXED_SHIFT))
    def _():
        run(None)


def _mla_attn_kernel(shift_ref, qt_ref, kl_ref, vtl_ref, kc_ref, vtc_ref, o_ref, *, bound_index):
    def chunk(cols, shift):
        def groups_fn(h, c0):
            dims = slice(64 * h, 64 * h + 64)
            return [(kl_ref[:, c0:c0 + 256], vtl_ref[0, dims, :], None),
                    (kc_ref[:, c0:c0 + 256], vtc_ref[0, dims, :], None)]

        ot = _mla_heads_t(lambda c0: qt_ref[0, c0:c0 + 256, cols], groups_fn, shift)
        o_ref[cols, :] = ot.T.astype(BF16)

    _latent_attention(shift_ref, bound_index, chunk)


def _diff_attn_kernel(shift_ref, qt_ref, kl_ref, vtl_ref, kc_ref, vtc_ref, lam_ref, gsubt_ref, o_ref, *, lam_init,
                      bound_index):
    def chunk(cols, shift):
        def groups_fn(h):
            dims = slice(64 * h, 64 * h + 64)
            return [(kl_ref[...], vtl_ref[0, dims, :], None), (kc_ref[...], vtc_ref[0, dims, :], None)]

        lam = _diff_lambda(lam_ref, lam_init)
        ot = _diff_heads_t(qt_ref[0, :, cols], groups_fn, lam, gsubt_ref[...], lam_init, shift)
        o_ref[cols, :] = ot.T.astype(BF16)

    _latent_attention(shift_ref, bound_index, chunk)


def _gqa_attn_kernel(shift_ref, qt_ref, kl_ref, vtl_ref, kc_ref, vtc_ref, o_ref, *, bound_index):
    def chunk(cols, shift):
        def groups_fn(n):
            dims = slice(64 * n, 64 * n + 64)
            return [(kl_ref[...], vtl_ref[0, dims, :], None), (kc_ref[...], vtc_ref[0, dims, :], None)]

        ot = _gqa_heads_t(qt_ref[0, :, cols], groups_fn, shift)
        o_ref[cols, :] = ot.T.astype(BF16)

    _latent_attention(shift_ref, bound_index, chunk)


def _latent_attn_call(kernel, name, bounds, bound_index, qt, kl, vtl, kc, vtc, extra, n_batch):
    wq, wk, wv = qt.shape[1], kl.shape[1], vtl.shape[1]
    tiles = SEQ // TQ
    return pl.pallas_call(
        functools.partial(kernel, bound_index=bound_index),
        grid=(n_batch, tiles),
        in_specs=[pl.BlockSpec(memory_space=pltpu.SMEM),
                  pl.BlockSpec((1, wq, TQ), lambda b, t: (b, 0, t)),
                  pl.BlockSpec((SEQ, wk), lambda b, t: (b, 0)),
                  pl.BlockSpec((1, wv, SEQ), lambda b, t: (b, 0, 0)),
                  pl.BlockSpec((CTX, wk), lambda b, t: (b, 0)),
                  pl.BlockSpec((1, wv, CTX), lambda b, t: (b, 0, 0))] + [_const_spec(a) for a in extra],
        out_specs=pl.BlockSpec((TQ, 256), lambda b, t: (b * tiles + t, 0)),
        out_shape=jax.ShapeDtypeStruct((n_batch * SEQ, 256), BF16),
        compiler_params=_cparams(2),
        name=name,
    )(bounds, qt, kl, vtl, kc, vtc, *map(_operand, extra))


def _na_window(t):
    return jnp.clip(t - 1, 0, SEQ // NA_TQ - NA_WIN)


def _na_build_bias(t, pair_ref, bias_ref):
    rows_per_tile = NA_TQ // GRID_W
    k_row0 = rows_per_tile * _na_window(t)
    lane = lax.broadcasted_iota(jnp.int32, (GRID_W, LANES), 1)
    for i in range(rows_per_tile // 2):
        qr = rows_per_tile * t + 2 * i
        r0 = jnp.clip(qr - NA_ROWS // 2, 0, N_ROWS - NA_ROWS)
        r1 = jnp.clip(qr + 1 - NA_ROWS // 2, 0, N_ROWS - NA_ROWS)
        for kl in range(NA_WIN * rows_per_tile):
            kr = k_row0 + kl
            ok0 = jnp.logical_and(kr >= r0, kr < r0 + NA_ROWS).astype(jnp.int32)
            ok1 = jnp.logical_and(kr >= r1, kr < r1 + NA_ROWS).astype(jnp.int32)
            ok = jnp.where(lane < GRID_W, ok0, ok1) > 0
            d = jnp.clip(kr - qr + NA_ROWS - 1, 0, 2 * NA_ROWS - 1)
            for h in range(4):
                blk = jnp.where(ok, pair_ref[h, d], MASK_VALUE)
                bias_ref[h, GRID_W * kl:GRID_W * (kl + 1), LANES * i:LANES * (i + 1)] = blk


def _na_attn_kernel(shift_ref, qt_ref, k0_ref, k1_ref, k2_ref, kc_ref, vt0_ref, vt1_ref, vt2_ref, vtc_ref, pair_ref,
                    o_ref, bias_ref, *, bound_index):
    t = pl.program_id(0)

    @pl.when(pl.program_id(1) == 0)
    def _():
        _na_build_bias(t, pair_ref, bias_ref)

    win = ((k0_ref, vt0_ref), (k1_ref, vt1_ref), (k2_ref, vt2_ref))

    def run(shift):
        for s in range(qt_ref.shape[0]):
            def groups_fn(h):
                dims = slice(64 * h, 64 * h + 64)
                return ([(k[s], vt[s, dims, :], bias_ref[h, NA_TQ * j:NA_TQ * (j + 1), :])
                         for j, (k, vt) in enumerate(win)] + [(kc_ref[s], vtc_ref[s, dims, :], None)])

            o_ref[s] = _plain_heads_t(qt_ref[s], groups_fn, shift).T.astype(BF16)

    bound = shift_ref[bound_index]

    @pl.when(bound <= MAX_FIXED_SHIFT)
    def _():
        run(bound)

    @pl.when(jnp.logical_not(bound <= MAX_FIXED_SHIFT))
    def _():
        run(None)


def _na_attn_call(bounds, bound_index, qt, kl, vtl, kc, vtc, pair, n_batch):
    tiles = SEQ // NA_TQ
    nb = math.gcd(NA_SAMPLES, n_batch)
    kl = kl.reshape(n_batch, SEQ, 256)
    kc = kc.reshape(n_batch, CTX, 256)
    k_spec = lambda j: pl.BlockSpec((nb, NA_TQ, 256), lambda t, b: (b, _na_window(t) + j, 0))
    vt_spec = lambda j: pl.BlockSpec((nb, 256, NA_TQ), lambda t, b: (b, 0, _na_window(t) + j))
    out = pl.pallas_call(
        functools.partial(_na_attn_kernel, bound_index=bound_index),
        grid=(tiles, n_batch // nb),
        in_specs=[pl.BlockSpec(memory_space=pltpu.SMEM),
                  pl.BlockSpec((nb, 256, NA_TQ), lambda t, b: (b, 0, t)),
                  k_spec(0), k_spec(1), k_spec(2),
                  pl.BlockSpec((nb, CTX, 256), lambda t, b: (b, 0, 0)),
                  vt_spec(0), vt_spec(1), vt_spec(2),
                  pl.BlockSpec((nb, 256, CTX), lambda t, b: (b, 0, 0)),
                  _const_spec(pair)],
        out_specs=pl.BlockSpec((nb, NA_TQ, 256), lambda t, b: (b, t, 0)),
        out_shape=jax.ShapeDtypeStruct((n_batch, SEQ, 256), BF16),
        scratch_shapes=[pltpu.VMEM((4, NA_WIN * NA_TQ, NA_TQ), F32)],
        compiler_params=_cparams(2),
        name="na_attn",
    )(bounds, qt, kl, kl, kl, kc, vtl, vtl, vtl, vtc, _operand(pair))
    return out.reshape(n_batch * SEQ, 256)


def _ctx_attn_kernel(mqt_ref, mk_ref, mvt_ref, nqt_ref, nk_ref, nvt_ref, dqt_ref, dk_ref, dvt_ref,
                     gqt_ref, gk_ref, gvt_ref, lam_ref, gsubt_ref, oa_ref, ob_ref, oc_ref, od_ref, *, lam_init):
    head_dims = lambda h: slice(64 * h, 64 * h + 64)
    oa_ref[...] = _mla_heads_t(lambda c0: mqt_ref[0, c0:c0 + 256, :],
                               lambda h, c0: [(mk_ref[:, c0:c0 + 256], mvt_ref[0, head_dims(h), :], None)],
                               None).T.astype(BF16)
    ob_ref[...] = _plain_heads_t(nqt_ref[0], lambda h: [(nk_ref[...], nvt_ref[0, head_dims(h), :], None)],
                                 None).T.astype(BF16)
    lam = _diff_lambda(lam_ref, lam_init)
    oc_ref[...] = _diff_heads_t(dqt_ref[0], lambda h: [(dk_ref[...], dvt_ref[0, head_dims(h), :], None)], lam,
                                gsubt_ref[...], lam_init, None).T.astype(BF16)
    od_ref[...] = _gqa_heads_t(gqt_ref[0], lambda n: [(gk_ref[...], gvt_ref[0, head_dims(n), :], None)],
                               None).T.astype(BF16)


def _ctx_attn_call(qkv_c, lam, gsub, lam_init, n_batch):
    specs = []
    for a in qkv_c:
        if a.ndim == 3:
            specs.append(pl.BlockSpec((1, a.shape[1], CTX), lambda b: (b, 0, 0)))
        else:
            specs.append(pl.BlockSpec((CTX, a.shape[1]), lambda b: (b, 0)))
    out = jax.ShapeDtypeStruct((n_batch * CTX, 256), BF16)
    return pl.pallas_call(
        functools.partial(_ctx_attn_kernel, lam_init=lam_init),
        grid=(n_batch,),
        in_specs=specs + [_const_spec(lam), _const_spec(gsub)],
        out_specs=[pl.BlockSpec((CTX, 256), lambda b: (b, 0))] * 4,
        out_shape=[out] * 4,
        compiler_params=_cparams(1),
        name="ctx_attn",
    )(*qkv_c, _operand(lam), _operand(gsub))


def _merge_kernel(x_ref, mod_ref, g_ref, oa_ref, ob_ref, oc_ref, od_ref, wp_ref, wb_ref, wo_ref, out_ref):
    x = x_ref[...]
    mod = mod_ref[0]
    xn = _modnorm(x, g_ref[...], mod[:, 0:D], mod[:, D:2 * D]).astype(BF16)
    acc = jnp.zeros(x.shape, F32)
    for n, o_ref in enumerate((oa_ref, ob_ref, oc_ref, od_ref)):
        gate = jax.nn.sigmoid(_dot(xn, wp_ref[:, Z_W + n * D:Z_W + (n + 1) * D]))
        acc = acc + gate * _dot(o_ref[...], wb_ref[n])
    y = _dot(acc.astype(BF16), wo_ref[...])
    out_ref[...] = x + mod[:, 2 * D:3 * D] * y


def _merge_call(rows, mods, layer, mod_row, g1, branches, w_proj, w_branch, w_out, tm):
    n_rows = rows.shape[0]
    tm = min(tm, n_rows)
    tok = lambda w: pl.BlockSpec((tm, w), lambda i: (i, 0))
    return pl.pallas_call(
        _merge_kernel,
        grid=(n_rows // tm,),
        in_specs=[tok(D), _mod_spec(layer, mod_row), _const_spec(g1)]
                 + [tok(256)] * 4 + [_const_spec(w_proj), _const_spec(w_branch), _const_spec(w_out)],
        out_specs=tok(D),
        out_shape=jax.ShapeDtypeStruct((n_rows, D), F32),
        compiler_params=_cparams(1),
        name="merge",
    )(rows, mods, _operand(g1), *branches, _operand(w_proj), _operand(w_branch), _operand(w_out))


def _mlp_kernel(x_ref, mod_ref, g_ref, wu_ref, wd_ref, out_ref):
    x = x_ref[...]
    mod = mod_ref[0]
    xn = _modnorm(x, g_ref[...], mod[:, 3 * D:4 * D], mod[:, 4 * D:5 * D]).astype(BF16)
    h = jnp.square(jnp.maximum(_dot(xn, wu_ref[...]), 0.0))
    out_ref[...] = x + mod[:, 5 * D:6 * D] * _dot(h.astype(BF16), wd_ref[...])


def _mlp_call(rows, mods, layer, mod_row, g2, w_up, w_down, tm):
    n_rows = rows.shape[0]
    tm = min(tm, n_rows)
    tok = lambda w: pl.BlockSpec((tm, w), lambda i: (i, 0))
    return pl.pallas_call(
        _mlp_kernel,
        grid=(n_rows // tm,),
        in_specs=[tok(D), _mod_spec(layer, mod_row), _const_spec(g2),
                  _const_spec(w_up), _const_spec(w_down)],
        out_specs=tok(D),
        out_shape=jax.ShapeDtypeStruct((n_rows, D), F32),
        compiler_params=_cparams(1),
        name="mlp",
    )(rows, mods, _operand(g2), _operand(w_up), _operand(w_down))


def _block_diag_ones(n, seg):
    i = jnp.arange(n) // seg
    return (i[:, None] == i[None, :]).astype(BF16)


def _rope_parts(rot_dim):
    t = jnp.arange(SEQ)
    rows, cols = t // GRID_W, t % GRID_W
    n = rot_dim // 4
    inv_freq = jnp.power(ROPE_THETA, -jnp.arange(n, dtype=F32) / n)
    ang_r = rows.astype(F32)[:, None] * inv_freq
    ang_c = cols.astype(F32)[:, None] * inv_freq
    ang = jnp.concatenate([ang_r, ang_r, ang_c, ang_c], axis=-1)
    cos, sin = jnp.cos(ang), jnp.sin(ang)
    even = (jnp.arange(rot_dim) // n) % 2 == 0
    return cos, jnp.where(even, -sin, 0.0), jnp.where(even, 0.0, sin)


def _rope_tables():
    r32, r64 = _rope_parts(MLA_ROPE), _rope_parts(HEAD_DIM)
    ones = jnp.ones((SEQ, MLA_NOPE), F32)
    zeros_n = jnp.zeros((SEQ, MLA_NOPE), F32)
    pad = jnp.zeros((SEQ, LANES - MLA_QK), F32)
    mla = [jnp.concatenate([lead, part, pad], axis=1) for lead, part in zip((ones, zeros_n, zeros_n), r32)]
    return jnp.stack(mla + [jnp.tile(a, (1, LANES // MLA_ROPE)) for a in r32]
                     + [jnp.tile(a, (1, LANES // HEAD_DIM)) for a in r64])


def _na_pair_table(rpb):
    kc = jnp.arange(GRID_W)[:, None]
    qc = jnp.arange(GRID_W)[None, :]
    onehot = (kc - qc + NA_COLS - 1)[None] == jnp.arange(2 * NA_COLS - 1)[:, None, None]
    toeplitz = jnp.einsum("lhrd,dkq->lhrkq", rpb, onehot.astype(F32), precision=lax.Precision.HIGHEST)
    c0 = jnp.clip(qc - NA_COLS // 2, 0, GRID_W - NA_COLS)
    in_window = (kc >= c0) & (kc < c0 + NA_COLS)
    masked = jnp.where(in_window, toeplitz * LOG2E, MASK_VALUE)
    ext = jnp.pad(masked, ((0, 0), (0, 0), (1, 1), (0, 0), (0, 0)), constant_values=MASK_VALUE)
    return jnp.concatenate([ext[:, :, 1:], ext[:, :, :-1]], axis=-1)


def _prepare_params(p):
    w_in = p["w_in"]
    n_layers = w_in.shape[0]
    zcol = lambda n: jnp.zeros((n_layers, D, n), F32)
    kr = w_in[:, :, C_KR:C_KR + MLA_ROPE]
    kr_slots = jnp.concatenate([zcol(MLA_NOPE), kr, zcol(LANES - MLA_QK)] * 4, axis=2)
    head_order = jnp.array([0, 2, 1, 3])
    gq_cols = w_in[:, :, C_GQ:C_GQ + 256].reshape(n_layers, D, 4, HEAD_DIM)[:, :, head_order].reshape(n_layers, D, 256)
    w_proj = jnp.concatenate([
        w_in[:, :, C_CQ:C_CQ + MLA_Q_LORA], zcol(256 - MLA_Q_LORA),
        w_in[:, :, C_CKV:C_CKV + MLA_KV_LORA],
        kr_slots,
        w_in[:, :, C_NA:C_NA + 768],
        w_in[:, :, C_DF:C_DF + 768],
        gq_cols, w_in[:, :, C_GQ + 256:C_GQ + 512],
        w_in[:, :, C_GATE:]], axis=2).astype(BF16)

    w_uq = p["w_mla_uq"].reshape(n_layers, MLA_Q_LORA, 4, MLA_QK)
    w_uq = jnp.pad(w_uq, ((0, 0), (0, 256 - MLA_Q_LORA), (0, 0), (0, LANES - MLA_QK)))
    w_ukv = p["w_mla_ukv"].reshape(n_layers, MLA_KV_LORA, 4, 2 * MLA_NOPE)
    w_uk = jnp.pad(w_ukv[..., :MLA_NOPE], ((0, 0), (0, 0), (0, 0), (0, LANES - MLA_NOPE)))

    mla_gain = lambda g: jnp.tile(jnp.pad(g, ((0, 0), (0, LANES - MLA_QK))), (1, 4))
    rep = lambda g, n: jnp.tile(g, (1, n))
    row = lambda g: jnp.pad(g, ((0, 0), (0, 512 - g.shape[1])))
    hd_scale = HEAD_DIM ** -0.5 * LOG2E
    gains = jnp.stack([
        row(mla_gain(p["g_mla_q"]) * (MLA_QK ** -0.5 * LOG2E)), row(mla_gain(p["g_mla_k"])),
        row(rep(p["g_na_q"], 4) * hd_scale), row(rep(p["g_na_k"], 4)),
        row(rep(p["g_diff_q"], 8) * (DIFF_DIM ** -0.5 * LOG2E)), row(rep(p["g_diff_k"], 8)),
        row(rep(p["g_gqa_q"], 4) * hd_scale), row(rep(p["g_gqa_k"], 2))], axis=1)

    def score_bound(dim, g_q, g_k):
        return LOG2E * dim ** 0.5 * jnp.max(jnp.abs(g_q), axis=-1) * jnp.max(jnp.abs(g_k), axis=-1)

    bounds = jnp.stack([
        score_bound(MLA_QK, p["g_mla_q"], p["g_mla_k"]),
        score_bound(HEAD_DIM, p["g_na_q"], p["g_na_k"]) + LOG2E * jnp.max(jnp.abs(p["na_rpb"]), axis=(1, 2, 3)),
        score_bound(DIFF_DIM, p["g_diff_q"], p["g_diff_k"]),
        score_bound(HEAD_DIM, p["g_gqa_q"], p["g_gqa_k"])], axis=1).astype(F32).reshape(n_layers * N_BOUNDS)

    w_branch = p["w_branch"]
    wb_gqa = w_branch[:, 3].reshape(n_layers, 4, HEAD_DIM, D)[:, head_order].reshape(n_layers, 1, 256, D)
    w_branch = jnp.concatenate([w_branch[:, :3], wb_gqa], axis=1).astype(BF16)

    return dict(
        g1=p["g_norm1"][:, None], g2=p["g_norm2"][:, None],
        w_proj=w_proj,
        g_qa=jnp.pad(p["g_mla_qa"], ((0, 0), (0, 256 - MLA_Q_LORA)))[:, None], g_kva=p["g_mla_kva"][:, None],
        w_uq=w_uq.reshape(n_layers, 256, 512).astype(BF16), w_uk=w_uk.reshape(n_layers, MLA_KV_LORA, 512).astype(BF16),
        w_uv=w_ukv[..., MLA_NOPE:].reshape(n_layers, MLA_KV_LORA, 256).astype(BF16),
        gains=gains, bounds=bounds, na_pair=_na_pair_table(p["na_rpb"]),
        lam=jnp.stack([p["diff_lq1"], p["diff_lk1"], p["diff_lq2"], p["diff_lk2"]], axis=1),
        g_sub=rep(p["g_diff_sub"], 4)[:, :, None],
        w_branch=w_branch, w_out=p["w_out"].astype(BF16),
        w_up=p["w_up"].astype(BF16), w_down=p["w_down"].astype(BF16),
    )


def kernel(x, c, ctx, c_ctx, w_ada, b_ada, g_norm1, g_norm2, w_in, g_mla_qa, w_mla_uq, g_mla_kva, w_mla_ukv, g_mla_q, g_mla_k, g_na_q, g_na_k, na_rpb, g_diff_q, g_diff_k, diff_lq1, diff_lk1, diff_lq2, diff_lk2, g_diff_sub, g_gqa_q, g_gqa_k, w_branch, w_out, w_up, w_down):
    p = dict(w_in=w_in, g_norm1=g_norm1, g_norm2=g_norm2, g_mla_qa=g_mla_qa, w_mla_uq=w_mla_uq,
             g_mla_kva=g_mla_kva, w_mla_ukv=w_mla_ukv, g_mla_q=g_mla_q, g_mla_k=g_mla_k,
             g_na_q=g_na_q, g_na_k=g_na_k, na_rpb=na_rpb, g_diff_q=g_diff_q, g_diff_k=g_diff_k,
             diff_lq1=diff_lq1, diff_lk1=diff_lk1, diff_lq2=diff_lq2, diff_lk2=diff_lk2,
             g_diff_sub=g_diff_sub, g_gqa_q=g_gqa_q, g_gqa_k=g_gqa_k, w_branch=w_branch,
             w_out=w_out, w_up=w_up, w_down=w_down)
    n_batch = x.shape[0]
    depth = w_ada.shape[0]
    assert x.shape[1:] == (SEQ, D) and ctx.shape[1:] == (CTX, D)

    mod_rows = -(-(n_batch + 1) // 8) * 8
    c_all = jnp.concatenate([c, c_ctx[None], jnp.zeros((mod_rows - n_batch - 1, D), F32)], axis=0)
    mods = _ada_call(c_all, w_ada, b_ada).reshape(depth, mod_rows, 1, 6 * D)
    lat_tiles = SEQ // TM
    lat_row = lambda i: i // lat_tiles
    ctx_row = lambda i: n_batch

    consts = dict(bd128=_block_diag_ones(256, 128), bd64=_block_diag_ones(256, 64),
                  bd32=_block_diag_ones(256, 32), bd64h=_block_diag_ones(128, 64))
    tabs = _rope_tables()
    stacked = _prepare_params(p)
    bounds = stacked.pop("bounds")

    xl = x.reshape(n_batch * SEQ, D)
    xc = ctx.reshape(n_batch * CTX, D)
    for l in range(depth):
        need_ctx = l < depth - 1
        lam_init = 0.8 - 0.6 * math.exp(-0.3 * l)
        lw = {name: _Layer(a, l) for name, a in stacked.items()}
        bound = lambda which: N_BOUNDS * l + which

        lat = _qkv_call(xl, mods, l, lambda b: b, lw, consts, tabs, n_batch, SEQ, TM)
        cx = _qkv_call(xc, mods, l, ctx_row, lw, consts, None, n_batch, CTX, CTX)
        mqt, mk, mvt, nqt, nk, nvt, dqt, dk, dvt, gqt, gk, gvt = lat
        oa = _latent_attn_call(_mla_attn_kernel, "mla_attn", bounds, bound(B_MLA), mqt, mk, mvt, cx[1], cx[2], [],
                               n_batch)
        ob = _na_attn_call(bounds, bound(B_NA), nqt, nk, nvt, cx[4], cx[5], lw["na_pair"], n_batch)
        oc = _latent_attn_call(functools.partial(_diff_attn_kernel, lam_init=lam_init), "diff_attn",
                               bounds, bound(B_DIFF), dqt, dk, dvt, cx[7], cx[8], [lw["lam"], lw["g_sub"]], n_batch)
        od = _latent_attn_call(_gqa_attn_kernel, "gqa_attn", bounds, bound(B_GQA), gqt, gk, gvt, cx[10], cx[11], [],
                               n_batch)
        if need_ctx:
            oc_all = _ctx_attn_call(cx, lw["lam"], lw["g_sub"], lam_init, n_batch)
            xc = _merge_call(xc, mods, l, ctx_row, lw["g1"], oc_all, lw["w_proj"], lw["w_branch"], lw["w_out"], TM)
            xc = _mlp_call(xc, mods, l, ctx_row, lw["g2"], lw["w_up"], lw["w_down"], TM)
        xl = _merge_call(xl, mods, l, lat_row, lw["g1"], (oa, ob, oc, od), lw["w_proj"], lw["w_branch"], lw["w_out"], TM)
        xl = _mlp_call(xl, mods, l, lat_row, lw["g2"], lw["w_up"], lw["w_down"], TM)
    return xl.reshape(n_batch, SEQ, D)
```

```python
import functools
import math
from typing import NamedTuple

import jax
import jax.numpy as jnp
from jax import lax
from jax.experimental import pallas as pl
from jax.experimental.pallas import tpu as pltpu

F32 = jnp.float32
BF16 = jnp.bfloat16

D = 1024
SEQ = 2048
CTX = 256
GRID_W = 64
N_ROWS = SEQ // GRID_W
ROPE_THETA = 10000.0
EPS = 1e-6
HEAD_DIM = 64
MLA_Q_LORA = 192
MLA_KV_LORA = 128
MLA_NOPE = 64
MLA_ROPE = 32
MLA_QK = MLA_NOPE + MLA_ROPE
NA_ROWS = 8
NA_COLS = 16
DIFF_DIM = 32
D_FF = 4 * D

LANES = 128
TM = 512
TQ = 2048
QC = 1024
NA_TQ = 256
NA_WIN = 3
NA_SAMPLES = 4
CTX_SAMPLES = 4
MASK_VALUE = -1e30
LOG2E = math.log2(math.e)
MAX_FIXED_SHIFT = 50.0

C_CQ, C_CKV, C_KR, C_NA, C_DF, C_GQ, C_GATE = 0, 192, 320, 352, 1120, 1888, 2400
Z_CQ, Z_CKV, Z_KR, Z_NA, Z_DF, Z_GQ, Z_W = 0, 256, 384, 896, 1664, 2432, 2944
G_MQ, G_MK, G_NQ, G_NK, G_DQ, G_DK, G_GQ, G_GK = range(8)
T_MLA, T_R32, T_R64 = 0, 3, 6
B_MLA, B_NA, B_DIFF, B_GQA, N_BOUNDS = 0, 1, 2, 3, 4

VMEM_LIMIT = 56 * 1024 * 1024


def _cparams(n_axes):
    return pltpu.CompilerParams(dimension_semantics=("arbitrary",) * n_axes,
                                vmem_limit_bytes=VMEM_LIMIT)


def _dot(a, b):
    return jnp.dot(a, b, preferred_element_type=F32)


class _Layer(NamedTuple):
    stacked: jax.Array
    index: int


def _operand(a):
    return a.stacked if isinstance(a, _Layer) else a


def _const_spec(a):
    if isinstance(a, _Layer):
        shape = a.stacked.shape[1:]
        return pl.BlockSpec((None,) + shape, lambda *_: (a.index,) + (0,) * len(shape),
                            pipeline_mode=pl.Buffered(1))
    return pl.BlockSpec(a.shape, lambda *_: (0,) * a.ndim, pipeline_mode=pl.Buffered(1))


def _mod_spec(layer, row_fn):
    return pl.BlockSpec((None, 1, 1, 6 * D), lambda *idx: (layer, row_fn(*idx), 0, 0))


def _modnorm(x, g, shift, scale):
    y = x * lax.rsqrt(jnp.mean(x * x, axis=-1, keepdims=True) + EPS) * g
    return y * (1.0 + scale) + shift


def _ada_kernel(c_ref, w_ref, b_ref, o_ref):
    c = c_ref[...]
    s = c * jax.nn.sigmoid(c)
    o_ref[0] = _dot(s.astype(BF16), w_ref[0].astype(BF16)) + b_ref[0]


def _ada_call(c_all, w_ada, b_ada):
    n_layers = w_ada.shape[0]
    rows = c_all.shape[0]
    bn = 1536
    return pl.pallas_call(
        _ada_kernel,
        grid=(n_layers, 6 * D // bn),
        in_specs=[pl.BlockSpec((rows, D), lambda l, j: (0, 0)),
                  pl.BlockSpec((1, D, bn), lambda l, j: (l, 0, j)),
                  pl.BlockSpec((1, 1, bn), lambda l, j: (l, 0, j))],
        out_specs=pl.BlockSpec((1, rows, bn), lambda l, j: (l, 0, j)),
        out_shape=jax.ShapeDtypeStruct((n_layers, rows, 6 * D), F32),
        compiler_params=_cparams(2),
        name="ada",
    )(c_all, w_ada, b_ada.reshape(n_layers, 1, 6 * D))


def _seg_mean_sq(t, bd, inv_d):
    x2 = t * t
    hi = x2.astype(BF16)
    lo = (x2 - hi.astype(F32)).astype(BF16)
    return (_dot(hi, bd) + _dot(lo, bd)) * inv_d


def _qkv_kernel(*refs, rope):
    (x_ref, mod_ref, g1_ref, w_ref, bd128_ref, bd64_ref, bd32_ref, bd64h_ref,
     gqa_ref, gkva_ref, wuq_ref, wuk_ref, wuv_ref, gains_ref) = refs[:14]
    tab_ref = refs[14] if rope else None
    (mqt_ref, mk_ref, mvt_ref, nqt_ref, nk_ref, nvt_ref,
     dqt_ref, dk_ref, dvt_ref, gqt_ref, gk_ref, gvt_ref) = refs[-12:]

    mod = mod_ref[0]
    xn = _modnorm(x_ref[...], g1_ref[...], mod[:, 0:D], mod[:, D:2 * D]).astype(BF16)

    def cols(lo, hi):
        return _dot(xn, w_ref[:, lo:hi])

    def put(o_ref, lo, val):
        o_ref[:, lo:lo + LANES] = val.astype(BF16)

    def put_t(o_ref, lo, val):
        o_ref[0, lo:lo + LANES, :] = val.T.astype(BF16)

    def section(t, bd, inv_d, gain_row, tab, shift, o_ref, store):
        width = t.shape[1]
        y = t * lax.rsqrt(_seg_mean_sq(t, bd, inv_d) + EPS) * gains_ref[gain_row:gain_row + 1, 0:width]
        for j in range(width // LANES):
            yc = y[:, j * LANES:(j + 1) * LANES]
            if rope and tab is not None:
                yc = (yc * tab_ref[tab] + pltpu.roll(yc, LANES - shift, 1) * tab_ref[tab + 1]
                      + pltpu.roll(yc, shift, 1) * tab_ref[tab + 2])
            store(o_ref, j * LANES, yc)

    za = cols(Z_CQ, Z_NA)
    cq = za[:, Z_CQ:Z_CQ + 256]
    cqn = cq * lax.rsqrt(jnp.sum(cq * cq, axis=-1, keepdims=True) * (1.0 / MLA_Q_LORA) + EPS) * gqa_ref[...]
    q = _dot(cqn.astype(BF16), wuq_ref[...])
    ckv = za[:, Z_CKV:Z_CKV + 128]
    kvn = (ckv * lax.rsqrt(jnp.mean(ckv * ckv, axis=-1, keepdims=True) + EPS) * gkva_ref[...]).astype(BF16)
    k = _dot(kvn, wuk_ref[...]) + za[:, Z_KR:Z_KR + 512]
    mvt_ref[0] = _dot(kvn, wuv_ref[...]).T.astype(BF16)
    bd128 = bd128_ref[...]
    for c in range(2):
        sl = slice(256 * c, 256 * c + 256)
        section(q[:, sl], bd128, 1.0 / MLA_QK, G_MQ, T_MLA, MLA_ROPE // 4, mqt_ref.at[:, sl, :], put_t)
        section(k[:, sl], bd128, 1.0 / MLA_QK, G_MK, T_MLA, MLA_ROPE // 4, mk_ref.at[:, sl], put)

    zn = cols(Z_NA, Z_DF)
    bd64 = bd64_ref[...]
    section(zn[:, 0:256], bd64, 1.0 / HEAD_DIM, G_NQ, None, 0, nqt_ref, put_t)
    section(zn[:, 256:512], bd64, 1.0 / HEAD_DIM, G_NK, None, 0, nk_ref, put)
    nvt_ref[0] = zn[:, 512:768].T.astype(BF16)

    zd = cols(Z_DF, Z_GQ)
    bd32 = bd32_ref[...]
    section(zd[:, 0:256], bd32, 1.0 / DIFF_DIM, G_DQ, T_R32, DIFF_DIM // 4, dqt_ref, put_t)
    section(zd[:, 256:512], bd32, 1.0 / DIFF_DIM, G_DK, T_R32, DIFF_DIM // 4, dk_ref, put)
    dvt_ref[0] = zd[:, 512:768].T.astype(BF16)

    zg = cols(Z_GQ, Z_W)
    section(zg[:, 0:256], bd64, 1.0 / HEAD_DIM, G_GQ, T_R64, HEAD_DIM // 4, gqt_ref, put_t)
    section(zg[:, 256:384], bd64h_ref[...], 1.0 / HEAD_DIM, G_GK, T_R64, HEAD_DIM // 4, gk_ref, put)
    gvt_ref[0] = zg[:, 384:512].T.astype(BF16)


def _qkv_call(rows, mods, layer, mod_row, lw, consts, tabs, n_batch, seq, tm):
    tiles = seq // tm
    row_map = lambda t, b: (b * tiles + t, 0)
    tok_spec = lambda w: pl.BlockSpec((tm, w), row_map)
    tr_spec = lambda w: pl.BlockSpec((1, w, tm), lambda t, b: (b, 0, t))
    tok_shape = lambda w: jax.ShapeDtypeStruct((n_batch * seq, w), BF16)
    tr_shape = lambda w: jax.ShapeDtypeStruct((n_batch, w, seq), BF16)
    small = [lw["g1"], lw["w_proj"], consts["bd128"], consts["bd64"], consts["bd32"], consts["bd64h"],
             lw["g_qa"], lw["g_kva"], lw["w_uq"], lw["w_uk"], lw["w_uv"], lw["gains"]]
    in_specs = [tok_spec(D), _mod_spec(layer, lambda t, b: mod_row(b))]
    in_specs += [_const_spec(a) for a in small]
    args = [rows, mods] + [_operand(a) for a in small]
    if tabs is not None:
        in_specs.append(pl.BlockSpec((tabs.shape[0], tm, LANES), lambda t, b: (0, t, 0)))
        args.append(tabs)
    return pl.pallas_call(
        functools.partial(_qkv_kernel, rope=tabs is not None),
        grid=(tiles, n_batch),
        in_specs=in_specs,
        out_specs=[tr_spec(512), tok_spec(512), tr_spec(256),
                   tr_spec(256), tok_spec(256), tr_spec(256),
                   tr_spec(256), tok_spec(256), tr_spec(256),
                   tr_spec(256), tok_spec(128), tr_spec(128)],
        out_shape=[tr_shape(512), tok_shape(512), tr_shape(256),
                   tr_shape(256), tok_shape(256), tr_shape(256),
                   tr_shape(256), tok_shape(256), tr_shape(256),
                   tr_shape(256), tok_shape(128), tr_shape(128)],
        compiler_params=_cparams(2),
        name="qkv",
    )(*args)


def _keep_rows(x, lo, hi):
    row = lax.broadcasted_iota(jnp.int32, x.shape, 0)
    return jnp.where(jnp.logical_and(row >= lo, row < hi), x, jnp.zeros_like(x))


def _sum_all(xs):
    return functools.reduce(lambda a, b: a + b, xs)


def _exp_scores(qt, groups, shift):
    s = [_dot(k, qt) if bias is None else _dot(k, qt) + bias for k, _, bias in groups]
    if shift is None:
        shift = functools.reduce(jnp.maximum, [jnp.max(x, axis=0, keepdims=True) for x in s])
    p = [jnp.exp2(x - shift) for x in s]
    return [x.astype(BF16) for x in p], _sum_all([jnp.sum(x, axis=0, keepdims=True) for x in p])


def _softmax_vt(qt, groups, shift):
    p, l = _exp_scores(qt, groups, shift)
    return _sum_all([_dot(g[1], x) for x, g in zip(p, groups)]) / l


def _mla_heads_t(qt_fn, groups_fn, shift):
    outs = []
    for h in range(4):
        c0 = 256 * (h // 2)
        lo = LANES * (h % 2)
        outs.append(_softmax_vt(_keep_rows(qt_fn(c0), lo, lo + LANES), groups_fn(h, c0), shift))
    return jnp.concatenate(outs, axis=0)


def _diff_heads_t(qt, groups_fn, lam, gsub_t, lam_init, shift):
    outs = []
    for h in range(4):
        groups = groups_fn(h)
        (p1, l1), (p2, l2) = [_exp_scores(_keep_rows(qt, 64 * h + DIFF_DIM * m, 64 * h + DIFF_DIM * (m + 1)),
                                          groups, shift) for m in range(2)]
        w1 = (1.0 / l1).astype(BF16)
        w2 = (lam / l2).astype(BF16)
        oh = _sum_all([_dot(g[1], a * w1 - b * w2) for a, b, g in zip(p1, p2, groups)])
        ms = jnp.mean(oh * oh, axis=0, keepdims=True)
        outs.append(oh * lax.rsqrt(ms + EPS) * gsub_t[64 * h:64 * h + 64, :])
    return jnp.concatenate(outs, axis=0) * (1.0 - lam_init)


def _gqa_heads_t(qt, groups_fn, shift):
    outs = []
    for g in range(2):
        qc = qt[LANES * g:LANES * (g + 1), :]
        for n in range(2):
            outs.append(_softmax_vt(_keep_rows(qc, 64 * n, 64 * n + 64), groups_fn(n), shift))
    return jnp.concatenate(outs, axis=0)


def _plain_heads_t(qt, groups_fn, shift):
    return jnp.concatenate([_softmax_vt(_keep_rows(qt, 64 * h, 64 * h + 64), groups_fn(h), shift) for h in range(4)],
                           axis=0)


def _diff_lambda(lam_ref, lam_init):
    lq1, lk1, lq2, lk2 = (lam_ref[i:i + 1, :] for i in range(4))
    return (jnp.exp(jnp.sum(lq1 * lk1, axis=-1, keepdims=True))
            - jnp.exp(jnp.sum(lq2 * lk2, axis=-1, keepdims=True)) + lam_init)


def _latent_attention(shift_ref, bound_index, chunk_fn):
    bound = shift_ref[bound_index]

    def run(shift):
        def step(c, carry):
            chunk_fn(pl.ds(pl.multiple_of(c * QC, QC), QC), shift)
            return carry

        lax.fori_loop(0, TQ // QC, step, 0)

    @pl.when(bound <= MAX_FIXED_SHIFT)
    def _():
        run(bound)

    @pl.when(jnp.logical_not(bound <= MAX_FIXED_SHIFT))
    def _():
        run(None)


def _mla_attn_kernel(shift_ref, qt_ref, kl_ref, vtl_ref, kc_ref, vtc_ref, o_ref, *, bound_index):
    def chunk(cols, shift):
        def groups_fn(h, c0):
            dims = slice(64 * h, 64 * h + 64)
            return [(kl_ref[:, c0:c0 + 256], vtl_ref[0, dims, :], None),
                    (kc_ref[:, c0:c0 + 256], vtc_ref[0, dims, :], None)]

        ot = _mla_heads_t(lambda c0: qt_ref[0, c0:c0 + 256, cols], groups_fn, shift)
        o_ref[cols, :] = ot.T.astype(BF16)

    _latent_attention(shift_ref, bound_index, chunk)


def _diff_attn_kernel(shift_ref, qt_ref, kl_ref, vtl_ref, kc_ref, vtc_ref, lam_ref, gsubt_ref, o_ref, *, lam_init,
                      bound_index):
    def chunk(cols, shift):
        def groups_fn(h):
            dims = slice(64 * h, 64 * h + 64)
            return [(kl_ref[...], vtl_ref[0, dims, :], None), (kc_ref[...], vtc_ref[0, dims, :], None)]

        lam = _diff_lambda(lam_ref, lam_init)
        ot = _diff_heads_t(qt_ref[0, :, cols], groups_fn, lam, gsubt_ref[...], lam_init, shift)
        o_ref[cols, :] = ot.T.astype(BF16)

    _latent_attention(shift_ref, bound_index, chunk)


def _gqa_attn_kernel(shift_ref, qt_ref, kl_ref, vtl_ref, kc_ref, vtc_ref, o_ref, *, bound_index):
    def chunk(cols, shift):
        def groups_fn(n):
            dims = slice(64 * n, 64 * n + 64)
            return [(kl_ref[...], vtl_ref[0, dims, :], None), (kc_ref[...], vtc_ref[0, dims, :], None)]

        ot = _gqa_heads_t(qt_ref[0, :, cols], groups_fn, shift)
        o_ref[cols, :] = ot.T.astype(BF16)

    _latent_attention(shift_ref, bound_index, chunk)


def _latent_attn_call(kernel, name, bounds, bound_index, qt, kl, vtl, kc, vtc, extra, n_batch):
    wq, wk, wv = qt.shape[1], kl.shape[1], vtl.shape[1]
    tiles = SEQ // TQ
    return pl.pallas_call(
        functools.partial(kernel, bound_index=bound_index),
        grid=(n_batch, tiles),
        in_specs=[pl.BlockSpec(memory_space=pltpu.SMEM),
                  pl.BlockSpec((1, wq, TQ), lambda b, t: (b, 0, t)),
                  pl.BlockSpec((SEQ, wk), lambda b, t: (b, 0)),
                  pl.BlockSpec((1, wv, SEQ), lambda b, t: (b, 0, 0)),
                  pl.BlockSpec((CTX, wk), lambda b, t: (b, 0)),
                  pl.BlockSpec((1, wv, CTX), lambda b, t: (b, 0, 0))] + [_const_spec(a) for a in extra],
        out_specs=pl.BlockSpec((TQ, 256), lambda b, t: (b * tiles + t, 0)),
        out_shape=jax.ShapeDtypeStruct((n_batch * SEQ, 256), BF16),
        compiler_params=_cparams(2),
        name=name,
    )(bounds, qt, kl, vtl, kc, vtc, *map(_operand, extra))


def _na_window(t):
    return jnp.clip(t - 1, 0, SEQ // NA_TQ - NA_WIN)


def _na_build_bias(t, pair_ref, bias_ref):
    rows_per_tile = NA_TQ // GRID_W
    k_row0 = rows_per_tile * _na_window(t)
    lane = lax.broadcasted_iota(jnp.int32, (GRID_W, LANES), 1)
    for i in range(rows_per_tile // 2):
        qr = rows_per_tile * t + 2 * i
        r0 = jnp.clip(qr - NA_ROWS // 2, 0, N_ROWS - NA_ROWS)
        r1 = jnp.clip(qr + 1 - NA_ROWS // 2, 0, N_ROWS - NA_ROWS)
        for kl in range(NA_WIN * rows_per_tile):
            kr = k_row0 + kl
            ok0 = jnp.logical_and(kr >= r0, kr < r0 + NA_ROWS).astype(jnp.int32)
            ok1 = jnp.logical_and(kr >= r1, kr < r1 + NA_ROWS).astype(jnp.int32)
            ok = jnp.where(lane < GRID_W, ok0, ok1) > 0
            d = jnp.clip(kr - qr + NA_ROWS - 1, 0, 2 * NA_ROWS - 1)
            for h in range(4):
                blk = jnp.where(ok, pair_ref[h, d], MASK_VALUE)
                bias_ref[h, GRID_W * kl:GRID_W * (kl + 1), LANES * i:LANES * (i + 1)] = blk


def _na_attn_kernel(shift_ref, qt_ref, k0_ref, k1_ref, k2_ref, kc_ref, vt0_ref, vt1_ref, vt2_ref, vtc_ref, pair_ref,
                    o_ref, bias_ref, *, bound_index):
    t = pl.program_id(0)

    @pl.when(pl.program_id(1) == 0)
    def _():
        _na_build_bias(t, pair_ref, bias_ref)

    win = ((k0_ref, vt0_ref), (k1_ref, vt1_ref), (k2_ref, vt2_ref))

    def run(shift):
        for s in range(qt_ref.shape[0]):
            def groups_fn(h):
                dims = slice(64 * h, 64 * h + 64)
                return ([(k[s], vt[s, dims, :], bias_ref[h, NA_TQ * j:NA_TQ * (j + 1), :])
                         for j, (k, vt) in enumerate(win)] + [(kc_ref[s], vtc_ref[s, dims, :], None)])

            o_ref[s] = _plain_heads_t(qt_ref[s], groups_fn, shift).T.astype(BF16)

    bound = shift_ref[bound_index]

    @pl.when(bound <= MAX_FIXED_SHIFT)
    def _():
        run(bound)

    @pl.when(jnp.logical_not(bound <= MAX_FIXED_SHIFT))
    def _():
        run(None)


def _na_attn_call(bounds, bound_index, qt, kl, vtl, kc, vtc, pair, n_batch):
    tiles = SEQ // NA_TQ
    nb = math.gcd(NA_SAMPLES, n_batch)
    kl = kl.reshape(n_batch, SEQ, 256)
    kc = kc.reshape(n_batch, CTX, 256)
    k_spec = lambda j: pl.BlockSpec((nb, NA_TQ, 256), lambda t, b: (b, _na_window(t) + j, 0))
    vt_spec = lambda j: pl.BlockSpec((nb, 256, NA_TQ), lambda t, b: (b, 0, _na_window(t) + j))
    out = pl.pallas_call(
        functools.partial(_na_attn_kernel, bound_index=bound_index),
        grid=(tiles, n_batch // nb),
        in_specs=[pl.BlockSpec(memory_space=pltpu.SMEM),
                  pl.BlockSpec((nb, 256, NA_TQ), lambda t, b: (b, 0, t)),
                  k_spec(0), k_spec(1), k_spec(2),
                  pl.BlockSpec((nb, CTX, 256), lambda t, b: (b, 0, 0)),
                  vt_spec(0), vt_spec(1), vt_spec(2),
                  pl.BlockSpec((nb, 256, CTX), lambda t, b: (b, 0, 0)),
                  _const_spec(pair)],
        out_specs=pl.BlockSpec((nb, NA_TQ, 256), lambda t, b: (b, t, 0)),
        out_shape=jax.ShapeDtypeStruct((n_batch, SEQ, 256), BF16),
        scratch_shapes=[pltpu.VMEM((4, NA_WIN * NA_TQ, NA_TQ), F32)],
        compiler_params=_cparams(2),
        name="na_attn",
    )(bounds, qt, kl, kl, kl, kc, vtl, vtl, vtl, vtc, _operand(pair))
    return out.reshape(n_batch * SEQ, 256)


def _ctx_attn_kernel(mqt_ref, mk_ref, mvt_ref, nqt_ref, nk_ref, nvt_ref, dqt_ref, dk_ref, dvt_ref,
                     gqt_ref, gk_ref, gvt_ref, lam_ref, gsubt_ref, oa_ref, ob_ref, oc_ref, od_ref, *, lam_init):
    head_dims = lambda h: slice(64 * h, 64 * h + 64)
    lam = _diff_lambda(lam_ref, lam_init)
    for s in range(mqt_ref.shape[0]):
        oa_ref[s] = _mla_heads_t(lambda c0: mqt_ref[s, c0:c0 + 256, :],
                                 lambda h, c0: [(mk_ref[s, :, c0:c0 + 256], mvt_ref[s, head_dims(h), :], None)],
                                 None).T.astype(BF16)
        ob_ref[s] = _plain_heads_t(nqt_ref[s], lambda h: [(nk_ref[s], nvt_ref[s, head_dims(h), :], None)],
                                   None).T.astype(BF16)
        oc_ref[s] = _diff_heads_t(dqt_ref[s], lambda h: [(dk_ref[s], dvt_ref[s, head_dims(h), :], None)], lam,
                                  gsubt_ref[...], lam_init, None).T.astype(BF16)
        od_ref[s] = _gqa_heads_t(gqt_ref[s], lambda n: [(gk_ref[s], gvt_ref[s, head_dims(n), :], None)],
                                 None).T.astype(BF16)


def _ctx_attn_call(qkv_c, lam, gsub, lam_init, n_batch):
    nb = math.gcd(CTX_SAMPLES, n_batch)
    arrays = [a if a.ndim == 3 else a.reshape(n_batch, CTX, a.shape[1]) for a in qkv_c]
    specs = [pl.BlockSpec((nb,) + a.shape[1:], lambda b: (b, 0, 0)) for a in arrays]
    out = jax.ShapeDtypeStruct((n_batch, CTX, 256), BF16)
    outs = pl.pallas_call(
        functools.partial(_ctx_attn_kernel, lam_init=lam_init),
        grid=(n_batch // nb,),
        in_specs=specs + [_const_spec(lam), _const_spec(gsub)],
        out_specs=[pl.BlockSpec((nb, CTX, 256), lambda b: (b, 0, 0))] * 4,
        out_shape=[out] * 4,
        compiler_params=_cparams(1),
        name="ctx_attn",
    )(*arrays, _operand(lam), _operand(gsub))
    return [o.reshape(n_batch * CTX, 256) for o in outs]


def _merge_kernel(x_ref, mod_ref, g_ref, oa_ref, ob_ref, oc_ref, od_ref, wp_ref, wb_ref, wo_ref, out_ref):
    x = x_ref[...]
    mod = mod_ref[0]
    xn = _modnorm(x, g_ref[...], mod[:, 0:D], mod[:, D:2 * D]).astype(BF16)
    acc = jnp.zeros(x.shape, F32)
    for n, o_ref in enumerate((oa_ref, ob_ref, oc_ref, od_ref)):
        gate = jax.nn.sigmoid(_dot(xn, wp_ref[:, Z_W + n * D:Z_W + (n + 1) * D]))
        acc = acc + gate * _dot(o_ref[...], wb_ref[n])
    y = _dot(acc.astype(BF16), wo_ref[...])
    out_ref[...] = x + mod[:, 2 * D:3 * D] * y


def _merge_call(rows, mods, layer, mod_row, g1, branches, w_proj, w_branch, w_out, tm):
    n_rows = rows.shape[0]
    tm = min(tm, n_rows)
    tok = lambda w: pl.BlockSpec((tm, w), lambda i: (i, 0))
    return pl.pallas_call(
        _merge_kernel,
        grid=(n_rows // tm,),
        in_specs=[tok(D), _mod_spec(layer, mod_row), _const_spec(g1)]
                 + [tok(256)] * 4 + [_const_spec(w_proj), _const_spec(w_branch), _const_spec(w_out)],
        out_specs=tok(D),
        out_shape=jax.ShapeDtypeStruct((n_rows, D), F32),
        compiler_params=_cparams(1),
        name="merge",
    )(rows, mods, _operand(g1), *branches, _operand(w_proj), _operand(w_branch), _operand(w_out))


def _mlp_kernel(x_ref, mod_ref, g_ref, wu_ref, wd_ref, out_ref):
    x = x_ref[...]
    mod = mod_ref[0]
    xn = _modnorm(x, g_ref[...], mod[:, 3 * D:4 * D], mod[:, 4 * D:5 * D]).astype(BF16)
    h = jnp.square(jnp.maximum(_dot(xn, wu_ref[...]), 0.0))
    out_ref[...] = x + mod[:, 5 * D:6 * D] * _dot(h.astype(BF16), wd_ref[...])


def _mlp_call(rows, mods, layer, mod_row, g2, w_up, w_down, tm):
    n_rows = rows.shape[0]
    tm = min(tm, n_rows)
    tok = lambda w: pl.BlockSpec((tm, w), lambda i: (i, 0))
    return pl.pallas_call(
        _mlp_kernel,
        grid=(n_rows // tm,),
        in_specs=[tok(D), _mod_spec(layer, mod_row), _const_spec(g2),
                  _const_spec(w_up), _const_spec(w_down)],
        out_specs=tok(D),
        out_shape=jax.ShapeDtypeStruct((n_rows, D), F32),
        compiler_params=_cparams(1),
        name="mlp",
    )(rows, mods, _operand(g2), _operand(w_up), _operand(w_down))


def _block_diag_ones(n, seg):
    i = jnp.arange(n) // seg
    return (i[:, None] == i[None, :]).astype(BF16)


def _rope_parts(rot_dim):
    t = jnp.arange(SEQ)
    rows, cols = t // GRID_W, t % GRID_W
    n = rot_dim // 4
    inv_freq = jnp.power(ROPE_THETA, -jnp.arange(n, dtype=F32) / n)
    ang_r = rows.astype(F32)[:, None] * inv_freq
    ang_c = cols.astype(F32)[:, None] * inv_freq
    ang = jnp.concatenate([ang_r, ang_r, ang_c, ang_c], axis=-1)
    cos, sin = jnp.cos(ang), jnp.sin(ang)
    even = (jnp.arange(rot_dim) // n) % 2 == 0
    return cos, jnp.where(even, -sin, 0.0), jnp.where(even, 0.0, sin)


def _rope_tables():
    r32, r64 = _rope_parts(MLA_ROPE), _rope_parts(HEAD_DIM)
    ones = jnp.ones((SEQ, MLA_NOPE), F32)
    zeros_n = jnp.zeros((SEQ, MLA_NOPE), F32)
    pad = jnp.zeros((SEQ, LANES - MLA_QK), F32)
    mla = [jnp.concatenate([lead, part, pad], axis=1) for lead, part in zip((ones, zeros_n, zeros_n), r32)]
    return jnp.stack(mla + [jnp.tile(a, (1, LANES // MLA_ROPE)) for a in r32]
                     + [jnp.tile(a, (1, LANES // HEAD_DIM)) for a in r64])


def _na_pair_table(rpb):
    kc = jnp.arange(GRID_W)[:, None]
    qc = jnp.arange(GRID_W)[None, :]
    onehot = (kc - qc + NA_COLS - 1)[None] == jnp.arange(2 * NA_COLS - 1)[:, None, None]
    toeplitz = jnp.einsum("lhrd,dkq->lhrkq", rpb, onehot.astype(F32), precision=lax.Precision.HIGHEST)
    c0 = jnp.clip(qc - NA_COLS // 2, 0, GRID_W - NA_COLS)
    in_window = (kc >= c0) & (kc < c0 + NA_COLS)
    masked = jnp.where(in_window, toeplitz * LOG2E, MASK_VALUE)
    ext = jnp.pad(masked, ((0, 0), (0, 0), (1, 1), (0, 0), (0, 0)), constant_values=MASK_VALUE)
    return jnp.concatenate([ext[:, :, 1:], ext[:, :, :-1]], axis=-1)


def _prepare_params(p):
    w_in = p["w_in"]
    n_layers = w_in.shape[0]
    zcol = lambda n: jnp.zeros((n_layers, D, n), F32)
    kr = w_in[:, :, C_KR:C_KR + MLA_ROPE]
    kr_slots = jnp.concatenate([zcol(MLA_NOPE), kr, zcol(LANES - MLA_QK)] * 4, axis=2)
    head_order = jnp.array([0, 2, 1, 3])
    gq_cols = w_in[:, :, C_GQ:C_GQ + 256].reshape(n_layers, D, 4, HEAD_DIM)[:, :, head_order].reshape(n_layers, D, 256)
    w_proj = jnp.concatenate([
        w_in[:, :, C_CQ:C_CQ + MLA_Q_LORA], zcol(256 - MLA_Q_LORA),
        w_in[:, :, C_CKV:C_CKV + MLA_KV_LORA],
        kr_slots,
        w_in[:, :, C_NA:C_NA + 768],
        w_in[:, :, C_DF:C_DF + 768],
        gq_cols, w_in[:, :, C_GQ + 256:C_GQ + 512],
        w_in[:, :, C_GATE:]], axis=2).astype(BF16)

    w_uq = p["w_mla_uq"].reshape(n_layers, MLA_Q_LORA, 4, MLA_QK)
    w_uq = jnp.pad(w_uq, ((0, 0), (0, 256 - MLA_Q_LORA), (0, 0), (0, LANES - MLA_QK)))
    w_ukv = p["w_mla_ukv"].reshape(n_layers, MLA_KV_LORA, 4, 2 * MLA_NOPE)
    w_uk = jnp.pad(w_ukv[..., :MLA_NOPE], ((0, 0), (0, 0), (0, 0), (0, LANES - MLA_NOPE)))

    mla_gain = lambda g: jnp.tile(jnp.pad(g, ((0, 0), (0, LANES - MLA_QK))), (1, 4))
    rep = lambda g, n: jnp.tile(g, (1, n))
    row = lambda g: jnp.pad(g, ((0, 0), (0, 512 - g.shape[1])))
    hd_scale = HEAD_DIM ** -0.5 * LOG2E
    gains = jnp.stack([
        row(mla_gain(p["g_mla_q"]) * (MLA_QK ** -0.5 * LOG2E)), row(mla_gain(p["g_mla_k"])),
        row(rep(p["g_na_q"], 4) * hd_scale), row(rep(p["g_na_k"], 4)),
        row(rep(p["g_diff_q"], 8) * (DIFF_DIM ** -0.5 * LOG2E)), row(rep(p["g_diff_k"], 8)),
        row(rep(p["g_gqa_q"], 4) * hd_scale), row(rep(p["g_gqa_k"], 2))], axis=1)

    def score_bound(dim, g_q, g_k):
        return LOG2E * dim ** 0.5 * jnp.max(jnp.abs(g_q), axis=-1) * jnp.max(jnp.abs(g_k), axis=-1)

    bounds = jnp.stack([
        score_bound(MLA_QK, p["g_mla_q"], p["g_mla_k"]),
        score_bound(HEAD_DIM, p["g_na_q"], p["g_na_k"]) + LOG2E * jnp.max(jnp.abs(p["na_rpb"]), axis=(1, 2, 3)),
        score_bound(DIFF_DIM, p["g_diff_q"], p["g_diff_k"]),
        score_bound(HEAD_DIM, p["g_gqa_q"], p["g_gqa_k"])], axis=1).astype(F32).reshape(n_layers * N_BOUNDS)

    w_branch = p["w_branch"]
    wb_gqa = w_branch[:, 3].reshape(n_layers, 4, HEAD_DIM, D)[:, head_order].reshape(n_layers, 1, 256, D)
    w_branch = jnp.concatenate([w_branch[:, :3], wb_gqa], axis=1).astype(BF16)

    return dict(
        g1=p["g_norm1"][:, None], g2=p["g_norm2"][:, None],
        w_proj=w_proj,
        g_qa=jnp.pad(p["g_mla_qa"], ((0, 0), (0, 256 - MLA_Q_LORA)))[:, None], g_kva=p["g_mla_kva"][:, None],
        w_uq=w_uq.reshape(n_layers, 256, 512).astype(BF16), w_uk=w_uk.reshape(n_layers, MLA_KV_LORA, 512).astype(BF16),
        w_uv=w_ukv[..., MLA_NOPE:].reshape(n_layers, MLA_KV_LORA, 256).astype(BF16),
        gains=gains, bounds=bounds, na_pair=_na_pair_table(p["na_rpb"]),
        lam=jnp.stack([p["diff_lq1"], p["diff_lk1"], p["diff_lq2"], p["diff_lk2"]], axis=1),
        g_sub=rep(p["g_diff_sub"], 4)[:, :, None],
        w_branch=w_branch, w_out=p["w_out"].astype(BF16),
        w_up=p["w_up"].astype(BF16), w_down=p["w_down"].astype(BF16),
    )


def kernel(x, c, ctx, c_ctx, w_ada, b_ada, g_norm1, g_norm2, w_in, g_mla_qa, w_mla_uq, g_mla_kva, w_mla_ukv, g_mla_q, g_mla_k, g_na_q, g_na_k, na_rpb, g_diff_q, g_diff_k, diff_lq1, diff_lk1, diff_lq2, diff_lk2, g_diff_sub, g_gqa_q, g_gqa_k, w_branch, w_out, w_up, w_down):
    p = dict(w_in=w_in, g_norm1=g_norm1, g_norm2=g_norm2, g_mla_qa=g_mla_qa, w_mla_uq=w_mla_uq,
             g_mla_kva=g_mla_kva, w_mla_ukv=w_mla_ukv, g_mla_q=g_mla_q, g_mla_k=g_mla_k,
             g_na_q=g_na_q, g_na_k=g_na_k, na_rpb=na_rpb, g_diff_q=g_diff_q, g_diff_k=g_diff_k,
             diff_lq1=diff_lq1, diff_lk1=diff_lk1, diff_lq2=diff_lq2, diff_lk2=diff_lk2,
             g_diff_sub=g_diff_sub, g_gqa_q=g_gqa_q, g_gqa_k=g_gqa_k, w_branch=w_branch,
             w_out=w_out, w_up=w_up, w_down=w_down)
    n_batch = x.shape[0]
    depth = w_ada.shape[0]
    assert x.shape[1:] == (SEQ, D) and ctx.shape[1:] == (CTX, D)

    mod_rows = -(-(n_batch + 1) // 8) * 8
    c_all = jnp.concatenate([c, c_ctx[None], jnp.zeros((mod_rows - n_batch - 1, D), F32)], axis=0)
    mods = _ada_call(c_all, w_ada, b_ada).reshape(depth, mod_rows, 1, 6 * D)
    lat_tiles = SEQ // TM
    lat_row = lambda i: i // lat_tiles
    ctx_row = lambda i: n_batch

    consts = dict(bd128=_block_diag_ones(256, 128), bd64=_block_diag_ones(256, 64),
                  bd32=_block_diag_ones(256, 32), bd64h=_block_diag_ones(128, 64))
    tabs = _rope_tables()
    stacked = _prepare_params(p)
    bounds = stacked.pop("bounds")

    xl = x.reshape(n_batch * SEQ, D)
    xc = ctx.reshape(n_batch * CTX, D)
    for l in range(depth):
        need_ctx = l < depth - 1
        lam_init = 0.8 - 0.6 * math.exp(-0.3 * l)
        lw = {name: _Layer(a, l) for name, a in stacked.items()}
        bound = lambda which: N_BOUNDS * l + which

        lat = _qkv_call(xl, mods, l, lambda b: b, lw, consts, tabs, n_batch, SEQ, TM)
        cx = _qkv_call(xc, mods, l, ctx_row, lw, consts, None, n_batch, CTX, CTX)
        mqt, mk, mvt, nqt, nk, nvt, dqt, dk, dvt, gqt, gk, gvt = lat
        oa = _latent_attn_call(_mla_attn_kernel, "mla_attn", bounds, bound(B_MLA), mqt, mk, mvt, cx[1], cx[2], [],
                               n_batch)
        ob = _na_attn_call(bounds, bound(B_NA), nqt, nk, nvt, cx[4], cx[5], lw["na_pair"], n_batch)
        oc = _latent_attn_call(functools.partial(_diff_attn_kernel, lam_init=lam_init), "diff_attn",
                               bounds, bound(B_DIFF), dqt, dk, dvt, cx[7], cx[8], [lw["lam"], lw["g_sub"]], n_batch)
        od = _latent_attn_call(_gqa_attn_kernel, "gqa_attn", bounds, bound(B_GQA), gqt, gk, gvt, cx[10], cx[11], [],
                               n_batch)
        if need_ctx:
            oc_all = _ctx_attn_call(cx, lw["lam"], lw["g_sub"], lam_init, n_batch)
            xc = _merge_call(xc, mods, l, ctx_row, lw["g1"], oc_all, lw["w_proj"], lw["w_branch"], lw["w_out"], TM)
            xc = _mlp_call(xc, mods, l, ctx_row, lw["g2"], lw["w_up"], lw["w_down"], TM)
        xl = _merge_call(xl, mods, l, lat_row, lw["g1"], (oa, ob, oc, od), lw["w_proj"], lw["w_branch"], lw["w_out"], TM)
        xl = _mlp_call(xl, mods, l, lat_row, lw["g2"], lw["w_up"], lw["w_down"], TM)
    return xl.reshape(n_batch, SEQ, D)
```

```python
import functools
import math
from typing import NamedTuple

import jax
import jax.numpy as jnp
from jax import lax
from jax.experimental import pallas as pl
from jax.experimental.pallas import tpu as pltpu

F32 = jnp.float32
BF16 = jnp.bfloat16

D = 1024
SEQ = 2048
CTX = 256
GRID_W = 64
N_ROWS = SEQ // GRID_W
ROPE_THETA = 10000.0
EPS = 1e-6
HEAD_DIM = 64
MLA_Q_LORA = 192
MLA_KV_LORA = 128
MLA_NOPE = 64
MLA_ROPE = 32
MLA_QK = MLA_NOPE + MLA_ROPE
NA_ROWS = 8
NA_COLS = 16
DIFF_DIM = 32
D_FF = 4 * D

LANES = 128
TM = 512
TM_MERGE = 1024
TQ = 2048
QC = 1024
NA_TQ = 256
NA_WIN = 3
NA_SAMPLES = 4
CTX_SAMPLES = 4
MASK_VALUE = -1e30
LOG2E = math.log2(math.e)
MAX_FIXED_SHIFT = 50.0

C_CQ, C_CKV, C_KR, C_NA, C_DF, C_GQ, C_GATE = 0, 192, 320, 352, 1120, 1888, 2400
Z_CQ, Z_CKV, Z_KR, Z_NA, Z_DF, Z_GQ, Z_W = 0, 256, 384, 896, 1664, 2432, 2944
G_MQ, G_MK, G_NQ, G_NK, G_DQ, G_DK, G_GQ, G_GK = range(8)
T_MLA, T_R32, T_R64 = 0, 3, 6
B_MLA, B_NA, B_DIFF, B_GQA, N_BOUNDS = 0, 1, 2, 3, 4

VMEM_LIMIT = 56 * 1024 * 1024


def _cparams(n_axes):
    return pltpu.CompilerParams(dimension_semantics=("arbitrary",) * n_axes,
                                vmem_limit_bytes=VMEM_LIMIT)


def _dot(a, b):
    return jnp.dot(a, b, preferred_element_type=F32)


class _Layer(NamedTuple):
    stacked: jax.Array
    index: int


def _operand(a):
    return a.stacked if isinstance(a, _Layer) else a


def _const_spec(a):
    if isinstance(a, _Layer):
        shape = a.stacked.shape[1:]
        return pl.BlockSpec((None,) + shape, lambda *_: (a.index,) + (0,) * len(shape),
                            pipeline_mode=pl.Buffered(1))
    return pl.BlockSpec(a.shape, lambda *_: (0,) * a.ndim, pipeline_mode=pl.Buffered(1))


def _mod_spec(layer, row_fn):
    return pl.BlockSpec((None, 1, 1, 6 * D), lambda *idx: (layer, row_fn(*idx), 0, 0))


def _modnorm(x, g, shift, scale):
    y = x * lax.rsqrt(jnp.mean(x * x, axis=-1, keepdims=True) + EPS) * g
    return y * (1.0 + scale) + shift


def _ada_kernel(c_ref, w_ref, b_ref, o_ref):
    c = c_ref[...]
    s = c * jax.nn.sigmoid(c)
    o_ref[0] = _dot(s.astype(BF16), w_ref[0].astype(BF16)) + b_ref[0]


def _ada_call(c_all, w_ada, b_ada):
    n_layers = w_ada.shape[0]
    rows = c_all.shape[0]
    bn = 1536
    return pl.pallas_call(
        _ada_kernel,
        grid=(n_layers, 6 * D // bn),
        in_specs=[pl.BlockSpec((rows, D), lambda l, j: (0, 0)),
                  pl.BlockSpec((1, D, bn), lambda l, j: (l, 0, j)),
                  pl.BlockSpec((1, 1, bn), lambda l, j: (l, 0, j))],
        out_specs=pl.BlockSpec((1, rows, bn), lambda l, j: (l, 0, j)),
        out_shape=jax.ShapeDtypeStruct((n_layers, rows, 6 * D), F32),
        compiler_params=_cparams(2),
        name="ada",
    )(c_all, w_ada, b_ada.reshape(n_layers, 1, 6 * D))


def _seg_mean_sq(t, bd, inv_d):
    x2 = t * t
    hi = x2.astype(BF16)
    lo = (x2 - hi.astype(F32)).astype(BF16)
    return (_dot(hi, bd) + _dot(lo, bd)) * inv_d


def _qkv_kernel(*refs, rope):
    (x_ref, mod_ref, g1_ref, w_ref, bd128_ref, bd64_ref, bd32_ref, bd64h_ref,
     gqa_ref, gkva_ref, wuq_ref, wuk_ref, wuv_ref, gains_ref) = refs[:14]
    tab_ref = refs[14] if rope else None
    (mqt_ref, mk_ref, mvt_ref, nqt_ref, nk_ref, nvt_ref,
     dqt_ref, dk_ref, dvt_ref, gqt_ref, gk_ref, gvt_ref) = refs[-12:]

    mod = mod_ref[0]
    xn = _modnorm(x_ref[...], g1_ref[...], mod[:, 0:D], mod[:, D:2 * D]).astype(BF16)

    def cols(lo, hi):
        return _dot(xn, w_ref[:, lo:hi])

    def put(o_ref, lo, val):
        o_ref[:, lo:lo + LANES] = val.astype(BF16)

    def put_t(o_ref, lo, val):
        o_ref[0, lo:lo + LANES, :] = val.T.astype(BF16)

    def section(t, bd, inv_d, gain_row, tab, shift, o_ref, store):
        width = t.shape[1]
        y = t * lax.rsqrt(_seg_mean_sq(t, bd, inv_d) + EPS) * gains_ref[gain_row:gain_row + 1, 0:width]
        for j in range(width // LANES):
            yc = y[:, j * LANES:(j + 1) * LANES]
            if rope and tab is not None:
                yc = (yc * tab_ref[tab] + pltpu.roll(yc, LANES - shift, 1) * tab_ref[tab + 1]
                      + pltpu.roll(yc, shift, 1) * tab_ref[tab + 2])
            store(o_ref, j * LANES, yc)

    za = cols(Z_CQ, Z_NA)
    cq = za[:, Z_CQ:Z_CQ + 256]
    cqn = cq * lax.rsqrt(jnp.sum(cq * cq, axis=-1, keepdims=True) * (1.0 / MLA_Q_LORA) + EPS) * gqa_ref[...]
    q = _dot(cqn.astype(BF16), wuq_ref[...])
    ckv = za[:, Z_CKV:Z_CKV + 128]
    kvn = (ckv * lax.rsqrt(jnp.mean(ckv * ckv, axis=-1, keepdims=True) + EPS) * gkva_ref[...]).astype(BF16)
    k = _dot(kvn, wuk_ref[...]) + za[:, Z_KR:Z_KR + 512]
    mvt_ref[0] = _dot(kvn, wuv_ref[...]).T.astype(BF16)
    bd128 = bd128_ref[...]
    for c in range(2):
        sl = slice(256 * c, 256 * c + 256)
        section(q[:, sl], bd128, 1.0 / MLA_QK, G_MQ, T_MLA, MLA_ROPE // 4, mqt_ref.at[:, sl, :], put_t)
        section(k[:, sl], bd128, 1.0 / MLA_QK, G_MK, T_MLA, MLA_ROPE // 4, mk_ref.at[:, sl], put)

    zn = cols(Z_NA, Z_DF)
    bd64 = bd64_ref[...]
    section(zn[:, 0:256], bd64, 1.0 / HEAD_DIM, G_NQ, None, 0, nqt_ref, put_t)
    section(zn[:, 256:512], bd64, 1.0 / HEAD_DIM, G_NK, None, 0, nk_ref, put)
    nvt_ref[0] = zn[:, 512:768].T.astype(BF16)

    zd = cols(Z_DF, Z_GQ)
    bd32 = bd32_ref[...]
    section(zd[:, 0:256], bd32, 1.0 / DIFF_DIM, G_DQ, T_R32, DIFF_DIM // 4, dqt_ref, put_t)
    section(zd[:, 256:512], bd32, 1.0 / DIFF_DIM, G_DK, T_R32, DIFF_DIM // 4, dk_ref, put)
    dvt_ref[0] = zd[:, 512:768].T.astype(BF16)

    zg = cols(Z_GQ, Z_W)
    section(zg[:, 0:256], bd64, 1.0 / HEAD_DIM, G_GQ, T_R64, HEAD_DIM // 4, gqt_ref, put_t)
    section(zg[:, 256:384], bd64h_ref[...], 1.0 / HEAD_DIM, G_GK, T_R64, HEAD_DIM // 4, gk_ref, put)
    gvt_ref[0] = zg[:, 384:512].T.astype(BF16)


def _qkv_call(rows, mods, layer, mod_row, lw, consts, tabs, n_batch, seq, tm):
    tiles = seq // tm
    row_map = lambda t, b: (b * tiles + t, 0)
    tok_spec = lambda w: pl.BlockSpec((tm, w), row_map)
    tr_spec = lambda w: pl.BlockSpec((1, w, tm), lambda t, b: (b, 0, t))
    tok_shape = lambda w: jax.ShapeDtypeStruct((n_batch * seq, w), BF16)
    tr_shape = lambda w: jax.ShapeDtypeStruct((n_batch, w, seq), BF16)
    small = [lw["g1"], lw["w_proj"], consts["bd128"], consts["bd64"], consts["bd32"], consts["bd64h"],
             lw["g_qa"], lw["g_kva"], lw["w_uq"], lw["w_uk"], lw["w_uv"], lw["gains"]]
    in_specs = [tok_spec(D), _mod_spec(layer, lambda t, b: mod_row(b))]
    in_specs += [_const_spec(a) for a in small]
    args = [rows, mods] + [_operand(a) for a in small]
    if tabs is not None:
        in_specs.append(pl.BlockSpec((tabs.shape[0], tm, LANES), lambda t, b: (0, t, 0)))
        args.append(tabs)
    return pl.pallas_call(
        functools.partial(_qkv_kernel, rope=tabs is not None),
        grid=(tiles, n_batch),
        in_specs=in_specs,
        out_specs=[tr_spec(512), tok_spec(512), tr_spec(256),
                   tr_spec(256), tok_spec(256), tr_spec(256),
                   tr_spec(256), tok_spec(256), tr_spec(256),
                   tr_spec(256), tok_spec(128), tr_spec(128)],
        out_shape=[tr_shape(512), tok_shape(512), tr_shape(256),
                   tr_shape(256), tok_shape(256), tr_shape(256),
                   tr_shape(256), tok_shape(256), tr_shape(256),
                   tr_shape(256), tok_shape(128), tr_shape(128)],
        compiler_params=_cparams(2),
        name="qkv",
    )(*args)


def _keep_rows(x, lo, hi):
    row = lax.broadcasted_iota(jnp.int32, x.shape, 0)
    return jnp.where(jnp.logical_and(row >= lo, row < hi), x, jnp.zeros_like(x))


def _sum_all(xs):
    return functools.reduce(lambda a, b: a + b, xs)


def _exp_scores(qt, groups, shift):
    s = [_dot(k, qt) if bias is None else _dot(k, qt) + bias for k, _, bias in groups]
    if shift is None:
        shift = functools.reduce(jnp.maximum, [jnp.max(x, axis=0, keepdims=True) for x in s])
    p = [jnp.exp2(x - shift) for x in s]
    return [x.astype(BF16) for x in p], _sum_all([jnp.sum(x, axis=0, keepdims=True) for x in p])


def _softmax_vt(qt, groups, shift):
    p, l = _exp_scores(qt, groups, shift)
    return _sum_all([_dot(g[1], x) for x, g in zip(p, groups)]) / l


def _mla_heads_t(qt_fn, groups_fn, shift):
    outs = []
    for h in range(4):
        c0 = 256 * (h // 2)
        lo = LANES * (h % 2)
        outs.append(_softmax_vt(_keep_rows(qt_fn(c0), lo, lo + LANES), groups_fn(h, c0), shift))
    return jnp.concatenate(outs, axis=0)


def _diff_heads_t(qt, groups_fn, lam, gsub_t, lam_init, shift):
    outs = []
    for h in range(4):
        groups = groups_fn(h)
        (p1, l1), (p2, l2) = [_exp_scores(_keep_rows(qt, 64 * h + DIFF_DIM * m, 64 * h + DIFF_DIM * (m + 1)),
                                          groups, shift) for m in range(2)]
        w1 = (1.0 / l1).astype(BF16)
        w2 = (lam / l2).astype(BF16)
        oh = _sum_all([_dot(g[1], a * w1 - b * w2) for a, b, g in zip(p1, p2, groups)])
        ms = jnp.mean(oh * oh, axis=0, keepdims=True)
        outs.append(oh * lax.rsqrt(ms + EPS) * gsub_t[64 * h:64 * h + 64, :])
    return jnp.concatenate(outs, axis=0) * (1.0 - lam_init)


def _gqa_heads_t(qt, groups_fn, shift):
    outs = []
    for g in range(2):
        qc = qt[LANES * g:LANES * (g + 1), :]
        for n in range(2):
            outs.append(_softmax_vt(_keep_rows(qc, 64 * n, 64 * n + 64), groups_fn(n), shift))
    return jnp.concatenate(outs, axis=0)


def _plain_heads_t(qt, groups_fn, shift):
    return jnp.concatenate([_softmax_vt(_keep_rows(qt, 64 * h, 64 * h + 64), groups_fn(h), shift) for h in range(4)],
                           axis=0)


def _diff_lambda(lam_ref, lam_init):
    lq1, lk1, lq2, lk2 = (lam_ref[i:i + 1, :] for i in range(4))
    return (jnp.exp(jnp.sum(lq1 * lk1, axis=-1, keepdims=True))
            - jnp.exp(jnp.sum(lq2 * lk2, axis=-1, keepdims=True)) + lam_init)


def _latent_attention(shift_ref, bound_index, chunk_fn):
    bound = shift_ref[bound_index]

    def run(shift):
        def step(c, carry):
            chunk_fn(pl.ds(pl.multiple_of(c * QC, QC), QC), shift)
            return carry

        lax.fori_loop(0, TQ // QC, step, 0)

    @pl.when(bound <= MAX_FIXED_SHIFT)
    def _():
        run(bound)

    @pl.when(jnp.logical_not(bound <= MAX_FIXED_SHIFT))
    def _():
        run(None)


def _mla_attn_kernel(shift_ref, qt_ref, kl_ref, vtl_ref, kc_ref, vtc_ref, o_ref, *, bound_index):
    def chunk(cols, shift):
        def groups_fn(h, c0):
            dims = slice(64 * h, 64 * h + 64)
            return [(kl_ref[:, c0:c0 + 256], vtl_ref[0, dims, :], None),
                    (kc_ref[:, c0:c0 + 256], vtc_ref[0, dims, :], None)]

        ot = _mla_heads_t(lambda c0: qt_ref[0, c0:c0 + 256, cols], groups_fn, shift)
        o_ref[cols, :] = ot.T.astype(BF16)

    _latent_attention(shift_ref, bound_index, chunk)


def _diff_attn_kernel(shift_ref, qt_ref, kl_ref, vtl_ref, kc_ref, vtc_ref, lam_ref, gsubt_ref, o_ref, *, lam_init,
                      bound_index):
    def chunk(cols, shift):
        def groups_fn(h):
            dims = slice(64 * h, 64 * h + 64)
            return [(kl_ref[...], vtl_ref[0, dims, :], None), (kc_ref[...], vtc_ref[0, dims, :], None)]

        lam = _diff_lambda(lam_ref, lam_init)
        ot = _diff_heads_t(qt_ref[0, :, cols], groups_fn, lam, gsubt_ref[...], lam_init, shift)
        o_ref[cols, :] = ot.T.astype(BF16)

    _latent_attention(shift_ref, bound_index, chunk)


def _gqa_attn_kernel(shift_ref, qt_ref, kl_ref, vtl_ref, kc_ref, vtc_ref, o_ref, *, bound_index):
    def chunk(cols, shift):
        def groups_fn(n):
            dims = slice(64 * n, 64 * n + 64)
            return [(kl_ref[...], vtl_ref[0, dims, :], None), (kc_ref[...], vtc_ref[0, dims, :], None)]

        ot = _gqa_heads_t(qt_ref[0, :, cols], groups_fn, shift)
        o_ref[cols, :] = ot.T.astype(BF16)

    _latent_attention(shift_ref, bound_index, chunk)


def _latent_attn_call(kernel, name, bounds, bound_index, qt, kl, vtl, kc, vtc, extra, n_batch):
    wq, wk, wv = qt.shape[1], kl.shape[1], vtl.shape[1]
    tiles = SEQ // TQ
    return pl.pallas_call(
        functools.partial(kernel, bound_index=bound_index),
        grid=(n_batch, tiles),
        in_specs=[pl.BlockSpec(memory_space=pltpu.SMEM),
                  pl.BlockSpec((1, wq, TQ), lambda b, t: (b, 0, t)),
                  pl.BlockSpec((SEQ, wk), lambda b, t: (b, 0)),
                  pl.BlockSpec((1, wv, SEQ), lambda b, t: (b, 0, 0)),
                  pl.BlockSpec((CTX, wk), lambda b, t: (b, 0)),
                  pl.BlockSpec((1, wv, CTX), lambda b, t: (b, 0, 0))] + [_const_spec(a) for a in extra],
        out_specs=pl.BlockSpec((TQ, 256), lambda b, t: (b * tiles + t, 0)),
        out_shape=jax.ShapeDtypeStruct((n_batch * SEQ, 256), BF16),
        compiler_params=_cparams(2),
        name=name,
    )(bounds, qt, kl, vtl, kc, vtc, *map(_operand, extra))


def _na_window(t):
    return jnp.clip(t - 1, 0, SEQ // NA_TQ - NA_WIN)


def _na_build_bias(t, pair_ref, bias_ref):
    rows_per_tile = NA_TQ // GRID_W
    k_row0 = rows_per_tile * _na_window(t)
    lane = lax.broadcasted_iota(jnp.int32, (GRID_W, LANES), 1)
    for i in range(rows_per_tile // 2):
        qr = rows_per_tile * t + 2 * i
        r0 = jnp.clip(qr - NA_ROWS // 2, 0, N_ROWS - NA_ROWS)
        r1 = jnp.clip(qr + 1 - NA_ROWS // 2, 0, N_ROWS - NA_ROWS)
        for kl in range(NA_WIN * rows_per_tile):
            kr = k_row0 + kl
            ok0 = jnp.logical_and(kr >= r0, kr < r0 + NA_ROWS).astype(jnp.int32)
            ok1 = jnp.logical_and(kr >= r1, kr < r1 + NA_ROWS).astype(jnp.int32)
            ok = jnp.where(lane < GRID_W, ok0, ok1) > 0
            d = jnp.clip(kr - qr + NA_ROWS - 1, 0, 2 * NA_ROWS - 1)
            for h in range(4):
                blk = jnp.where(ok, pair_ref[h, d], MASK_VALUE)
                bias_ref[h, GRID_W * kl:GRID_W * (kl + 1), LANES * i:LANES * (i + 1)] = blk


def _na_attn_kernel(shift_ref, qt_ref, k0_ref, k1_ref, k2_ref, kc_ref, vt0_ref, vt1_ref, vt2_ref, vtc_ref, pair_ref,
                    o_ref, bias_ref, *, bound_index):
    t = pl.program_id(0)

    @pl.when(pl.program_id(1) == 0)
    def _():
        _na_build_bias(t, pair_ref, bias_ref)

    win = ((k0_ref, vt0_ref), (k1_ref, vt1_ref), (k2_ref, vt2_ref))

    def run(shift):
        for s in range(qt_ref.shape[0]):
            def groups_fn(h):
                dims = slice(64 * h, 64 * h + 64)
                return ([(k[s], vt[s, dims, :], bias_ref[h, NA_TQ * j:NA_TQ * (j + 1), :])
                         for j, (k, vt) in enumerate(win)] + [(kc_ref[s], vtc_ref[s, dims, :], None)])

            o_ref[s] = _plain_heads_t(qt_ref[s], groups_fn, shift).T.astype(BF16)

    bound = shift_ref[bound_index]

    @pl.when(bound <= MAX_FIXED_SHIFT)
    def _():
        run(bound)

    @pl.when(jnp.logical_not(bound <= MAX_FIXED_SHIFT))
    def _():
        run(None)


def _na_attn_call(bounds, bound_index, qt, kl, vtl, kc, vtc, pair, n_batch):
    tiles = SEQ // NA_TQ
    nb = math.gcd(NA_SAMPLES, n_batch)
    kl = kl.reshape(n_batch, SEQ, 256)
    kc = kc.reshape(n_batch, CTX, 256)
    k_spec = lambda j: pl.BlockSpec((nb, NA_TQ, 256), lambda t, b: (b, _na_window(t) + j, 0))
    vt_spec = lambda j: pl.BlockSpec((nb, 256, NA_TQ), lambda t, b: (b, 0, _na_window(t) + j))
    out = pl.pallas_call(
        functools.partial(_na_attn_kernel, bound_index=bound_index),
        grid=(tiles, n_batch // nb),
        in_specs=[pl.BlockSpec(memory_space=pltpu.SMEM),
                  pl.BlockSpec((nb, 256, NA_TQ), lambda t, b: (b, 0, t)),
                  k_spec(0), k_spec(1), k_spec(2),
                  pl.BlockSpec((nb, CTX, 256), lambda t, b: (b, 0, 0)),
                  vt_spec(0), vt_spec(1), vt_spec(2),
                  pl.BlockSpec((nb, 256, CTX), lambda t, b: (b, 0, 0)),
                  _const_spec(pair)],
        out_specs=pl.BlockSpec((nb, NA_TQ, 256), lambda t, b: (b, t, 0)),
        out_shape=jax.ShapeDtypeStruct((n_batch, SEQ, 256), BF16),
        scratch_shapes=[pltpu.VMEM((4, NA_WIN * NA_TQ, NA_TQ), F32)],
        compiler_params=_cparams(2),
        name="na_attn",
    )(bounds, qt, kl, kl, kl, kc, vtl, vtl, vtl, vtc, _operand(pair))
    return out.reshape(n_batch * SEQ, 256)


def _ctx_attn_kernel(mqt_ref, mk_ref, mvt_ref, nqt_ref, nk_ref, nvt_ref, dqt_ref, dk_ref, dvt_ref,
                     gqt_ref, gk_ref, gvt_ref, lam_ref, gsubt_ref, oa_ref, ob_ref, oc_ref, od_ref, *, lam_init):
    head_dims = lambda h: slice(64 * h, 64 * h + 64)
    lam = _diff_lambda(lam_ref, lam_init)
    for s in range(mqt_ref.shape[0]):
        oa_ref[s] = _mla_heads_t(lambda c0: mqt_ref[s, c0:c0 + 256, :],
                                 lambda h, c0: [(mk_ref[s, :, c0:c0 + 256], mvt_ref[s, head_dims(h), :], None)],
                                 None).T.astype(BF16)
        ob_ref[s] = _plain_heads_t(nqt_ref[s], lambda h: [(nk_ref[s], nvt_ref[s, head_dims(h), :], None)],
                                   None).T.astype(BF16)
        oc_ref[s] = _diff_heads_t(dqt_ref[s], lambda h: [(dk_ref[s], dvt_ref[s, head_dims(h), :], None)], lam,
                                  gsubt_ref[...], lam_init, None).T.astype(BF16)
        od_ref[s] = _gqa_heads_t(gqt_ref[s], lambda n: [(gk_ref[s], gvt_ref[s, head_dims(n), :], None)],
                                 None).T.astype(BF16)


def _ctx_attn_call(qkv_c, lam, gsub, lam_init, n_batch):
    nb = math.gcd(CTX_SAMPLES, n_batch)
    arrays = [a if a.ndim == 3 else a.reshape(n_batch, CTX, a.shape[1]) for a in qkv_c]
    specs = [pl.BlockSpec((nb,) + a.shape[1:], lambda b: (b, 0, 0)) for a in arrays]
    out = jax.ShapeDtypeStruct((n_batch, CTX, 256), BF16)
    outs = pl.pallas_call(
        functools.partial(_ctx_attn_kernel, lam_init=lam_init),
        grid=(n_batch // nb,),
        in_specs=specs + [_const_spec(lam), _const_spec(gsub)],
        out_specs=[pl.BlockSpec((nb, CTX, 256), lambda b: (b, 0, 0))] * 4,
        out_shape=[out] * 4,
        compiler_params=_cparams(1),
        name="ctx_attn",
    )(*arrays, _operand(lam), _operand(gsub))
    return [o.reshape(n_batch * CTX, 256) for o in outs]


def _merge_kernel(x_ref, mod_ref, g_ref, oa_ref, ob_ref, oc_ref, od_ref, wp_ref, wb_ref, wo_ref, out_ref):
    x = x_ref[...]
    mod = mod_ref[0]
    xn = _modnorm(x, g_ref[...], mod[:, 0:D], mod[:, D:2 * D]).astype(BF16)
    acc = jnp.zeros(x.shape, F32)
    for n, o_ref in enumerate((oa_ref, ob_ref, oc_ref, od_ref)):
        gate = jax.nn.sigmoid(_dot(xn, wp_ref[:, Z_W + n * D:Z_W + (n + 1) * D]))
        acc = acc + gate * _dot(o_ref[...], wb_ref[n])
    y = _dot(acc.astype(BF16), wo_ref[...])
    out_ref[...] = x + mod[:, 2 * D:3 * D] * y


def _merge_call(rows, mods, layer, mod_row, g1, branches, w_proj, w_branch, w_out, tm):
    n_rows = rows.shape[0]
    tm = min(tm, n_rows)
    tok = lambda w: pl.BlockSpec((tm, w), lambda i: (i, 0))
    return pl.pallas_call(
        _merge_kernel,
        grid=(n_rows // tm,),
        in_specs=[tok(D), _mod_spec(layer, mod_row), _const_spec(g1)]
                 + [tok(256)] * 4 + [_const_spec(w_proj), _const_spec(w_branch), _const_spec(w_out)],
        out_specs=tok(D),
        out_shape=jax.ShapeDtypeStruct((n_rows, D), F32),
        compiler_params=_cparams(1),
        name="merge",
    )(rows, mods, _operand(g1), *branches, _operand(w_proj), _operand(w_branch), _operand(w_out))


def _mlp_kernel(x_ref, mod_ref, g_ref, wu_ref, wd_ref, out_ref):
    x = x_ref[...]
    mod = mod_ref[0]
    xn = _modnorm(x, g_ref[...], mod[:, 3 * D:4 * D], mod[:, 4 * D:5 * D]).astype(BF16)
    h = jnp.square(jnp.maximum(_dot(xn, wu_ref[...]), 0.0))
    out_ref[...] = x + mod[:, 5 * D:6 * D] * _dot(h.astype(BF16), wd_ref[...])


def _mlp_call(rows, mods, layer, mod_row, g2, w_up, w_down, tm):
    n_rows = rows.shape[0]
    tm = min(tm, n_rows)
    tok = lambda w: pl.BlockSpec((tm, w), lambda i: (i, 0))
    return pl.pallas_call(
        _mlp_kernel,
        grid=(n_rows // tm,),
        in_specs=[tok(D), _mod_spec(layer, mod_row), _const_spec(g2),
                  _const_spec(w_up), _const_spec(w_down)],
        out_specs=tok(D),
        out_shape=jax.ShapeDtypeStruct((n_rows, D), F32),
        compiler_params=_cparams(1),
        name="mlp",
    )(rows, mods, _operand(g2), _operand(w_up), _operand(w_down))


def _block_diag_ones(n, seg):
    i = jnp.arange(n) // seg
    return (i[:, None] == i[None, :]).astype(BF16)


def _rope_parts(rot_dim):
    t = jnp.arange(SEQ)
    rows, cols = t // GRID_W, t % GRID_W
    n = rot_dim // 4
    inv_freq = jnp.power(ROPE_THETA, -jnp.arange(n, dtype=F32) / n)
    ang_r = rows.astype(F32)[:, None] * inv_freq
    ang_c = cols.astype(F32)[:, None] * inv_freq
    ang = jnp.concatenate([ang_r, ang_r, ang_c, ang_c], axis=-1)
    cos, sin = jnp.cos(ang), jnp.sin(ang)
    even = (jnp.arange(rot_dim) // n) % 2 == 0
    return cos, jnp.where(even, -sin, 0.0), jnp.where(even, 0.0, sin)


def _rope_tables():
    r32, r64 = _rope_parts(MLA_ROPE), _rope_parts(HEAD_DIM)
    ones = jnp.ones((SEQ, MLA_NOPE), F32)
    zeros_n = jnp.zeros((SEQ, MLA_NOPE), F32)
    pad = jnp.zeros((SEQ, LANES - MLA_QK), F32)
    mla = [jnp.concatenate([lead, part, pad], axis=1) for lead, part in zip((ones, zeros_n, zeros_n), r32)]
    return jnp.stack(mla + [jnp.tile(a, (1, LANES // MLA_ROPE)) for a in r32]
                     + [jnp.tile(a, (1, LANES // HEAD_DIM)) for a in r64])


def _na_pair_table(rpb):
    kc = jnp.arange(GRID_W)[:, None]
    qc = jnp.arange(GRID_W)[None, :]
    onehot = (kc - qc + NA_COLS - 1)[None] == jnp.arange(2 * NA_COLS - 1)[:, None, None]
    toeplitz = jnp.einsum("lhrd,dkq->lhrkq", rpb, onehot.astype(F32), precision=lax.Precision.HIGHEST)
    c0 = jnp.clip(qc - NA_COLS // 2, 0, GRID_W - NA_COLS)
    in_window = (kc >= c0) & (kc < c0 + NA_COLS)
    masked = jnp.where(in_window, toeplitz * LOG2E, MASK_VALUE)
    ext = jnp.pad(masked, ((0, 0), (0, 0), (1, 1), (0, 0), (0, 0)), constant_values=MASK_VALUE)
    return jnp.concatenate([ext[:, :, 1:], ext[:, :, :-1]], axis=-1)


def _prepare_params(p):
    w_in = p["w_in"]
    n_layers = w_in.shape[0]
    zcol = lambda n: jnp.zeros((n_layers, D, n), F32)
    kr = w_in[:, :, C_KR:C_KR + MLA_ROPE]
    kr_slots = jnp.concatenate([zcol(MLA_NOPE), kr, zcol(LANES - MLA_QK)] * 4, axis=2)
    head_order = jnp.array([0, 2, 1, 3])
    gq_cols = w_in[:, :, C_GQ:C_GQ + 256].reshape(n_layers, D, 4, HEAD_DIM)[:, :, head_order].reshape(n_layers, D, 256)
    w_proj = jnp.concatenate([
        w_in[:, :, C_CQ:C_CQ + MLA_Q_LORA], zcol(256 - MLA_Q_LORA),
        w_in[:, :, C_CKV:C_CKV + MLA_KV_LORA],
        kr_slots,
        w_in[:, :, C_NA:C_NA + 768],
        w_in[:, :, C_DF:C_DF + 768],
        gq_cols, w_in[:, :, C_GQ + 256:C_GQ + 512],
        w_in[:, :, C_GATE:]], axis=2).astype(BF16)

    w_uq = p["w_mla_uq"].reshape(n_layers, MLA_Q_LORA, 4, MLA_QK)
    w_uq = jnp.pad(w_uq, ((0, 0), (0, 256 - MLA_Q_LORA), (0, 0), (0, LANES - MLA_QK)))
    w_ukv = p["w_mla_ukv"].reshape(n_layers, MLA_KV_LORA, 4, 2 * MLA_NOPE)
    w_uk = jnp.pad(w_ukv[..., :MLA_NOPE], ((0, 0), (0, 0), (0, 0), (0, LANES - MLA_NOPE)))

    mla_gain = lambda g: jnp.tile(jnp.pad(g, ((0, 0), (0, LANES - MLA_QK))), (1, 4))
    rep = lambda g, n: jnp.tile(g, (1, n))
    row = lambda g: jnp.pad(g, ((0, 0), (0, 512 - g.shape[1])))
    hd_scale = HEAD_DIM ** -0.5 * LOG2E
    gains = jnp.stack([
        row(mla_gain(p["g_mla_q"]) * (MLA_QK ** -0.5 * LOG2E)), row(mla_gain(p["g_mla_k"])),
        row(rep(p["g_na_q"], 4) * hd_scale), row(rep(p["g_na_k"], 4)),
        row(rep(p["g_diff_q"], 8) * (DIFF_DIM ** -0.5 * LOG2E)), row(rep(p["g_diff_k"], 8)),
        row(rep(p["g_gqa_q"], 4) * hd_scale), row(rep(p["g_gqa_k"], 2))], axis=1)

    def score_bound(dim, g_q, g_k):
        return LOG2E * dim ** 0.5 * jnp.max(jnp.abs(g_q), axis=-1) * jnp.max(jnp.abs(g_k), axis=-1)

    bounds = jnp.stack([
        score_bound(MLA_QK, p["g_mla_q"], p["g_mla_k"]),
        score_bound(HEAD_DIM, p["g_na_q"], p["g_na_k"]) + LOG2E * jnp.max(jnp.abs(p["na_rpb"]), axis=(1, 2, 3)),
        score_bound(DIFF_DIM, p["g_diff_q"], p["g_diff_k"]),
        score_bound(HEAD_DIM, p["g_gqa_q"], p["g_gqa_k"])], axis=1).astype(F32).reshape(n_layers * N_BOUNDS)

    w_branch = p["w_branch"]
    wb_gqa = w_branch[:, 3].reshape(n_layers, 4, HEAD_DIM, D)[:, head_order].reshape(n_layers, 1, 256, D)
    w_branch = jnp.concatenate([w_branch[:, :3], wb_gqa], axis=1).astype(BF16)

    return dict(
        g1=p["g_norm1"][:, None], g2=p["g_norm2"][:, None],
        w_proj=w_proj,
        g_qa=jnp.pad(p["g_mla_qa"], ((0, 0), (0, 256 - MLA_Q_LORA)))[:, None], g_kva=p["g_mla_kva"][:, None],
        w_uq=w_uq.reshape(n_layers, 256, 512).astype(BF16), w_uk=w_uk.reshape(n_layers, MLA_KV_LORA, 512).astype(BF16),
        w_uv=w_ukv[..., MLA_NOPE:].reshape(n_layers, MLA_KV_LORA, 256).astype(BF16),
        gains=gains, bounds=bounds, na_pair=_na_pair_table(p["na_rpb"]),
        lam=jnp.stack([p["diff_lq1"], p["diff_lk1"], p["diff_lq2"], p["diff_lk2"]], axis=1),
        g_sub=rep(p["g_diff_sub"], 4)[:, :, None],
        w_branch=w_branch, w_out=p["w_out"].astype(BF16),
        w_up=p["w_up"].astype(BF16), w_down=p["w_down"].astype(BF16),
    )


def kernel(x, c, ctx, c_ctx, w_ada, b_ada, g_norm1, g_norm2, w_in, g_mla_qa, w_mla_uq, g_mla_kva, w_mla_ukv, g_mla_q, g_mla_k, g_na_q, g_na_k, na_rpb, g_diff_q, g_diff_k, diff_lq1, diff_lk1, diff_lq2, diff_lk2, g_diff_sub, g_gqa_q, g_gqa_k, w_branch, w_out, w_up, w_down):
    p = dict(w_in=w_in, g_norm1=g_norm1, g_norm2=g_norm2, g_mla_qa=g_mla_qa, w_mla_uq=w_mla_uq,
             g_mla_kva=g_mla_kva, w_mla_ukv=w_mla_ukv, g_mla_q=g_mla_q, g_mla_k=g_mla_k,
             g_na_q=g_na_q, g_na_k=g_na_k, na_rpb=na_rpb, g_diff_q=g_diff_q, g_diff_k=g_diff_k,
             diff_lq1=diff_lq1, diff_lk1=diff_lk1, diff_lq2=diff_lq2, diff_lk2=diff_lk2,
             g_diff_sub=g_diff_sub, g_gqa_q=g_gqa_q, g_gqa_k=g_gqa_k, w_branch=w_branch,
             w_out=w_out, w_up=w_up, w_down=w_down)
    n_batch = x.shape[0]
    depth = w_ada.shape[0]
    assert x.shape[1:] == (SEQ, D) and ctx.shape[1:] == (CTX, D)

    mod_rows = -(-(n_batch + 1) // 8) * 8
    c_all = jnp.concatenate([c, c_ctx[None], jnp.zeros((mod_rows - n_batch - 1, D), F32)], axis=0)
    mods = _ada_call(c_all, w_ada, b_ada).reshape(depth, mod_rows, 1, 6 * D)
    lat_tiles = SEQ // TM
    lat_row = lambda i: i // lat_tiles
    ctx_row = lambda i: n_batch

    consts = dict(bd128=_block_diag_ones(256, 128), bd64=_block_diag_ones(256, 64),
                  bd32=_block_diag_ones(256, 32), bd64h=_block_diag_ones(128, 64))
    tabs = _rope_tables()
    stacked = _prepare_params(p)
    bounds = stacked.pop("bounds")

    xl = x.reshape(n_batch * SEQ, D)
    xc = ctx.reshape(n_batch * CTX, D)
    for l in range(depth):
        need_ctx = l < depth - 1
        lam_init = 0.8 - 0.6 * math.exp(-0.3 * l)
        lw = {name: _Layer(a, l) for name, a in stacked.items()}
        bound = lambda which: N_BOUNDS * l + which

        lat = _qkv_call(xl, mods, l, lambda b: b, lw, consts, tabs, n_batch, SEQ, TM)
        cx = _qkv_call(xc, mods, l, ctx_row, lw, consts, None, n_batch, CTX, CTX)
        mqt, mk, mvt, nqt, nk, nvt, dqt, dk, dvt, gqt, gk, gvt = lat
        oa = _latent_attn_call(_mla_attn_kernel, "mla_attn", bounds, bound(B_MLA), mqt, mk, mvt, cx[1], cx[2], [],
                               n_batch)
        ob = _na_attn_call(bounds, bound(B_NA), nqt, nk, nvt, cx[4], cx[5], lw["na_pair"], n_batch)
        oc = _latent_attn_call(functools.partial(_diff_attn_kernel, lam_init=lam_init), "diff_attn",
                               bounds, bound(B_DIFF), dqt, dk, dvt, cx[7], cx[8], [lw["lam"], lw["g_sub"]], n_batch)
        od = _latent_attn_call(_gqa_attn_kernel, "gqa_attn", bounds, bound(B_GQA), gqt, gk, gvt, cx[10], cx[11], [],
                               n_batch)
        if need_ctx:
            oc_all = _ctx_attn_call(cx, lw["lam"], lw["g_sub"], lam_init, n_batch)
            xc = _merge_call(xc, mods, l, ctx_row, lw["g1"], oc_all, lw["w_proj"], lw["w_branch"], lw["w_out"], TM)
            xc = _mlp_call(xc, mods, l, ctx_row, lw["g2"], lw["w_up"], lw["w_down"], TM)
        xl = _merge_call(xl, mods, l, lambda i: i // (SEQ // TM_MERGE), lw["g1"], (oa, ob, oc, od), lw["w_proj"],
                         lw["w_branch"], lw["w_out"], TM_MERGE)
        xl = _mlp_call(xl, mods, l, lat_row, lw["g2"], lw["w_up"], lw["w_down"], TM)
    return xl.reshape(n_batch, SEQ, D)
```

```python
import functools
import math
from typing import NamedTuple

import jax
import jax.numpy as jnp
from jax import lax
from jax.experimental import pallas as pl
from jax.experimental.pallas import tpu as pltpu

F32 = jnp.float32
BF16 = jnp.bfloat16

D = 1024
SEQ = 2048
CTX = 256
GRID_W = 64
N_ROWS = SEQ // GRID_W
ROPE_THETA = 10000.0
EPS = 1e-6
HEAD_DIM = 64
MLA_Q_LORA = 192
MLA_KV_LORA = 128
MLA_NOPE = 64
MLA_ROPE = 32
MLA_QK = MLA_NOPE + MLA_ROPE
NA_ROWS = 8
NA_COLS = 16
DIFF_DIM = 32
D_FF = 4 * D

LANES = 128
TM = 512
TM_MERGE = 1024
TQ = 1024
QC = 1024
NA_TQ = 256
NA_WIN = 3
NA_SAMPLES = 4
CTX_SAMPLES = 4
MASK_VALUE = -1e30
LOG2E = math.log2(math.e)
MAX_FIXED_SHIFT = 50.0

C_CQ, C_CKV, C_KR, C_NA, C_DF, C_GQ, C_GATE = 0, 192, 320, 352, 1120, 1888, 2400
Z_CQ, Z_CKV, Z_KR, Z_NA, Z_DF, Z_GQ, Z_W = 0, 256, 384, 896, 1664, 2432, 2944
G_MQ, G_MK, G_NQ, G_NK, G_DQ, G_DK, G_GQ, G_GK = range(8)
T_MLA, T_R32, T_R64 = 0, 3, 6
B_MLA, B_NA, B_DIFF, B_GQA, N_BOUNDS = 0, 1, 2, 3, 4

VMEM_LIMIT = 56 * 1024 * 1024


def _cparams(n_axes):
    return pltpu.CompilerParams(dimension_semantics=("arbitrary",) * n_axes,
                                vmem_limit_bytes=VMEM_LIMIT)


def _dot(a, b):
    return jnp.dot(a, b, preferred_element_type=F32)


class _Layer(NamedTuple):
    stacked: jax.Array
    index: int


def _operand(a):
    return a.stacked if isinstance(a, _Layer) else a


def _const_spec(a):
    if isinstance(a, _Layer):
        shape = a.stacked.shape[1:]
        return pl.BlockSpec((None,) + shape, lambda *_: (a.index,) + (0,) * len(shape),
                            pipeline_mode=pl.Buffered(1))
    return pl.BlockSpec(a.shape, lambda *_: (0,) * a.ndim, pipeline_mode=pl.Buffered(1))


def _mod_spec(layer, row_fn):
    return pl.BlockSpec((None, 1, 1, 6 * D), lambda *idx: (layer, row_fn(*idx), 0, 0))


def _modnorm(x, g, shift, scale):
    y = x * lax.rsqrt(jnp.mean(x * x, axis=-1, keepdims=True) + EPS) * g
    return y * (1.0 + scale) + shift


def _ada_kernel(c_ref, w_ref, b_ref, o_ref):
    c = c_ref[...]
    s = c * jax.nn.sigmoid(c)
    o_ref[0] = _dot(s.astype(BF16), w_ref[0].astype(BF16)) + b_ref[0]


def _ada_call(c_all, w_ada, b_ada):
    n_layers = w_ada.shape[0]
    rows = c_all.shape[0]
    bn = 1536
    return pl.pallas_call(
        _ada_kernel,
        grid=(n_layers, 6 * D // bn),
        in_specs=[pl.BlockSpec((rows, D), lambda l, j: (0, 0)),
                  pl.BlockSpec((1, D, bn), lambda l, j: (l, 0, j)),
                  pl.BlockSpec((1, 1, bn), lambda l, j: (l, 0, j))],
        out_specs=pl.BlockSpec((1, rows, bn), lambda l, j: (l, 0, j)),
        out_shape=jax.ShapeDtypeStruct((n_layers, rows, 6 * D), F32),
        compiler_params=_cparams(2),
        name="ada",
    )(c_all, w_ada, b_ada.reshape(n_layers, 1, 6 * D))


def _seg_mean_sq(t, bd, inv_d):
    x2 = t * t
    hi = x2.astype(BF16)
    lo = (x2 - hi.astype(F32)).astype(BF16)
    return (_dot(hi, bd) + _dot(lo, bd)) * inv_d


def _qkv_kernel(*refs, rope):
    (x_ref, mod_ref, g1_ref, w_ref, bd128_ref, bd64_ref, bd32_ref, bd64h_ref,
     gqa_ref, gkva_ref, wuq_ref, wuk_ref, wuv_ref, gains_ref) = refs[:14]
    tab_ref = refs[14] if rope else None
    (mqt_ref, mk_ref, mvt_ref, nqt_ref, nk_ref, nvt_ref,
     dqt_ref, dk_ref, dvt_ref, gqt_ref, gk_ref, gvt_ref) = refs[-12:]

    mod = mod_ref[0]
    xn = _modnorm(x_ref[...], g1_ref[...], mod[:, 0:D], mod[:, D:2 * D]).astype(BF16)

    def cols(lo, hi):
        return _dot(xn, w_ref[:, lo:hi])

    def put(o_ref, lo, val):
        o_ref[:, lo:lo + LANES] = val.astype(BF16)

    def put_t(o_ref, lo, val):
        o_ref[0, lo:lo + LANES, :] = val.T.astype(BF16)

    def section(t, bd, inv_d, gain_row, tab, shift, o_ref, store):
        width = t.shape[1]
        y = t * lax.rsqrt(_seg_mean_sq(t, bd, inv_d) + EPS) * gains_ref[gain_row:gain_row + 1, 0:width]
        for j in range(width // LANES):
            yc = y[:, j * LANES:(j + 1) * LANES]
            if rope and tab is not None:
                yc = (yc * tab_ref[tab] + pltpu.roll(yc, LANES - shift, 1) * tab_ref[tab + 1]
                      + pltpu.roll(yc, shift, 1) * tab_ref[tab + 2])
            store(o_ref, j * LANES, yc)

    za = cols(Z_CQ, Z_NA)
    cq = za[:, Z_CQ:Z_CQ + 256]
    cqn = cq * lax.rsqrt(jnp.sum(cq * cq, axis=-1, keepdims=True) * (1.0 / MLA_Q_LORA) + EPS) * gqa_ref[...]
    q = _dot(cqn.astype(BF16), wuq_ref[...])
    ckv = za[:, Z_CKV:Z_CKV + 128]
    kvn = (ckv * lax.rsqrt(jnp.mean(ckv * ckv, axis=-1, keepdims=True) + EPS) * gkva_ref[...]).astype(BF16)
    k = _dot(kvn, wuk_ref[...]) + za[:, Z_KR:Z_KR + 512]
    mvt_ref[0] = _dot(kvn, wuv_ref[...]).T.astype(BF16)
    bd128 = bd128_ref[...]
    for c in range(2):
        sl = slice(256 * c, 256 * c + 256)
        section(q[:, sl], bd128, 1.0 / MLA_QK, G_MQ, T_MLA, MLA_ROPE // 4, mqt_ref.at[:, sl, :], put_t)
        section(k[:, sl], bd128, 1.0 / MLA_QK, G_MK, T_MLA, MLA_ROPE // 4, mk_ref.at[:, sl], put)

    zn = cols(Z_NA, Z_DF)
    bd64 = bd64_ref[...]
    section(zn[:, 0:256], bd64, 1.0 / HEAD_DIM, G_NQ, None, 0, nqt_ref, put_t)
    section(zn[:, 256:512], bd64, 1.0 / HEAD_DIM, G_NK, None, 0, nk_ref, put)
    nvt_ref[0] = zn[:, 512:768].T.astype(BF16)

    zd = cols(Z_DF, Z_GQ)
    bd32 = bd32_ref[...]
    section(zd[:, 0:256], bd32, 1.0 / DIFF_DIM, G_DQ, T_R32, DIFF_DIM // 4, dqt_ref, put_t)
    section(zd[:, 256:512], bd32, 1.0 / DIFF_DIM, G_DK, T_R32, DIFF_DIM // 4, dk_ref, put)
    dvt_ref[0] = zd[:, 512:768].T.astype(BF16)

    zg = cols(Z_GQ, Z_W)
    section(zg[:, 0:256], bd64, 1.0 / HEAD_DIM, G_GQ, T_R64, HEAD_DIM // 4, gqt_ref, put_t)
    section(zg[:, 256:384], bd64h_ref[...], 1.0 / HEAD_DIM, G_GK, T_R64, HEAD_DIM // 4, gk_ref, put)
    gvt_ref[0] = zg[:, 384:512].T.astype(BF16)


def _qkv_call(rows, mods, layer, mod_row, lw, consts, tabs, n_batch, seq, tm):
    tiles = seq // tm
    row_map = lambda t, b: (b * tiles + t, 0)
    tok_spec = lambda w: pl.BlockSpec((tm, w), row_map)
    tr_spec = lambda w: pl.BlockSpec((1, w, tm), lambda t, b: (b, 0, t))
    tok_shape = lambda w: jax.ShapeDtypeStruct((n_batch * seq, w), BF16)
    tr_shape = lambda w: jax.ShapeDtypeStruct((n_batch, w, seq), BF16)
    small = [lw["g1"], lw["w_proj"], consts["bd128"], consts["bd64"], consts["bd32"], consts["bd64h"],
             lw["g_qa"], lw["g_kva"], lw["w_uq"], lw["w_uk"], lw["w_uv"], lw["gains"]]
    in_specs = [tok_spec(D), _mod_spec(layer, lambda t, b: mod_row(b))]
    in_specs += [_const_spec(a) for a in small]
    args = [rows, mods] + [_operand(a) for a in small]
    if tabs is not None:
        in_specs.append(pl.BlockSpec((tabs.shape[0], tm, LANES), lambda t, b: (0, t, 0)))
        args.append(tabs)
    return pl.pallas_call(
        functools.partial(_qkv_kernel, rope=tabs is not None),
        grid=(tiles, n_batch),
        in_specs=in_specs,
        out_specs=[tr_spec(512), tok_spec(512), tr_spec(256),
                   tr_spec(256), tok_spec(256), tr_spec(256),
                   tr_spec(256), tok_spec(256), tr_spec(256),
                   tr_spec(256), tok_spec(128), tr_spec(128)],
        out_shape=[tr_shape(512), tok_shape(512), tr_shape(256),
                   tr_shape(256), tok_shape(256), tr_shape(256),
                   tr_shape(256), tok_shape(256), tr_shape(256),
                   tr_shape(256), tok_shape(128), tr_shape(128)],
        compiler_params=_cparams(2),
        name="qkv",
    )(*args)


def _keep_rows(x, lo, hi):
    row = lax.broadcasted_iota(jnp.int32, x.shape, 0)
    return jnp.where(jnp.logical_and(row >= lo, row < hi), x, jnp.zeros_like(x))


def _sum_all(xs):
    return functools.reduce(lambda a, b: a + b, xs)


def _exp_scores(qt, groups, shift):
    s = [_dot(k, qt) if bias is None else _dot(k, qt) + bias for k, _, bias in groups]
    if shift is None:
        shift = functools.reduce(jnp.maximum, [jnp.max(x, axis=0, keepdims=True) for x in s])
    p = [jnp.exp2(x - shift) for x in s]
    return [x.astype(BF16) for x in p], _sum_all([jnp.sum(x, axis=0, keepdims=True) for x in p])


def _softmax_vt(qt, groups, shift):
    p, l = _exp_scores(qt, groups, shift)
    return _sum_all([_dot(g[1], x) for x, g in zip(p, groups)]) / l


def _mla_heads_t(qt_fn, groups_fn, shift):
    outs = []
    for h in range(4):
        c0 = 256 * (h // 2)
        lo = LANES * (h % 2)
        outs.append(_softmax_vt(_keep_rows(qt_fn(c0), lo, lo + LANES), groups_fn(h, c0), shift))
    return jnp.concatenate(outs, axis=0)


def _diff_heads_t(qt, groups_fn, lam, gsub_t, lam_init, shift):
    outs = []
    for h in range(4):
        groups = groups_fn(h)
        (p1, l1), (p2, l2) = [_exp_scores(_keep_rows(qt, 64 * h + DIFF_DIM * m, 64 * h + DIFF_DIM * (m + 1)),
                                          groups, shift) for m in range(2)]
        w1 = (1.0 / l1).astype(BF16)
        w2 = (lam / l2).astype(BF16)
        oh = _sum_all([_dot(g[1], a * w1 - b * w2) for a, b, g in zip(p1, p2, groups)])
        ms = jnp.mean(oh * oh, axis=0, keepdims=True)
        outs.append(oh * lax.rsqrt(ms + EPS) * gsub_t[64 * h:64 * h + 64, :])
    return jnp.concatenate(outs, axis=0) * (1.0 - lam_init)


def _gqa_heads_t(qt, groups_fn, shift):
    outs = []
    for g in range(2):
        qc = qt[LANES * g:LANES * (g + 1), :]
        for n in range(2):
            outs.append(_softmax_vt(_keep_rows(qc, 64 * n, 64 * n + 64), groups_fn(n), shift))
    return jnp.concatenate(outs, axis=0)


def _plain_heads_t(qt, groups_fn, shift):
    return jnp.concatenate([_softmax_vt(_keep_rows(qt, 64 * h, 64 * h + 64), groups_fn(h), shift) for h in range(4)],
                           axis=0)


def _diff_lambda(lam_ref, lam_init):
    lq1, lk1, lq2, lk2 = (lam_ref[i:i + 1, :] for i in range(4))
    return (jnp.exp(jnp.sum(lq1 * lk1, axis=-1, keepdims=True))
            - jnp.exp(jnp.sum(lq2 * lk2, axis=-1, keepdims=True)) + lam_init)


def _latent_attention(shift_ref, bound_index, chunk_fn):
    bound = shift_ref[bound_index]

    def run(shift):
        if TQ == QC:
            chunk_fn(slice(0, QC), shift)
            return

        def step(c, carry):
            chunk_fn(pl.ds(pl.multiple_of(c * QC, QC), QC), shift)
            return carry

        lax.fori_loop(0, TQ // QC, step, 0)

    @pl.when(bound <= MAX_FIXED_SHIFT)
    def _():
        run(bound)

    @pl.when(jnp.logical_not(bound <= MAX_FIXED_SHIFT))
    def _():
        run(None)


def _mla_attn_kernel(shift_ref, qt_ref, kl_ref, vtl_ref, kc_ref, vtc_ref, o_ref, *, bound_index):
    def chunk(cols, shift):
        def groups_fn(h, c0):
            dims = slice(64 * h, 64 * h + 64)
            return [(kl_ref[:, c0:c0 + 256], vtl_ref[0, dims, :], None),
                    (kc_ref[:, c0:c0 + 256], vtc_ref[0, dims, :], None)]

        ot = _mla_heads_t(lambda c0: qt_ref[0, c0:c0 + 256, cols], groups_fn, shift)
        o_ref[cols, :] = ot.T.astype(BF16)

    _latent_attention(shift_ref, bound_index, chunk)


def _diff_attn_kernel(shift_ref, qt_ref, kl_ref, vtl_ref, kc_ref, vtc_ref, lam_ref, gsubt_ref, o_ref, *, lam_init,
                      bound_index):
    def chunk(cols, shift):
        def groups_fn(h):
            dims = slice(64 * h, 64 * h + 64)
            return [(kl_ref[...], vtl_ref[0, dims, :], None), (kc_ref[...], vtc_ref[0, dims, :], None)]

        lam = _diff_lambda(lam_ref, lam_init)
        ot = _diff_heads_t(qt_ref[0, :, cols], groups_fn, lam, gsubt_ref[...], lam_init, shift)
        o_ref[cols, :] = ot.T.astype(BF16)

    _latent_attention(shift_ref, bound_index, chunk)


def _gqa_attn_kernel(shift_ref, qt_ref, kl_ref, vtl_ref, kc_ref, vtc_ref, o_ref, *, bound_index):
    def chunk(cols, shift):
        def groups_fn(n):
            dims = slice(64 * n, 64 * n + 64)
            return [(kl_ref[...], vtl_ref[0, dims, :], None), (kc_ref[...], vtc_ref[0, dims, :], None)]

        ot = _gqa_heads_t(qt_ref[0, :, cols], groups_fn, shift)
        o_ref[cols, :] = ot.T.astype(BF16)

    _latent_attention(shift_ref, bound_index, chunk)


def _latent_attn_call(kernel, name, bounds, bound_index, qt, kl, vtl, kc, vtc, extra, n_batch):
    wq, wk, wv = qt.shape[1], kl.shape[1], vtl.shape[1]
    tiles = SEQ // TQ
    return pl.pallas_call(
        functools.partial(kernel, bound_index=bound_index),
        grid=(n_batch, tiles),
        in_specs=[pl.BlockSpec(memory_space=pltpu.SMEM),
                  pl.BlockSpec((1, wq, TQ), lambda b, t: (b, 0, t)),
                  pl.BlockSpec((SEQ, wk), lambda b, t: (b, 0)),
                  pl.BlockSpec((1, wv, SEQ), lambda b, t: (b, 0, 0)),
                  pl.BlockSpec((CTX, wk), lambda b, t: (b, 0)),
                  pl.BlockSpec((1, wv, CTX), lambda b, t: (b, 0, 0))] + [_const_spec(a) for a in extra],
        out_specs=pl.BlockSpec((TQ, 256), lambda b, t: (b * tiles + t, 0)),
        out_shape=jax.ShapeDtypeStruct((n_batch * SEQ, 256), BF16),
        compiler_params=_cparams(2),
        name=name,
    )(bounds, qt, kl, vtl, kc, vtc, *map(_operand, extra))


def _na_window(t):
    return jnp.clip(t - 1, 0, SEQ // NA_TQ - NA_WIN)


def _na_build_bias(t, pair_ref, bias_ref):
    rows_per_tile = NA_TQ // GRID_W
    k_row0 = rows_per_tile * _na_window(t)
    lane = lax.broadcasted_iota(jnp.int32, (GRID_W, LANES), 1)
    for i in range(rows_per_tile // 2):
        qr = rows_per_tile * t + 2 * i
        r0 = jnp.clip(qr - NA_ROWS // 2, 0, N_ROWS - NA_ROWS)
        r1 = jnp.clip(qr + 1 - NA_ROWS // 2, 0, N_ROWS - NA_ROWS)
        for kl in range(NA_WIN * rows_per_tile):
            kr = k_row0 + kl
            ok0 = jnp.logical_and(kr >= r0, kr < r0 + NA_ROWS).astype(jnp.int32)
            ok1 = jnp.logical_and(kr >= r1, kr < r1 + NA_ROWS).astype(jnp.int32)
            ok = jnp.where(lane < GRID_W, ok0, ok1) > 0
            d = jnp.clip(kr - qr + NA_ROWS - 1, 0, 2 * NA_ROWS - 1)
            for h in range(4):
                blk = jnp.where(ok, pair_ref[h, d], MASK_VALUE)
                bias_ref[h, GRID_W * kl:GRID_W * (kl + 1), LANES * i:LANES * (i + 1)] = blk


def _na_attn_kernel(shift_ref, qt_ref, k0_ref, k1_ref, k2_ref, kc_ref, vt0_ref, vt1_ref, vt2_ref, vtc_ref, pair_ref,
                    o_ref, bias_ref, *, bound_index):
    t = pl.program_id(0)

    @pl.when(pl.program_id(1) == 0)
    def _():
        _na_build_bias(t, pair_ref, bias_ref)

    win = ((k0_ref, vt0_ref), (k1_ref, vt1_ref), (k2_ref, vt2_ref))

    def run(shift):
        for s in range(qt_ref.shape[0]):
            def groups_fn(h):
                dims = slice(64 * h, 64 * h + 64)
                return ([(k[s], vt[s, dims, :], bias_ref[h, NA_TQ * j:NA_TQ * (j + 1), :])
                         for j, (k, vt) in enumerate(win)] + [(kc_ref[s], vtc_ref[s, dims, :], None)])

            o_ref[s] = _plain_heads_t(qt_ref[s], groups_fn, shift).T.astype(BF16)

    bound = shift_ref[bound_index]

    @pl.when(bound <= MAX_FIXED_SHIFT)
    def _():
        run(bound)

    @pl.when(jnp.logical_not(bound <= MAX_FIXED_SHIFT))
    def _():
        run(None)


def _na_attn_call(bounds, bound_index, qt, kl, vtl, kc, vtc, pair, n_batch):
    tiles = SEQ // NA_TQ
    nb = math.gcd(NA_SAMPLES, n_batch)
    kl = kl.reshape(n_batch, SEQ, 256)
    kc = kc.reshape(n_batch, CTX, 256)
    k_spec = lambda j: pl.BlockSpec((nb, NA_TQ, 256), lambda t, b: (b, _na_window(t) + j, 0))
    vt_spec = lambda j: pl.BlockSpec((nb, 256, NA_TQ), lambda t, b: (b, 0, _na_window(t) + j))
    out = pl.pallas_call(
        functools.partial(_na_attn_kernel, bound_index=bound_index),
        grid=(tiles, n_batch // nb),
        in_specs=[pl.BlockSpec(memory_space=pltpu.SMEM),
                  pl.BlockSpec((nb, 256, NA_TQ), lambda t, b: (b, 0, t)),
                  k_spec(0), k_spec(1), k_spec(2),
                  pl.BlockSpec((nb, CTX, 256), lambda t, b: (b, 0, 0)),
                  vt_spec(0), vt_spec(1), vt_spec(2),
                  pl.BlockSpec((nb, 256, CTX), lambda t, b: (b, 0, 0)),
                  _const_spec(pair)],
        out_specs=pl.BlockSpec((nb, NA_TQ, 256), lambda t, b: (b, t, 0)),
        out_shape=jax.ShapeDtypeStruct((n_batch, SEQ, 256), BF16),
        scratch_shapes=[pltpu.VMEM((4, NA_WIN * NA_TQ, NA_TQ), F32)],
        compiler_params=_cparams(2),
        name="na_attn",
    )(bounds, qt, kl, kl, kl, kc, vtl, vtl, vtl, vtc, _operand(pair))
    return out.reshape(n_batch * SEQ, 256)


def _ctx_attn_kernel(mqt_ref, mk_ref, mvt_ref, nqt_ref, nk_ref, nvt_ref, dqt_ref, dk_ref, dvt_ref,
                     gqt_ref, gk_ref, gvt_ref, lam_ref, gsubt_ref, oa_ref, ob_ref, oc_ref, od_ref, *, lam_init):
    head_dims = lambda h: slice(64 * h, 64 * h + 64)
    lam = _diff_lambda(lam_ref, lam_init)
    for s in range(mqt_ref.shape[0]):
        oa_ref[s] = _mla_heads_t(lambda c0: mqt_ref[s, c0:c0 + 256, :],
                                 lambda h, c0: [(mk_ref[s, :, c0:c0 + 256], mvt_ref[s, head_dims(h), :], None)],
                                 None).T.astype(BF16)
        ob_ref[s] = _plain_heads_t(nqt_ref[s], lambda h: [(nk_ref[s], nvt_ref[s, head_dims(h), :], None)],
                                   None).T.astype(BF16)
        oc_ref[s] = _diff_heads_t(dqt_ref[s], lambda h: [(dk_ref[s], dvt_ref[s, head_dims(h), :], None)], lam,
                                  gsubt_ref[...], lam_init, None).T.astype(BF16)
        od_ref[s] = _gqa_heads_t(gqt_ref[s], lambda n: [(gk_ref[s], gvt_ref[s, head_dims(n), :], None)],
                                 None).T.astype(BF16)


def _ctx_attn_call(qkv_c, lam, gsub, lam_init, n_batch):
    nb = math.gcd(CTX_SAMPLES, n_batch)
    arrays = [a if a.ndim == 3 else a.reshape(n_batch, CTX, a.shape[1]) for a in qkv_c]
    specs = [pl.BlockSpec((nb,) + a.shape[1:], lambda b: (b, 0, 0)) for a in arrays]
    out = jax.ShapeDtypeStruct((n_batch, CTX, 256), BF16)
    outs = pl.pallas_call(
        functools.partial(_ctx_attn_kernel, lam_init=lam_init),
        grid=(n_batch // nb,),
        in_specs=specs + [_const_spec(lam), _const_spec(gsub)],
        out_specs=[pl.BlockSpec((nb, CTX, 256), lambda b: (b, 0, 0))] * 4,
        out_shape=[out] * 4,
        compiler_params=_cparams(1),
        name="ctx_attn",
    )(*arrays, _operand(lam), _operand(gsub))
    return [o.reshape(n_batch * CTX, 256) for o in outs]


def _merge_kernel(x_ref, mod_ref, g_ref, oa_ref, ob_ref, oc_ref, od_ref, wp_ref, wb_ref, wo_ref, out_ref):
    x = x_ref[...]
    mod = mod_ref[0]
    xn = _modnorm(x, g_ref[...], mod[:, 0:D], mod[:, D:2 * D]).astype(BF16)
    acc = jnp.zeros(x.shape, F32)
    for n, o_ref in enumerate((oa_ref, ob_ref, oc_ref, od_ref)):
        gate = jax.nn.sigmoid(_dot(xn, wp_ref[:, Z_W + n * D:Z_W + (n + 1) * D]))
        acc = acc + gate * _dot(o_ref[...], wb_ref[n])
    y = _dot(acc.astype(BF16), wo_ref[...])
    out_ref[...] = x + mod[:, 2 * D:3 * D] * y


def _merge_call(rows, mods, layer, mod_row, g1, branches, w_proj, w_branch, w_out, tm):
    n_rows = rows.shape[0]
    tm = min(tm, n_rows)
    tok = lambda w: pl.BlockSpec((tm, w), lambda i: (i, 0))
    return pl.pallas_call(
        _merge_kernel,
        grid=(n_rows // tm,),
        in_specs=[tok(D), _mod_spec(layer, mod_row), _const_spec(g1)]
                 + [tok(256)] * 4 + [_const_spec(w_proj), _const_spec(w_branch), _const_spec(w_out)],
        out_specs=tok(D),
        out_shape=jax.ShapeDtypeStruct((n_rows, D), F32),
        compiler_params=_cparams(1),
        name="merge",
    )(rows, mods, _operand(g1), *branches, _operand(w_proj), _operand(w_branch), _operand(w_out))


def _mlp_kernel(x_ref, mod_ref, g_ref, wu_ref, wd_ref, out_ref):
    x = x_ref[...]
    mod = mod_ref[0]
    xn = _modnorm(x, g_ref[...], mod[:, 3 * D:4 * D], mod[:, 4 * D:5 * D]).astype(BF16)
    h = jnp.square(jnp.maximum(_dot(xn, wu_ref[...]), 0.0))
    out_ref[...] = x + mod[:, 5 * D:6 * D] * _dot(h.astype(BF16), wd_ref[...])


def _mlp_call(rows, mods, layer, mod_row, g2, w_up, w_down, tm):
    n_rows = rows.shape[0]
    tm = min(tm, n_rows)
    tok = lambda w: pl.BlockSpec((tm, w), lambda i: (i, 0))
    return pl.pallas_call(
        _mlp_kernel,
        grid=(n_rows // tm,),
        in_specs=[tok(D), _mod_spec(layer, mod_row), _const_spec(g2),
                  _const_spec(w_up), _const_spec(w_down)],
        out_specs=tok(D),
        out_shape=jax.ShapeDtypeStruct((n_rows, D), F32),
        compiler_params=_cparams(1),
        name="mlp",
    )(rows, mods, _operand(g2), _operand(w_up), _operand(w_down))


def _block_diag_ones(n, seg):
    i = jnp.arange(n) // seg
    return (i[:, None] == i[None, :]).astype(BF16)


def _rope_parts(rot_dim):
    t = jnp.arange(SEQ)
    rows, cols = t // GRID_W, t % GRID_W
    n = rot_dim // 4
    inv_freq = jnp.power(ROPE_THETA, -jnp.arange(n, dtype=F32) / n)
    ang_r = rows.astype(F32)[:, None] * inv_freq
    ang_c = cols.astype(F32)[:, None] * inv_freq
    ang = jnp.concatenate([ang_r, ang_r, ang_c, ang_c], axis=-1)
    cos, sin = jnp.cos(ang), jnp.sin(ang)
    even = (jnp.arange(rot_dim) // n) % 2 == 0
    return cos, jnp.where(even, -sin, 0.0), jnp.where(even, 0.0, sin)


def _rope_tables():
    r32, r64 = _rope_parts(MLA_ROPE), _rope_parts(HEAD_DIM)
    ones = jnp.ones((SEQ, MLA_NOPE), F32)
    zeros_n = jnp.zeros((SEQ, MLA_NOPE), F32)
    pad = jnp.zeros((SEQ, LANES - MLA_QK), F32)
    mla = [jnp.concatenate([lead, part, pad], axis=1) for lead, part in zip((ones, zeros_n, zeros_n), r32)]
    return jnp.stack(mla + [jnp.tile(a, (1, LANES // MLA_ROPE)) for a in r32]
                     + [jnp.tile(a, (1, LANES // HEAD_DIM)) for a in r64])


def _na_pair_table(rpb):
    kc = jnp.arange(GRID_W)[:, None]
    qc = jnp.arange(GRID_W)[None, :]
    onehot = (kc - qc + NA_COLS - 1)[None] == jnp.arange(2 * NA_COLS - 1)[:, None, None]
    toeplitz = jnp.einsum("lhrd,dkq->lhrkq", rpb, onehot.astype(F32), precision=lax.Precision.HIGHEST)
    c0 = jnp.clip(qc - NA_COLS // 2, 0, GRID_W - NA_COLS)
    in_window = (kc >= c0) & (kc < c0 + NA_COLS)
    masked = jnp.where(in_window, toeplitz * LOG2E, MASK_VALUE)
    ext = jnp.pad(masked, ((0, 0), (0, 0), (1, 1), (0, 0), (0, 0)), constant_values=MASK_VALUE)
    return jnp.concatenate([ext[:, :, 1:], ext[:, :, :-1]], axis=-1)


def _prepare_params(p):
    w_in = p["w_in"]
    n_layers = w_in.shape[0]
    zcol = lambda n: jnp.zeros((n_layers, D, n), F32)
    kr = w_in[:, :, C_KR:C_KR + MLA_ROPE]
    kr_slots = jnp.concatenate([zcol(MLA_NOPE), kr, zcol(LANES - MLA_QK)] * 4, axis=2)
    head_order = jnp.array([0, 2, 1, 3])
    gq_cols = w_in[:, :, C_GQ:C_GQ + 256].reshape(n_layers, D, 4, HEAD_DIM)[:, :, head_order].reshape(n_layers, D, 256)
    w_proj = jnp.concatenate([
        w_in[:, :, C_CQ:C_CQ + MLA_Q_LORA], zcol(256 - MLA_Q_LORA),
        w_in[:, :, C_CKV:C_CKV + MLA_KV_LORA],
        kr_slots,
        w_in[:, :, C_NA:C_NA + 768],
        w_in[:, :, C_DF:C_DF + 768],
        gq_cols, w_in[:, :, C_GQ + 256:C_GQ + 512],
        w_in[:, :, C_GATE:]], axis=2).astype(BF16)

    w_uq = p["w_mla_uq"].reshape(n_layers, MLA_Q_LORA, 4, MLA_QK)
    w_uq = jnp.pad(w_uq, ((0, 0), (0, 256 - MLA_Q_LORA), (0, 0), (0, LANES - MLA_QK)))
    w_ukv = p["w_mla_ukv"].reshape(n_layers, MLA_KV_LORA, 4, 2 * MLA_NOPE)
    w_uk = jnp.pad(w_ukv[..., :MLA_NOPE], ((0, 0), (0, 0), (0, 0), (0, LANES - MLA_NOPE)))

    mla_gain = lambda g: jnp.tile(jnp.pad(g, ((0, 0), (0, LANES - MLA_QK))), (1, 4))
    rep = lambda g, n: jnp.tile(g, (1, n))
    row = lambda g: jnp.pad(g, ((0, 0), (0, 512 - g.shape[1])))
    hd_scale = HEAD_DIM ** -0.5 * LOG2E
    gains = jnp.stack([
        row(mla_gain(p["g_mla_q"]) * (MLA_QK ** -0.5 * LOG2E)), row(mla_gain(p["g_mla_k"])),
        row(rep(p["g_na_q"], 4) * hd_scale), row(rep(p["g_na_k"], 4)),
        row(rep(p["g_diff_q"], 8) * (DIFF_DIM ** -0.5 * LOG2E)), row(rep(p["g_diff_k"], 8)),
        row(rep(p["g_gqa_q"], 4) * hd_scale), row(rep(p["g_gqa_k"], 2))], axis=1)

    def score_bound(dim, g_q, g_k):
        return LOG2E * dim ** 0.5 * jnp.max(jnp.abs(g_q), axis=-1) * jnp.max(jnp.abs(g_k), axis=-1)

    bounds = jnp.stack([
        score_bound(MLA_QK, p["g_mla_q"], p["g_mla_k"]),
        score_bound(HEAD_DIM, p["g_na_q"], p["g_na_k"]) + LOG2E * jnp.max(jnp.abs(p["na_rpb"]), axis=(1, 2, 3)),
        score_bound(DIFF_DIM, p["g_diff_q"], p["g_diff_k"]),
        score_bound(HEAD_DIM, p["g_gqa_q"], p["g_gqa_k"])], axis=1).astype(F32).reshape(n_layers * N_BOUNDS)

    w_branch = p["w_branch"]
    wb_gqa = w_branch[:, 3].reshape(n_layers, 4, HEAD_DIM, D)[:, head_order].reshape(n_layers, 1, 256, D)
    w_branch = jnp.concatenate([w_branch[:, :3], wb_gqa], axis=1).astype(BF16)

    return dict(
        g1=p["g_norm1"][:, None], g2=p["g_norm2"][:, None],
        w_proj=w_proj,
        g_qa=jnp.pad(p["g_mla_qa"], ((0, 0), (0, 256 - MLA_Q_LORA)))[:, None], g_kva=p["g_mla_kva"][:, None],
        w_uq=w_uq.reshape(n_layers, 256, 512).astype(BF16), w_uk=w_uk.reshape(n_layers, MLA_KV_LORA, 512).astype(BF16),
        w_uv=w_ukv[..., MLA_NOPE:].reshape(n_layers, MLA_KV_LORA, 256).astype(BF16),
        gains=gains, bounds=bounds, na_pair=_na_pair_table(p["na_rpb"]),
        lam=jnp.stack([p["diff_lq1"], p["diff_lk1"], p["diff_lq2"], p["diff_lk2"]], axis=1),
        g_sub=rep(p["g_diff_sub"], 4)[:, :, None],
        w_branch=w_branch, w_out=p["w_out"].astype(BF16),
        w_up=p["w_up"].astype(BF16), w_down=p["w_down"].astype(BF16),
    )


def kernel(x, c, ctx, c_ctx, w_ada, b_ada, g_norm1, g_norm2, w_in, g_mla_qa, w_mla_uq, g_mla_kva, w_mla_ukv, g_mla_q, g_mla_k, g_na_q, g_na_k, na_rpb, g_diff_q, g_diff_k, diff_lq1, diff_lk1, diff_lq2, diff_lk2, g_diff_sub, g_gqa_q, g_gqa_k, w_branch, w_out, w_up, w_down):
    p = dict(w_in=w_in, g_norm1=g_norm1, g_norm2=g_norm2, g_mla_qa=g_mla_qa, w_mla_uq=w_mla_uq,
             g_mla_kva=g_mla_kva, w_mla_ukv=w_mla_ukv, g_mla_q=g_mla_q, g_mla_k=g_mla_k,
             g_na_q=g_na_q, g_na_k=g_na_k, na_rpb=na_rpb, g_diff_q=g_diff_q, g_diff_k=g_diff_k,
             diff_lq1=diff_lq1, diff_lk1=diff_lk1, diff_lq2=diff_lq2, diff_lk2=diff_lk2,
             g_diff_sub=g_diff_sub, g_gqa_q=g_gqa_q, g_gqa_k=g_gqa_k, w_branch=w_branch,
             w_out=w_out, w_up=w_up, w_down=w_down)
    n_batch = x.shape[0]
    depth = w_ada.shape[0]
    assert x.shape[1:] == (SEQ, D) and ctx.shape[1:] == (CTX, D)

    mod_rows = -(-(n_batch + 1) // 8) * 8
    c_all = jnp.concatenate([c, c_ctx[None], jnp.zeros((mod_rows - n_batch - 1, D), F32)], axis=0)
    mods = _ada_call(c_all, w_ada, b_ada).reshape(depth, mod_rows, 1, 6 * D)
    lat_tiles = SEQ // TM
    lat_row = lambda i: i // lat_tiles
    ctx_row = lambda i: n_batch

    consts = dict(bd128=_block_diag_ones(256, 128), bd64=_block_diag_ones(256, 64),
                  bd32=_block_diag_ones(256, 32), bd64h=_block_diag_ones(128, 64))
    tabs = _rope_tables()
    stacked = _prepare_params(p)
    bounds = stacked.pop("bounds")

    xl = x.reshape(n_batch * SEQ, D)
    xc = ctx.reshape(n_batch * CTX, D)
    for l in range(depth):
        need_ctx = l < depth - 1
        lam_init = 0.8 - 0.6 * math.exp(-0.3 * l)
        lw = {name: _Layer(a, l) for name, a in stacked.items()}
        bound = lambda which: N_BOUNDS * l + which

        lat = _qkv_call(xl, mods, l, lambda b: b, lw, consts, tabs, n_batch, SEQ, TM)
        cx = _qkv_call(xc, mods, l, ctx_row, lw, consts, None, n_batch, CTX, CTX)
        mqt, mk, mvt, nqt, nk, nvt, dqt, dk, dvt, gqt, gk, gvt = lat
        oa = _latent_attn_call(_mla_attn_kernel, "mla_attn", bounds, bound(B_MLA), mqt, mk, mvt, cx[1], cx[2], [],
                               n_batch)
        ob = _na_attn_call(bounds, bound(B_NA), nqt, nk, nvt, cx[4], cx[5], lw["na_pair"], n_batch)
        oc = _latent_attn_call(functools.partial(_diff_attn_kernel, lam_init=lam_init), "diff_attn",
                               bounds, bound(B_DIFF), dqt, dk, dvt, cx[7], cx[8], [lw["lam"], lw["g_sub"]], n_batch)
        od = _latent_attn_call(_gqa_attn_kernel, "gqa_attn", bounds, bound(B_GQA), gqt, gk, gvt, cx[10], cx[11], [],
                               n_batch)
        if need_ctx:
            oc_all = _ctx_attn_call(cx, lw["lam"], lw["g_sub"], lam_init, n_batch)
            xc = _merge_call(xc, mods, l, ctx_row, lw["g1"], oc_all, lw["w_proj"], lw["w_branch"], lw["w_out"], TM)
            xc = _mlp_call(xc, mods, l, ctx_row, lw["g2"], lw["w_up"], lw["w_down"], TM)
        xl = _merge_call(xl, mods, l, lambda i: i // (SEQ // TM_MERGE), lw["g1"], (oa, ob, oc, od), lw["w_proj"],
                         lw["w_branch"], lw["w_out"], TM_MERGE)
        xl = _mlp_call(xl, mods, l, lat_row, lw["g2"], lw["w_up"], lw["w_down"], TM)
    return xl.reshape(n_batch, SEQ, D)
```

```python
import functools
import math
from typing import NamedTuple

import jax
import jax.numpy as jnp
from jax import lax
from jax.experimental import pallas as pl
from jax.experimental.pallas import tpu as pltpu

F32 = jnp.float32
BF16 = jnp.bfloat16

D = 1024
SEQ = 2048
CTX = 256
GRID_W = 64
N_ROWS = SEQ // GRID_W
ROPE_THETA = 10000.0
EPS = 1e-6
HEAD_DIM = 64
MLA_Q_LORA = 192
MLA_KV_LORA = 128
MLA_NOPE = 64
MLA_ROPE = 32
MLA_QK = MLA_NOPE + MLA_ROPE
NA_ROWS = 8
NA_COLS = 16
DIFF_DIM = 32
D_FF = 4 * D

LANES = 128
TM = 512
TM_MERGE = 1024
TQ = 2048
QC = 1024
NA_TQ = 256
NA_WIN = 3
NA_SAMPLES = 4
CTX_SAMPLES = 4
MASK_VALUE = -1e30
LOG2E = math.log2(math.e)
MAX_FIXED_SHIFT = 50.0

C_CQ, C_CKV, C_KR, C_NA, C_DF, C_GQ, C_GATE = 0, 192, 320, 352, 1120, 1888, 2400
Z_CQ, Z_CKV, Z_KR, Z_NA, Z_DF, Z_GQ, Z_W = 0, 256, 384, 896, 1664, 2432, 2944
G_MQ, G_MK, G_NQ, G_NK, G_DQ, G_DK, G_GQ, G_GK = range(8)
T_MLA, T_R32, T_R64 = 0, 3, 6
B_MLA, B_NA, B_DIFF, B_GQA, N_BOUNDS = 0, 1, 2, 3, 4

VMEM_LIMIT = 56 * 1024 * 1024


def _cparams(n_axes):
    return pltpu.CompilerParams(dimension_semantics=("arbitrary",) * n_axes,
                                vmem_limit_bytes=VMEM_LIMIT)


def _dot(a, b):
    return jnp.dot(a, b, preferred_element_type=F32)


class _Layer(NamedTuple):
    stacked: jax.Array
    index: int


def _operand(a):
    return a.stacked if isinstance(a, _Layer) else a


def _const_spec(a):
    if isinstance(a, _Layer):
        shape = a.stacked.shape[1:]
        return pl.BlockSpec((None,) + shape, lambda *_: (a.index,) + (0,) * len(shape),
                            pipeline_mode=pl.Buffered(1))
    return pl.BlockSpec(a.shape, lambda *_: (0,) * a.ndim, pipeline_mode=pl.Buffered(1))


def _mod_spec(layer, row_fn):
    return pl.BlockSpec((None, 1, 1, 6 * D), lambda *idx: (layer, row_fn(*idx), 0, 0))


def _modnorm(x, g, shift, scale):
    y = x * lax.rsqrt(jnp.mean(x * x, axis=-1, keepdims=True) + EPS) * g
    return y * (1.0 + scale) + shift


def _ada_kernel(c_ref, w_ref, b_ref, o_ref):
    c = c_ref[...]
    s = c * jax.nn.sigmoid(c)
    o_ref[0] = _dot(s.astype(BF16), w_ref[0].astype(BF16)) + b_ref[0]


def _ada_call(c_all, w_ada, b_ada):
    n_layers = w_ada.shape[0]
    rows = c_all.shape[0]
    bn = 1536
    return pl.pallas_call(
        _ada_kernel,
        grid=(n_layers, 6 * D // bn),
        in_specs=[pl.BlockSpec((rows, D), lambda l, j: (0, 0)),
                  pl.BlockSpec((1, D, bn), lambda l, j: (l, 0, j)),
                  pl.BlockSpec((1, 1, bn), lambda l, j: (l, 0, j))],
        out_specs=pl.BlockSpec((1, rows, bn), lambda l, j: (l, 0, j)),
        out_shape=jax.ShapeDtypeStruct((n_layers, rows, 6 * D), F32),
        compiler_params=_cparams(2),
        name="ada",
    )(c_all, w_ada, b_ada.reshape(n_layers, 1, 6 * D))


def _seg_mean_sq(t, bd, inv_d):
    return _dot((t * t).astype(BF16), bd) * inv_d


def _qkv_kernel(*refs, rope):
    (x_ref, mod_ref, g1_ref, w_ref, bd128_ref, bd64_ref, bd32_ref, bd64h_ref,
     gqa_ref, gkva_ref, wuq_ref, wuk_ref, wuv_ref, gains_ref) = refs[:14]
    tab_ref = refs[14] if rope else None
    (mqt_ref, mk_ref, mvt_ref, nqt_ref, nk_ref, nvt_ref,
     dqt_ref, dk_ref, dvt_ref, gqt_ref, gk_ref, gvt_ref) = refs[-12:]

    mod = mod_ref[0]
    xn = _modnorm(x_ref[...], g1_ref[...], mod[:, 0:D], mod[:, D:2 * D]).astype(BF16)

    def cols(lo, hi):
        return _dot(xn, w_ref[:, lo:hi])

    def put(o_ref, lo, val):
        o_ref[:, lo:lo + LANES] = val.astype(BF16)

    def put_t(o_ref, lo, val):
        o_ref[0, lo:lo + LANES, :] = val.T.astype(BF16)

    def section(t, bd, inv_d, gain_row, tab, shift, o_ref, store):
        width = t.shape[1]
        y = t * lax.rsqrt(_seg_mean_sq(t, bd, inv_d) + EPS) * gains_ref[gain_row:gain_row + 1, 0:width]
        for j in range(width // LANES):
            yc = y[:, j * LANES:(j + 1) * LANES]
            if rope and tab is not None:
                yc = (yc * tab_ref[tab] + pltpu.roll(yc, LANES - shift, 1) * tab_ref[tab + 1]
                      + pltpu.roll(yc, shift, 1) * tab_ref[tab + 2])
            store(o_ref, j * LANES, yc)

    za = cols(Z_CQ, Z_NA)
    cq = za[:, Z_CQ:Z_CQ + 256]
    cqn = cq * lax.rsqrt(jnp.sum(cq * cq, axis=-1, keepdims=True) * (1.0 / MLA_Q_LORA) + EPS) * gqa_ref[...]
    q = _dot(cqn.astype(BF16), wuq_ref[...])
    ckv = za[:, Z_CKV:Z_CKV + 128]
    kvn = (ckv * lax.rsqrt(jnp.mean(ckv * ckv, axis=-1, keepdims=True) + EPS) * gkva_ref[...]).astype(BF16)
    k = _dot(kvn, wuk_ref[...]) + za[:, Z_KR:Z_KR + 512]
    mvt_ref[0] = _dot(kvn, wuv_ref[...]).T.astype(BF16)
    bd128 = bd128_ref[...]
    for c in range(2):
        sl = slice(256 * c, 256 * c + 256)
        section(q[:, sl], bd128, 1.0 / MLA_QK, G_MQ, T_MLA, MLA_ROPE // 4, mqt_ref.at[:, sl, :], put_t)
        section(k[:, sl], bd128, 1.0 / MLA_QK, G_MK, T_MLA, MLA_ROPE // 4, mk_ref.at[:, sl], put)

    zn = cols(Z_NA, Z_DF)
    bd64 = bd64_ref[...]
    section(zn[:, 0:256], bd64, 1.0 / HEAD_DIM, G_NQ, None, 0, nqt_ref, put_t)
    section(zn[:, 256:512], bd64, 1.0 / HEAD_DIM, G_NK, None, 0, nk_ref, put)
    nvt_ref[0] = zn[:, 512:768].T.astype(BF16)

    zd = cols(Z_DF, Z_GQ)
    bd32 = bd32_ref[...]
    section(zd[:, 0:256], bd32, 1.0 / DIFF_DIM, G_DQ, T_R32, DIFF_DIM // 4, dqt_ref, put_t)
    section(zd[:, 256:512], bd32, 1.0 / DIFF_DIM, G_DK, T_R32, DIFF_DIM // 4, dk_ref, put)
    dvt_ref[0] = zd[:, 512:768].T.astype(BF16)

    zg = cols(Z_GQ, Z_W)
    section(zg[:, 0:256], bd64, 1.0 / HEAD_DIM, G_GQ, T_R64, HEAD_DIM // 4, gqt_ref, put_t)
    section(zg[:, 256:384], bd64h_ref[...], 1.0 / HEAD_DIM, G_GK, T_R64, HEAD_DIM // 4, gk_ref, put)
    gvt_ref[0] = zg[:, 384:512].T.astype(BF16)


def _qkv_call(rows, mods, layer, mod_row, lw, consts, tabs, n_batch, seq, tm):
    tiles = seq // tm
    row_map = lambda t, b: (b * tiles + t, 0)
    tok_spec = lambda w: pl.BlockSpec((tm, w), row_map)
    tr_spec = lambda w: pl.BlockSpec((1, w, tm), lambda t, b: (b, 0, t))
    tok_shape = lambda w: jax.ShapeDtypeStruct((n_batch * seq, w), BF16)
    tr_shape = lambda w: jax.ShapeDtypeStruct((n_batch, w, seq), BF16)
    small = [lw["g1"], lw["w_proj"], consts["bd128"], consts["bd64"], consts["bd32"], consts["bd64h"],
             lw["g_qa"], lw["g_kva"], lw["w_uq"], lw["w_uk"], lw["w_uv"], lw["gains"]]
    in_specs = [tok_spec(D), _mod_spec(layer, lambda t, b: mod_row(b))]
    in_specs += [_const_spec(a) for a in small]
    args = [rows, mods] + [_operand(a) for a in small]
    if tabs is not None:
        in_specs.append(pl.BlockSpec((tabs.shape[0], tm, LANES), lambda t, b: (0, t, 0)))
        args.append(tabs)
    return pl.pallas_call(
        functools.partial(_qkv_kernel, rope=tabs is not None),
        grid=(tiles, n_batch),
        in_specs=in_specs,
        out_specs=[tr_spec(512), tok_spec(512), tr_spec(256),
                   tr_spec(256), tok_spec(256), tr_spec(256),
                   tr_spec(256), tok_spec(256), tr_spec(256),
                   tr_spec(256), tok_spec(128), tr_spec(128)],
        out_shape=[tr_shape(512), tok_shape(512), tr_shape(256),
                   tr_shape(256), tok_shape(256), tr_shape(256),
                   tr_shape(256), tok_shape(256), tr_shape(256),
                   tr_shape(256), tok_shape(128), tr_shape(128)],
        compiler_params=_cparams(2),
        name="qkv",
    )(*args)


def _keep_rows(x, lo, hi):
    row = lax.broadcasted_iota(jnp.int32, x.shape, 0)
    return jnp.where(jnp.logical_and(row >= lo, row < hi), x, jnp.zeros_like(x))


def _sum_all(xs):
    return functools.reduce(lambda a, b: a + b, xs)


def _exp_scores(qt, groups, shift):
    s = [_dot(k, qt) if bias is None else _dot(k, qt) + bias for k, _, bias in groups]
    if shift is None:
        shift = functools.reduce(jnp.maximum, [jnp.max(x, axis=0, keepdims=True) for x in s])
    p = [jnp.exp2(x - shift) for x in s]
    return [x.astype(BF16) for x in p], _sum_all([jnp.sum(x, axis=0, keepdims=True) for x in p])


def _softmax_vt(qt, groups, shift):
    p, l = _exp_scores(qt, groups, shift)
    return _sum_all([_dot(g[1], x) for x, g in zip(p, groups)]) / l


def _mla_heads_t(qt_fn, groups_fn, shift):
    outs = []
    for h in range(4):
        c0 = 256 * (h // 2)
        lo = LANES * (h % 2)
        outs.append(_softmax_vt(_keep_rows(qt_fn(c0), lo, lo + LANES), groups_fn(h, c0), shift))
    return jnp.concatenate(outs, axis=0)


def _diff_heads_t(qt, groups_fn, lam, gsub_t, lam_init, shift):
    outs = []
    for h in range(4):
        groups = groups_fn(h)
        (p1, l1), (p2, l2) = [_exp_scores(_keep_rows(qt, 64 * h + DIFF_DIM * m, 64 * h + DIFF_DIM * (m + 1)),
                                          groups, shift) for m in range(2)]
        w1 = (1.0 / l1).astype(BF16)
        w2 = (lam / l2).astype(BF16)
        oh = _sum_all([_dot(g[1], a * w1 - b * w2) for a, b, g in zip(p1, p2, groups)])
        ms = jnp.mean(oh * oh, axis=0, keepdims=True)
        outs.append(oh * lax.rsqrt(ms + EPS) * gsub_t[64 * h:64 * h + 64, :])
    return jnp.concatenate(outs, axis=0) * (1.0 - lam_init)


def _gqa_heads_t(qt, groups_fn, shift):
    outs = []
    for g in range(2):
        qc = qt[LANES * g:LANES * (g + 1), :]
        for n in range(2):
            outs.append(_softmax_vt(_keep_rows(qc, 64 * n, 64 * n + 64), groups_fn(n), shift))
    return jnp.concatenate(outs, axis=0)


def _plain_heads_t(qt, groups_fn, shift):
    return jnp.concatenate([_softmax_vt(_keep_rows(qt, 64 * h, 64 * h + 64), groups_fn(h), shift) for h in range(4)],
                           axis=0)


def _diff_lambda(lam_ref, lam_init):
    lq1, lk1, lq2, lk2 = (lam_ref[i:i + 1, :] for i in range(4))
    return (jnp.exp(jnp.sum(lq1 * lk1, axis=-1, keepdims=True))
            - jnp.exp(jnp.sum(lq2 * lk2, axis=-1, keepdims=True)) + lam_init)


def _latent_attention(shift_ref, bound_index, chunk_fn):
    bound = shift_ref[bound_index]

    def run(shift):
        def step(c, carry):
            chunk_fn(pl.ds(pl.multiple_of(c * QC, QC), QC), shift)
            return carry

        lax.fori_loop(0, TQ // QC, step, 0)

    @pl.when(bound <= MAX_FIXED_SHIFT)
    def _():
        run(bound)

    @pl.when(jnp.logical_not(bound <= MAX_FIXED_SHIFT))
    def _():
        run(None)


def _mla_attn_kernel(shift_ref, qt_ref, kl_ref, vtl_ref, kc_ref, vtc_ref, o_ref, *, bound_index):
    def chunk(cols, shift):
        def groups_fn(h, c0):
            dims = slice(64 * h, 64 * h + 64)
            return [(kl_ref[:, c0:c0 + 256], vtl_ref[0, dims, :], None),
                    (kc_ref[:, c0:c0 + 256], vtc_ref[0, dims, :], None)]

        ot = _mla_heads_t(lambda c0: qt_ref[0, c0:c0 + 256, cols], groups_fn, shift)
        o_ref[cols, :] = ot.T.astype(BF16)

    _latent_attention(shift_ref, bound_index, chunk)


def _diff_attn_kernel(shift_ref, qt_ref, kl_ref, vtl_ref, kc_ref, vtc_ref, lam_ref, gsubt_ref, o_ref, *, lam_init,
                      bound_index):
    def chunk(cols, shift):
        def groups_fn(h):
            dims = slice(64 * h, 64 * h + 64)
            return [(kl_ref[...], vtl_ref[0, dims, :], None), (kc_ref[...], vtc_ref[0, dims, :], None)]

        lam = _diff_lambda(lam_ref, lam_init)
        ot = _diff_heads_t(qt_ref[0, :, cols], groups_fn, lam, gsubt_ref[...], lam_init, shift)
        o_ref[cols, :] = ot.T.astype(BF16)

    _latent_attention(shift_ref, bound_index, chunk)


def _gqa_attn_kernel(shift_ref, qt_ref, kl_ref, vtl_ref, kc_ref, vtc_ref, o_ref, *, bound_index):
    def chunk(cols, shift):
        def groups_fn(n):
            dims = slice(64 * n, 64 * n + 64)
            return [(kl_ref[...], vtl_ref[0, dims, :], None), (kc_ref[...], vtc_ref[0, dims, :], None)]

        ot = _gqa_heads_t(qt_ref[0, :, cols], groups_fn, shift)
        o_ref[cols, :] = ot.T.astype(BF16)

    _latent_attention(shift_ref, bound_index, chunk)


def _latent_attn_call(kernel, name, bounds, bound_index, qt, kl, vtl, kc, vtc, extra, n_batch):
    wq, wk, wv = qt.shape[1], kl.shape[1], vtl.shape[1]
    tiles = SEQ // TQ
    return pl.pallas_call(
        functools.partial(kernel, bound_index=bound_index),
        grid=(n_batch, tiles),
        in_specs=[pl.BlockSpec(memory_space=pltpu.SMEM),
                  pl.BlockSpec((1, wq, TQ), lambda b, t: (b, 0, t)),
                  pl.BlockSpec((SEQ, wk), lambda b, t: (b, 0)),
                  pl.BlockSpec((1, wv, SEQ), lambda b, t: (b, 0, 0)),
                  pl.BlockSpec((CTX, wk), lambda b, t: (b, 0)),
                  pl.BlockSpec((1, wv, CTX), lambda b, t: (b, 0, 0))] + [_const_spec(a) for a in extra],
        out_specs=pl.BlockSpec((TQ, 256), lambda b, t: (b * tiles + t, 0)),
        out_shape=jax.ShapeDtypeStruct((n_batch * SEQ, 256), BF16),
        compiler_params=_cparams(2),
        name=name,
    )(bounds, qt, kl, vtl, kc, vtc, *map(_operand, extra))


def _na_window(t):
    return jnp.clip(t - 1, 0, SEQ // NA_TQ - NA_WIN)


def _na_build_bias(t, pair_ref, bias_ref):
    rows_per_tile = NA_TQ // GRID_W
    k_row0 = rows_per_tile * _na_window(t)
    lane = lax.broadcasted_iota(jnp.int32, (GRID_W, LANES), 1)
    for i in range(rows_per_tile // 2):
        qr = rows_per_tile * t + 2 * i
        r0 = jnp.clip(qr - NA_ROWS // 2, 0, N_ROWS - NA_ROWS)
        r1 = jnp.clip(qr + 1 - NA_ROWS // 2, 0, N_ROWS - NA_ROWS)
        for kl in range(NA_WIN * rows_per_tile):
            kr = k_row0 + kl
            ok0 = jnp.logical_and(kr >= r0, kr < r0 + NA_ROWS).astype(jnp.int32)
            ok1 = jnp.logical_and(kr >= r1, kr < r1 + NA_ROWS).astype(jnp.int32)
            ok = jnp.where(lane < GRID_W, ok0, ok1) > 0
            d = jnp.clip(kr - qr + NA_ROWS - 1, 0, 2 * NA_ROWS - 1)
            for h in range(4):
                blk = jnp.where(ok, pair_ref[h, d], MASK_VALUE)
                bias_ref[h, GRID_W * kl:GRID_W * (kl + 1), LANES * i:LANES * (i + 1)] = blk


def _na_attn_kernel(shift_ref, qt_ref, k0_ref, k1_ref, k2_ref, kc_ref, vt0_ref, vt1_ref, vt2_ref, vtc_ref, pair_ref,
                    o_ref, bias_ref, *, bound_index):
    t = pl.program_id(0)

    @pl.when(pl.program_id(1) == 0)
    def _():
        _na_build_bias(t, pair_ref, bias_ref)

    win = ((k0_ref, vt0_ref), (k1_ref, vt1_ref), (k2_ref, vt2_ref))

    def run(shift):
        for s in range(qt_ref.shape[0]):
            def groups_fn(h):
                dims = slice(64 * h, 64 * h + 64)
                return ([(k[s], vt[s, dims, :], bias_ref[h, NA_TQ * j:NA_TQ * (j + 1), :])
                         for j, (k, vt) in enumerate(win)] + [(kc_ref[s], vtc_ref[s, dims, :], None)])

            o_ref[s] = _plain_heads_t(qt_ref[s], groups_fn, shift).T.astype(BF16)

    bound = shift_ref[bound_index]

    @pl.when(bound <= MAX_FIXED_SHIFT)
    def _():
        run(bound)

    @pl.when(jnp.logical_not(bound <= MAX_FIXED_SHIFT))
    def _():
        run(None)


def _na_attn_call(bounds, bound_index, qt, kl, vtl, kc, vtc, pair, n_batch):
    tiles = SEQ // NA_TQ
    nb = math.gcd(NA_SAMPLES, n_batch)
    kl = kl.reshape(n_batch, SEQ, 256)
    kc = kc.reshape(n_batch, CTX, 256)
    k_spec = lambda j: pl.BlockSpec((nb, NA_TQ, 256), lambda t, b: (b, _na_window(t) + j, 0))
    vt_spec = lambda j: pl.BlockSpec((nb, 256, NA_TQ), lambda t, b: (b, 0, _na_window(t) + j))
    out = pl.pallas_call(
        functools.partial(_na_attn_kernel, bound_index=bound_index),
        grid=(tiles, n_batch // nb),
        in_specs=[pl.BlockSpec(memory_space=pltpu.SMEM),
                  pl.BlockSpec((nb, 256, NA_TQ), lambda t, b: (b, 0, t)),
                  k_spec(0), k_spec(1), k_spec(2),
                  pl.BlockSpec((nb, CTX, 256), lambda t, b: (b, 0, 0)),
                  vt_spec(0), vt_spec(1), vt_spec(2),
                  pl.BlockSpec((nb, 256, CTX), lambda t, b: (b, 0, 0)),
                  _const_spec(pair)],
        out_specs=pl.BlockSpec((nb, NA_TQ, 256), lambda t, b: (b, t, 0)),
        out_shape=jax.ShapeDtypeStruct((n_batch, SEQ, 256), BF16),
        scratch_shapes=[pltpu.VMEM((4, NA_WIN * NA_TQ, NA_TQ), F32)],
        compiler_params=_cparams(2),
        name="na_attn",
    )(bounds, qt, kl, kl, kl, kc, vtl, vtl, vtl, vtc, _operand(pair))
    return out.reshape(n_batch * SEQ, 256)


def _ctx_attn_kernel(mqt_ref, mk_ref, mvt_ref, nqt_ref, nk_ref, nvt_ref, dqt_ref, dk_ref, dvt_ref,
                     gqt_ref, gk_ref, gvt_ref, lam_ref, gsubt_ref, oa_ref, ob_ref, oc_ref, od_ref, *, lam_init):
    head_dims = lambda h: slice(64 * h, 64 * h + 64)
    lam = _diff_lambda(lam_ref, lam_init)
    for s in range(mqt_ref.shape[0]):
        oa_ref[s] = _mla_heads_t(lambda c0: mqt_ref[s, c0:c0 + 256, :],
                                 lambda h, c0: [(mk_ref[s, :, c0:c0 + 256], mvt_ref[s, head_dims(h), :], None)],
                                 None).T.astype(BF16)
        ob_ref[s] = _plain_heads_t(nqt_ref[s], lambda h: [(nk_ref[s], nvt_ref[s, head_dims(h), :], None)],
                                   None).T.astype(BF16)
        oc_ref[s] = _diff_heads_t(dqt_ref[s], lambda h: [(dk_ref[s], dvt_ref[s, head_dims(h), :], None)], lam,
                                  gsubt_ref[...], lam_init, None).T.astype(BF16)
        od_ref[s] = _gqa_heads_t(gqt_ref[s], lambda n: [(gk_ref[s], gvt_ref[s, head_dims(n), :], None)],
                                 None).T.astype(BF16)


def _ctx_attn_call(qkv_c, lam, gsub, lam_init, n_batch):
    nb = math.gcd(CTX_SAMPLES, n_batch)
    arrays = [a if a.ndim == 3 else a.reshape(n_batch, CTX, a.shape[1]) for a in qkv_c]
    specs = [pl.BlockSpec((nb,) + a.shape[1:], lambda b: (b, 0, 0)) for a in arrays]
    out = jax.ShapeDtypeStruct((n_batch, CTX, 256), BF16)
    outs = pl.pallas_call(
        functools.partial(_ctx_attn_kernel, lam_init=lam_init),
        grid=(n_batch // nb,),
        in_specs=specs + [_const_spec(lam), _const_spec(gsub)],
        out_specs=[pl.BlockSpec((nb, CTX, 256), lambda b: (b, 0, 0))] * 4,
        out_shape=[out] * 4,
        compiler_params=_cparams(1),
        name="ctx_attn",
    )(*arrays, _operand(lam), _operand(gsub))
    return [o.reshape(n_batch * CTX, 256) for o in outs]


def _merge_kernel(x_ref, mod_ref, g_ref, oa_ref, ob_ref, oc_ref, od_ref, wp_ref, wb_ref, wo_ref, out_ref):
    x = x_ref[...]
    mod = mod_ref[0]
    xn = _modnorm(x, g_ref[...], mod[:, 0:D], mod[:, D:2 * D]).astype(BF16)
    acc = jnp.zeros(x.shape, F32)
    for n, o_ref in enumerate((oa_ref, ob_ref, oc_ref, od_ref)):
        gate = jax.nn.sigmoid(_dot(xn, wp_ref[:, Z_W + n * D:Z_W + (n + 1) * D]))
        acc = acc + gate * _dot(o_ref[...], wb_ref[n])
    y = _dot(acc.astype(BF16), wo_ref[...])
    out_ref[...] = x + mod[:, 2 * D:3 * D] * y


def _merge_call(rows, mods, layer, mod_row, g1, branches, w_proj, w_branch, w_out, tm):
    n_rows = rows.shape[0]
    tm = min(tm, n_rows)
    tok = lambda w: pl.BlockSpec((tm, w), lambda i: (i, 0))
    return pl.pallas_call(
        _merge_kernel,
        grid=(n_rows // tm,),
        in_specs=[tok(D), _mod_spec(layer, mod_row), _const_spec(g1)]
                 + [tok(256)] * 4 + [_const_spec(w_proj), _const_spec(w_branch), _const_spec(w_out)],
        out_specs=tok(D),
        out_shape=jax.ShapeDtypeStruct((n_rows, D), F32),
        compiler_params=_cparams(1),
        name="merge",
    )(rows, mods, _operand(g1), *branches, _operand(w_proj), _operand(w_branch), _operand(w_out))


def _mlp_kernel(x_ref, mod_ref, g_ref, wu_ref, wd_ref, out_ref):
    x = x_ref[...]
    mod = mod_ref[0]
    xn = _modnorm(x, g_ref[...], mod[:, 3 * D:4 * D], mod[:, 4 * D:5 * D]).astype(BF16)
    h = jnp.square(jnp.maximum(_dot(xn, wu_ref[...]), 0.0))
    out_ref[...] = x + mod[:, 5 * D:6 * D] * _dot(h.astype(BF16), wd_ref[...])


def _mlp_call(rows, mods, layer, mod_row, g2, w_up, w_down, tm):
    n_rows = rows.shape[0]
    tm = min(tm, n_rows)
    tok = lambda w: pl.BlockSpec((tm, w), lambda i: (i, 0))
    return pl.pallas_call(
        _mlp_kernel,
        grid=(n_rows // tm,),
        in_specs=[tok(D), _mod_spec(layer, mod_row), _const_spec(g2),
                  _const_spec(w_up), _const_spec(w_down)],
        out_specs=tok(D),
        out_shape=jax.ShapeDtypeStruct((n_rows, D), F32),
        compiler_params=_cparams(1),
        name="mlp",
    )(rows, mods, _operand(g2), _operand(w_up), _operand(w_down))


def _block_diag_ones(n, seg):
    i = jnp.arange(n) // seg
    return (i[:, None] == i[None, :]).astype(BF16)


def _rope_parts(rot_dim):
    t = jnp.arange(SEQ)
    rows, cols = t // GRID_W, t % GRID_W
    n = rot_dim // 4
    inv_freq = jnp.power(ROPE_THETA, -jnp.arange(n, dtype=F32) / n)
    ang_r = rows.astype(F32)[:, None] * inv_freq
    ang_c = cols.astype(F32)[:, None] * inv_freq
    ang = jnp.concatenate([ang_r, ang_r, ang_c, ang_c], axis=-1)
    cos, sin = jnp.cos(ang), jnp.sin(ang)
    even = (jnp.arange(rot_dim) // n) % 2 == 0
    return cos, jnp.where(even, -sin, 0.0), jnp.where(even, 0.0, sin)


def _rope_tables():
    r32, r64 = _rope_parts(MLA_ROPE), _rope_parts(HEAD_DIM)
    ones = jnp.ones((SEQ, MLA_NOPE), F32)
    zeros_n = jnp.zeros((SEQ, MLA_NOPE), F32)
    pad = jnp.zeros((SEQ, LANES - MLA_QK), F32)
    mla = [jnp.concatenate([lead, part, pad], axis=1) for lead, part in zip((ones, zeros_n, zeros_n), r32)]
    return jnp.stack(mla + [jnp.tile(a, (1, LANES // MLA_ROPE)) for a in r32]
                     + [jnp.tile(a, (1, LANES // HEAD_DIM)) for a in r64])


def _na_pair_table(rpb):
    kc = jnp.arange(GRID_W)[:, None]
    qc = jnp.arange(GRID_W)[None, :]
    onehot = (kc - qc + NA_COLS - 1)[None] == jnp.arange(2 * NA_COLS - 1)[:, None, None]
    toeplitz = jnp.einsum("lhrd,dkq->lhrkq", rpb, onehot.astype(F32), precision=lax.Precision.HIGHEST)
    c0 = jnp.clip(qc - NA_COLS // 2, 0, GRID_W - NA_COLS)
    in_window = (kc >= c0) & (kc < c0 + NA_COLS)
    masked = jnp.where(in_window, toeplitz * LOG2E, MASK_VALUE)
    ext = jnp.pad(masked, ((0, 0), (0, 0), (1, 1), (0, 0), (0, 0)), constant_values=MASK_VALUE)
    return jnp.concatenate([ext[:, :, 1:], ext[:, :, :-1]], axis=-1)


def _prepare_params(p):
    w_in = p["w_in"]
    n_layers = w_in.shape[0]
    zcol = lambda n: jnp.zeros((n_layers, D, n), F32)
    kr = w_in[:, :, C_KR:C_KR + MLA_ROPE]
    kr_slots = jnp.concatenate([zcol(MLA_NOPE), kr, zcol(LANES - MLA_QK)] * 4, axis=2)
    head_order = jnp.array([0, 2, 1, 3])
    gq_cols = w_in[:, :, C_GQ:C_GQ + 256].reshape(n_layers, D, 4, HEAD_DIM)[:, :, head_order].reshape(n_layers, D, 256)
    w_proj = jnp.concatenate([
        w_in[:, :, C_CQ:C_CQ + MLA_Q_LORA], zcol(256 - MLA_Q_LORA),
        w_in[:, :, C_CKV:C_CKV + MLA_KV_LORA],
        kr_slots,
        w_in[:, :, C_NA:C_NA + 768],
        w_in[:, :, C_DF:C_DF + 768],
        gq_cols, w_in[:, :, C_GQ + 256:C_GQ + 512],
        w_in[:, :, C_GATE:]], axis=2).astype(BF16)

    w_uq = p["w_mla_uq"].reshape(n_layers, MLA_Q_LORA, 4, MLA_QK)
    w_uq = jnp.pad(w_uq, ((0, 0), (0, 256 - MLA_Q_LORA), (0, 0), (0, LANES - MLA_QK)))
    w_ukv = p["w_mla_ukv"].reshape(n_layers, MLA_KV_LORA, 4, 2 * MLA_NOPE)
    w_uk = jnp.pad(w_ukv[..., :MLA_NOPE], ((0, 0), (0, 0), (0, 0), (0, LANES - MLA_NOPE)))

    mla_gain = lambda g: jnp.tile(jnp.pad(g, ((0, 0), (0, LANES - MLA_QK))), (1, 4))
    rep = lambda g, n: jnp.tile(g, (1, n))
    row = lambda g: jnp.pad(g, ((0, 0), (0, 512 - g.shape[1])))
    hd_scale = HEAD_DIM ** -0.5 * LOG2E
    gains = jnp.stack([
        row(mla_gain(p["g_mla_q"]) * (MLA_QK ** -0.5 * LOG2E)), row(mla_gain(p["g_mla_k"])),
        row(rep(p["g_na_q"], 4) * hd_scale), row(rep(p["g_na_k"], 4)),
        row(rep(p["g_diff_q"], 8) * (DIFF_DIM ** -0.5 * LOG2E)), row(rep(p["g_diff_k"], 8)),
        row(rep(p["g_gqa_q"], 4) * hd_scale), row(rep(p["g_gqa_k"], 2))], axis=1)

    def score_bound(dim, g_q, g_k):
        return LOG2E * dim ** 0.5 * jnp.max(jnp.abs(g_q), axis=-1) * jnp.max(jnp.abs(g_k), axis=-1)

    bounds = jnp.stack([
        score_bound(MLA_QK, p["g_mla_q"], p["g_mla_k"]),
        score_bound(HEAD_DIM, p["g_na_q"], p["g_na_k"]) + LOG2E * jnp.max(jnp.abs(p["na_rpb"]), axis=(1, 2, 3)),
        score_bound(DIFF_DIM, p["g_diff_q"], p["g_diff_k"]),
        score_bound(HEAD_DIM, p["g_gqa_q"], p["g_gqa_k"])], axis=1).astype(F32).reshape(n_layers * N_BOUNDS)

    w_branch = p["w_branch"]
    wb_gqa = w_branch[:, 3].reshape(n_layers, 4, HEAD_DIM, D)[:, head_order].reshape(n_layers, 1, 256, D)
    w_branch = jnp.concatenate([w_branch[:, :3], wb_gqa], axis=1).astype(BF16)

    return dict(
        g1=p["g_norm1"][:, None], g2=p["g_norm2"][:, None],
        w_proj=w_proj,
        g_qa=jnp.pad(p["g_mla_qa"], ((0, 0), (0, 256 - MLA_Q_LORA)))[:, None], g_kva=p["g_mla_kva"][:, None],
        w_uq=w_uq.reshape(n_layers, 256, 512).astype(BF16), w_uk=w_uk.reshape(n_layers, MLA_KV_LORA, 512).astype(BF16),
        w_uv=w_ukv[..., MLA_NOPE:].reshape(n_layers, MLA_KV_LORA, 256).astype(BF16),
        gains=gains, bounds=bounds, na_pair=_na_pair_table(p["na_rpb"]),
        lam=jnp.stack([p["diff_lq1"], p["diff_lk1"], p["diff_lq2"], p["diff_lk2"]], axis=1),
        g_sub=rep(p["g_diff_sub"], 4)[:, :, None],
        w_branch=w_branch, w_out=p["w_out"].astype(BF16),
        w_up=p["w_up"].astype(BF16), w_down=p["w_down"].astype(BF16),
    )


def kernel(x, c, ctx, c_ctx, w_ada, b_ada, g_norm1, g_norm2, w_in, g_mla_qa, w_mla_uq, g_mla_kva, w_mla_ukv, g_mla_q, g_mla_k, g_na_q, g_na_k, na_rpb, g_diff_q, g_diff_k, diff_lq1, diff_lk1, diff_lq2, diff_lk2, g_diff_sub, g_gqa_q, g_gqa_k, w_branch, w_out, w_up, w_down):
    p = dict(w_in=w_in, g_norm1=g_norm1, g_norm2=g_norm2, g_mla_qa=g_mla_qa, w_mla_uq=w_mla_uq,
             g_mla_kva=g_mla_kva, w_mla_ukv=w_mla_ukv, g_mla_q=g_mla_q, g_mla_k=g_mla_k,
             g_na_q=g_na_q, g_na_k=g_na_k, na_rpb=na_rpb, g_diff_q=g_diff_q, g_diff_k=g_diff_k,
             diff_lq1=diff_lq1, diff_lk1=diff_lk1, diff_lq2=diff_lq2, diff_lk2=diff_lk2,
             g_diff_sub=g_diff_sub, g_gqa_q=g_gqa_q, g_gqa_k=g_gqa_k, w_branch=w_branch,
             w_out=w_out, w_up=w_up, w_down=w_down)
    n_batch = x.shape[0]
    depth = w_ada.shape[0]
    assert x.shape[1:] == (SEQ, D) and ctx.shape[1:] == (CTX, D)

    mod_rows = -(-(n_batch + 1) // 8) * 8
    c_all = jnp.concatenate([c, c_ctx[None], jnp.zeros((mod_rows - n_batch - 1, D), F32)], axis=0)
    mods = _ada_call(c_all, w_ada, b_ada).reshape(depth, mod_rows, 1, 6 * D)
    lat_tiles = SEQ // TM
    lat_row = lambda i: i // lat_tiles
    ctx_row = lambda i: n_batch

    consts = dict(bd128=_block_diag_ones(256, 128), bd64=_block_diag_ones(256, 64),
                  bd32=_block_diag_ones(256, 32), bd64h=_block_diag_ones(128, 64))
    tabs = _rope_tables()
    stacked = _prepare_params(p)
    bounds = stacked.pop("bounds")

    xl = x.reshape(n_batch * SEQ, D)
    xc = ctx.reshape(n_batch * CTX, D)
    for l in range(depth):
        need_ctx = l < depth - 1
        lam_init = 0.8 - 0.6 * math.exp(-0.3 * l)
        lw = {name: _Layer(a, l) for name, a in stacked.items()}
        bound = lambda which: N_BOUNDS * l + which

        lat = _qkv_call(xl, mods, l, lambda b: b, lw, consts, tabs, n_batch, SEQ, TM)
        cx = _qkv_call(xc, mods, l, ctx_row, lw, consts, None, n_batch, CTX, CTX)
        mqt, mk, mvt, nqt, nk, nvt, dqt, dk, dvt, gqt, gk, gvt = lat
        oa = _latent_attn_call(_mla_attn_kernel, "mla_attn", bounds, bound(B_MLA), mqt, mk, mvt, cx[1], cx[2], [],
                               n_batch)
        ob = _na_attn_call(bounds, bound(B_NA), nqt, nk, nvt, cx[4], cx[5], lw["na_pair"], n_batch)
        oc = _latent_attn_call(functools.partial(_diff_attn_kernel, lam_init=lam_init), "diff_attn",
                               bounds, bound(B_DIFF), dqt, dk, dvt, cx[7], cx[8], [lw["lam"], lw["g_sub"]], n_batch)
        od = _latent_attn_call(_gqa_attn_kernel, "gqa_attn", bounds, bound(B_GQA), gqt, gk, gvt, cx[10], cx[11], [],
                               n_batch)
        if need_ctx:
            oc_all = _ctx_attn_call(cx, lw["lam"], lw["g_sub"], lam_init, n_batch)
            xc = _merge_call(xc, mods, l, ctx_row, lw["g1"], oc_all, lw["w_proj"], lw["w_branch"], lw["w_out"], TM)
            xc = _mlp_call(xc, mods, l, ctx_row, lw["g2"], lw["w_up"], lw["w_down"], TM)
        xl = _merge_call(xl, mods, l, lambda i: i // (SEQ // TM_MERGE), lw["g1"], (oa, ob, oc, od), lw["w_proj"],
                         lw["w_branch"], lw["w_out"], TM_MERGE)
        xl = _mlp_call(xl, mods, l, lat_row, lw["g2"], lw["w_up"], lw["w_down"], TM)
    return xl.reshape(n_batch, SEQ, D)
```
